```python
import math
import jax
import jax.numpy as jnp
from jax import lax
import numpy as np

D_MODEL = 2048
BATCH = 4
SEQ = 2048
DEPTH = 2

PLE_DIM = 256
CHUNK = 64
N_EVEN = (DEPTH + 1) // 2
N_ODD = DEPTH // 2
MIX_WIDTH = D_MODEL
GROUP_WIDTH = MIX_WIDTH // 2

HGRN_HEAD_DIM = 128
HGRN_HEADS = GROUP_WIDTH // HGRN_HEAD_DIM

GLA_HEADS = 4
GLA_DV = GROUP_WIDTH // GLA_HEADS
GLA_DK = GLA_DV // 2
GLA_K_WIDTH = GLA_HEADS * GLA_DK
GLA_GATE_RANK = 16
GLA_GATE_TAU = 16.0

RWKV_HEAD_DIM = 64
RWKV_HEADS = GROUP_WIDTH // RWKV_HEAD_DIM
RWKV_DECAY_LORA = max(32, int(round(1.8 * D_MODEL ** 0.5 / 32)) * 32)
RWKV_AAA_LORA = max(32, int(round(1.8 * D_MODEL ** 0.5 / 32)) * 32)
RWKV_GATE_LORA = max(32, int(round(0.6 * D_MODEL ** 0.8 / 32)) * 32)
RWKV_GN_EPS = 64e-5

GDN_HEAD_DIM = 128
GDN_HEADS = GROUP_WIDTH // GDN_HEAD_DIM
GDN_CONV = 4

FFN_DIM = 5632
FFN_CONV = 3

ALPHA = (2 * DEPTH) ** 0.25
BETA = (8 * DEPTH) ** -0.25
LN_EPS = 1e-5
RMS_EPS = 1e-6
L2_EPS = 1e-6

L0_SPLITS = (GROUP_WIDTH, GROUP_WIDTH, GROUP_WIDTH, GROUP_WIDTH,
             GLA_K_WIDTH, GLA_K_WIDTH, GROUP_WIDTH, GLA_GATE_RANK, GROUP_WIDTH)
L0_COLS = sum(L0_SPLITS)
RWKV_SPLITS = (GROUP_WIDTH, GROUP_WIDTH, GROUP_WIDTH, RWKV_DECAY_LORA, RWKV_AAA_LORA, RWKV_GATE_LORA)
RWKV_COLS = sum(RWKV_SPLITS)
GDN_SPLITS = (3 * GROUP_WIDTH, GROUP_WIDTH, GDN_HEADS, GDN_HEADS)
GDN_COLS = sum(GDN_SPLITS)
L1_COLS = RWKV_COLS + GDN_COLS

kernel_name = "hybrid_hgrn2_gla_rwkv7_gdn_deepnorm"


def split_cols(t, sizes):
    idx = [int(s) for s in np.cumsum(sizes)[:-1]]
    return jnp.split(t, idx, axis=-1)


def to_heads(t, n_heads):
    b, s, _ = t.shape
    return t.reshape(b, s, n_heads, -1).transpose(0, 2, 1, 3)


def from_heads(t):
    b, h, s, d = t.shape
    return t.transpose(0, 2, 1, 3).reshape(b, s, h * d)


def layer_norm(x, w, b):
    xf = x.astype(jnp.float32)
    mu = jnp.mean(xf, -1, keepdims=True)
    var = jnp.mean(jnp.square(xf - mu), -1, keepdims=True)
    return ((xf - mu) * lax.rsqrt(var + LN_EPS) * w + b).astype(x.dtype)


def head_rmsnorm(o, w):
    o = o * lax.rsqrt(jnp.mean(jnp.square(o), -1, keepdims=True) + RMS_EPS)
    return from_heads(o) * w


def l2norm(t):
    return t * lax.rsqrt(jnp.sum(jnp.square(t), -1, keepdims=True) + L2_EPS)


def causal_dwconv(x, w):
    width = w.shape[0]
    return lax.conv_general_dilated(
        x, w.astype(x.dtype)[:, None, :], window_strides=(1,), padding=[(width - 1, 0)],
        dimension_numbers=("NWC", "WIO", "NWC"), feature_group_count=x.shape[-1])


def chunk_gla(q, k, v, g):
    B, H, T, K = q.shape
    V = v.shape[-1]
    N = T // CHUNK
    q, k, g = (t.reshape(B, H, N, CHUNK, K) for t in (q, k, g))
    v = v.reshape(B, H, N, CHUNK, V)
    G = jnp.cumsum(g, axis=3)
    G_ref = G[:, :, :, CHUNK // 2:CHUNK // 2 + 1]
    G_last = G[:, :, :, -1:]
    causal = jnp.tril(jnp.ones((CHUNK, CHUNK), dtype=bool))
    a_qk = jnp.einsum("bhnik,bhnjk->bhnij", q * jnp.exp(G - G_ref), k * jnp.exp(G_ref - G))
    o_intra = jnp.einsum("bhnij,bhnjv->bhniv", jnp.where(causal, a_qk, 0.0), v)
    u = jnp.einsum("bhnck,bhncv->bhnkv", k * jnp.exp(G_last - G), v)
    d = jnp.exp(G_last[:, :, :, 0])

    def step(S, inp):
        d_n, u_n = inp
        return S * d_n[..., None] + u_n, S

    _, S_prev = lax.scan(step, jnp.zeros((B, H, K, V), q.dtype),
                         (jnp.moveaxis(d, 2, 0), jnp.moveaxis(u, 2, 0)))
    o_inter = jnp.einsum("bhnik,bhnkv->bhniv", q * jnp.exp(G), jnp.moveaxis(S_prev, 0, 2))
    return (o_intra + o_inter).reshape(B, H, T, V)


def chunk_gated_delta(q, k, v, g, beta):
    B, H, T, K = q.shape
    V = v.shape[-1]
    N = T // CHUNK
    q, k = (t.reshape(B, H, N, CHUNK, K) for t in (q, k))
    v = v.reshape(B, H, N, CHUNK, V)
    g, beta = (t.reshape(B, H, N, CHUNK) for t in (g, beta))
    G = jnp.cumsum(g, axis=-1)
    incl = jnp.tril(jnp.ones((CHUNK, CHUNK), dtype=bool))
    strict = jnp.tril(jnp.ones((CHUNK, CHUNK), dtype=bool), k=-1)
    diff = G[..., :, None] - G[..., None, :]
    gam = jnp.where(incl, jnp.exp(jnp.where(incl, diff, 0.0)), 0.0)
    kb = k * beta[..., None]
    L = jnp.where(strict, jnp.einsum("bhnik,bhnjk->bhnij", kb, k) * gam, 0.0)
    eye = jnp.eye(CHUNK, dtype=q.dtype)
    Tinv = lax.linalg.triangular_solve(L + eye, jnp.broadcast_to(eye, L.shape), left_side=True,
                                       lower=True, unit_diagonal=True)
    u = jnp.einsum("bhnij,bhnjv->bhniv", Tinv, v * beta[..., None])
    w = jnp.einsum("bhnij,bhnjk->bhnik", Tinv, kb * jnp.exp(G)[..., None])
    a_qk = jnp.where(incl, jnp.einsum("bhnik,bhnjk->bhnij", q, k) * gam, 0.0)
    qg = q * jnp.exp(G)[..., None]
    kg = k * jnp.exp(G[..., -1:] - G)[..., None]
    d_last = jnp.exp(G[..., -1])

    def step(S, inp):
        u_n, w_n, aqk_n, qg_n, kg_n, d_n = inp
        v_new = u_n - jnp.einsum("bhck,bhkv->bhcv", w_n, S)
        o_n = jnp.einsum("bhck,bhkv->bhcv", qg_n, S) + jnp.einsum("bhij,bhjv->bhiv", aqk_n, v_new)
        S = S * d_n[..., None, None] + jnp.einsum("bhck,bhcv->bhkv", kg_n, v_new)
        return S, o_n

    xs = tuple(jnp.moveaxis(t, 2, 0) for t in (u, w, a_qk, qg, kg, d_last))
    _, o = lax.scan(step, jnp.zeros((B, H, K, V), q.dtype), xs)
    return jnp.moveaxis(o, 0, 2).reshape(B, H, T, V)


def rwkv7_scan(r, w, k, v, a, b):
    B, T, H, N = r.shape

    def step(S, inp):
        r_t, w_t, k_t, v_t, a_t, b_t = inp
        sa = jnp.einsum("bhvk,bhk->bhv", S, a_t)
        S = S * w_t[:, :, None, :] + sa[..., None] * b_t[:, :, None, :] + v_t[..., None] * k_t[:, :, None, :]
        return S, jnp.einsum("bhvk,bhk->bhv", S, r_t)

    xs = tuple(jnp.moveaxis(t, 1, 0) for t in (r, w, k, v, a, b))
    _, y = lax.scan(step, jnp.zeros((B, H, N, N), r.dtype), xs)
    return jnp.moveaxis(y, 0, 1)


def mixer_hgrn2_gla(x, lb, w_in, gla_w2, gla_b, hgrn_norm, gla_norm, w_out):
    proj = (x @ w_in).astype(jnp.float32)
    hq, hf, hi, hg, gq, gk, gv, gd, gr = split_cols(proj, L0_SPLITS)
    f = lb + (1.0 - lb) * jax.nn.sigmoid(hf)
    o_h = chunk_gla(to_heads(jax.nn.silu(hq), HGRN_HEADS), to_heads(1.0 - f, HGRN_HEADS),
                    to_heads(hi, HGRN_HEADS), to_heads(jnp.log(f), HGRN_HEADS))
    o_h = head_rmsnorm(o_h, hgrn_norm) * jax.nn.silu(hg)
    g_log = jax.nn.log_sigmoid(gd @ gla_w2 + gla_b) / GLA_GATE_TAU
    o_g = chunk_gla(to_heads(gq * GLA_DK ** -0.5, GLA_HEADS), to_heads(gk, GLA_HEADS),
                    to_heads(gv, GLA_HEADS), to_heads(g_log, GLA_HEADS))
    o_g = head_rmsnorm(o_g, gla_norm) * jax.nn.silu(gr)
    return jnp.concatenate([o_h, o_g], axis=-1).astype(x.dtype) @ w_out


def mixer_rwkv7_gdn(x, w_in, mu, w0, w2, a0, a2, g2, k_k, k_a, r_k, gn_w, gn_b,
                    conv_w, a_log, dt_bias, gdn_norm, w_out):
    B, T, _ = x.shape
    proj = (x @ w_in).astype(jnp.float32)
    pr, pd = proj[..., :RWKV_COLS], proj[..., RWKV_COLS:]
    prev = jnp.pad(pr, ((0, 0), (1, 0), (0, 0)))[:, :-1]
    pr = pr + (prev - pr) * mu
    r, k, v, wd, ad, gd = split_cols(pr, RWKV_SPLITS)
    decay = jnp.exp(-jnp.exp(-jax.nn.softplus(-(w0 + jnp.tanh(wd) @ w2)) - 0.5))
    a = jax.nn.sigmoid(a0 + ad @ a2)
    gate = jax.nn.sigmoid(gd) @ g2
    hs = lambda t: t.reshape(B, T, RWKV_HEADS, RWKV_HEAD_DIM)
    kk = l2norm(hs(k * k_k))
    k = k * (1.0 + (a - 1.0) * k_a)
    rh, kh, vh = hs(r), hs(k), hs(v)
    y = rwkv7_scan(rh, hs(decay), kh, vh, -kk, kk * hs(a))
    mean = jnp.mean(y, -1, keepdims=True)
    var = jnp.mean(jnp.square(y - mean), -1, keepdims=True)
    y = ((y - mean) * lax.rsqrt(var + RWKV_GN_EPS)).reshape(B, T, GROUP_WIDTH) * gn_w + gn_b
    bonus = (jnp.sum(rh * kh * r_k, -1, keepdims=True) * vh).reshape(B, T, GROUP_WIDTH)
    o_c = (y + bonus) * gate
    qkv_raw, dz, da, db = split_cols(pd, GDN_SPLITS)
    qkv = jax.nn.silu(causal_dwconv(qkv_raw, conv_w.astype(jnp.float32)))
    dq, dk, dv = jnp.split(qkv, 3, axis=-1)
    qh = l2norm(to_heads(dq, GDN_HEADS)) * GDN_HEAD_DIM ** -0.5
    kh2 = l2norm(to_heads(dk, GDN_HEADS))
    vh2 = to_heads(dv, GDN_HEADS)
    beta = jax.nn.sigmoid(db).transpose(0, 2, 1)
    g = (-jnp.exp(a_log) * jax.nn.softplus(da + dt_bias)).transpose(0, 2, 1)
    o_d = chunk_gated_delta(qh, kh2, vh2, g, beta)
    o_d = head_rmsnorm(o_d, gdn_norm) * jax.nn.silu(dz)
    return jnp.concatenate([o_c, o_d], axis=-1).astype(x.dtype) @ w_out


def conv_ffn(x, w_up, conv_w, w_down):
    u = causal_dwconv(x @ w_up, conv_w)
    gate, val = jnp.split(u, 2, axis=-1)
    return (jax.nn.silu(gate) * val) @ w_down


def setup_inputs(seed: int = 0) -> dict:
    key = jax.random.key(seed)
    ks = iter(jax.random.split(key, 48))
    f32 = jnp.float32
    nrm = lambda shape, s: jax.random.normal(next(ks), shape, f32) * s
    gain = lambda shape: 1.0 + 0.02 * jax.random.normal(next(ks), shape, f32)
    D, W = D_MODEL, GROUP_WIDTH
    dt = jnp.exp(jax.random.uniform(next(ks), (N_ODD, GDN_HEADS), f32, math.log(1e-3), math.log(1e-1)))
    return {
        "x": nrm((BATCH, SEQ, D), 1.0),
        "p": nrm((DEPTH, BATCH, SEQ, PLE_DIM), 1.0),
        "hgrn_lb_logits": nrm((DEPTH + 1, W), 0.1),
        "e_w_in": nrm((N_EVEN, D, L0_COLS), D ** -0.5),
        "e_gla_w2": nrm((N_EVEN, GLA_GATE_RANK, GLA_K_WIDTH), GLA_GATE_RANK ** -0.5),
        "e_gla_b": nrm((N_EVEN, GLA_K_WIDTH), 0.1),
        "e_hgrn_norm": gain((N_EVEN, W)),
        "e_gla_norm": gain((N_EVEN, W)),
        "e_w_out": nrm((N_EVEN, MIX_WIDTH, D), MIX_WIDTH ** -0.5 * BETA),
        "o_w_in": nrm((N_ODD, D, L1_COLS), D ** -0.5),
        "o_rwkv_mu": jax.random.uniform(next(ks), (N_ODD, RWKV_COLS), f32),
        "o_rwkv_w0": jax.random.uniform(next(ks), (N_ODD, W), f32, -6.0, 1.0),
        "o_rwkv_w2": nrm((N_ODD, RWKV_DECAY_LORA, W), RWKV_DECAY_LORA ** -0.5),
        "o_rwkv_a0": nrm((N_ODD, W), 0.1),
        "o_rwkv_a2": nrm((N_ODD, RWKV_AAA_LORA, W), RWKV_AAA_LORA ** -0.5),
        "o_rwkv_g2": nrm((N_ODD, RWKV_GATE_LORA, W), RWKV_GATE_LORA ** -0.5),
        "o_rwkv_kk": 0.85 + nrm((N_ODD, W), 0.02),
        "o_rwkv_ka": gain((N_ODD, W)),
        "o_rwkv_rk": nrm((N_ODD, RWKV_HEADS, RWKV_HEAD_DIM), 0.1),
        "o_rwkv_gn_w": gain((N_ODD, W)),
        "o_rwkv_gn_b": nrm((N_ODD, W), 0.02),
        "o_gdn_conv": nrm((N_ODD, GDN_CONV, 3 * W), GDN_CONV ** -0.5),
        "o_gdn_a_log": jnp.log(jax.random.uniform(next(ks), (N_ODD, GDN_HEADS), f32, 1.0, 16.0)),
        "o_gdn_dt_bias": dt + jnp.log(-jnp.expm1(-dt)),
        "o_gdn_norm": gain((N_ODD, W)),
        "o_w_out": nrm((N_ODD, MIX_WIDTH, D), MIX_WIDTH ** -0.5 * BETA),
        "ln_mix_w": gain((DEPTH, D)),
        "ln_mix_b": nrm((DEPTH, D), 0.02),
        "ln_ffn_w": gain((DEPTH, D)),
        "ln_ffn_b": nrm((DEPTH, D), 0.02),
        "ffn_w_up": nrm((DEPTH, D, 2 * FFN_DIM), D ** -0.5),
        "ffn_conv": nrm((DEPTH, FFN_CONV, 2 * FFN_DIM), FFN_CONV ** -0.5),
        "ffn_w_down": nrm((DEPTH, FFN_DIM, D), FFN_DIM ** -0.5 * BETA),
        "ple_w_proj": nrm((DEPTH, PLE_DIM, D), PLE_DIM ** -0.5),
        "ple_w_gate": nrm((DEPTH, D, D), D ** -0.5),
    }


def reference(x, p, hgrn_lb_logits, e_w_in, e_gla_w2, e_gla_b, e_hgrn_norm, e_gla_norm, e_w_out,
              o_w_in, o_rwkv_mu, o_rwkv_w0, o_rwkv_w2, o_rwkv_a0, o_rwkv_a2, o_rwkv_g2, o_rwkv_kk,
              o_rwkv_ka, o_rwkv_rk, o_rwkv_gn_w, o_rwkv_gn_b, o_gdn_conv, o_gdn_a_log, o_gdn_dt_bias,
              o_gdn_norm, o_w_out, ln_mix_w, ln_mix_b, ln_ffn_w, ln_ffn_b, ffn_w_up, ffn_conv,
              ffn_w_down, ple_w_proj, ple_w_gate):
    lb_table = jnp.cumsum(jax.nn.softmax(hgrn_lb_logits.astype(jnp.float32), axis=0), axis=0)
    for layer in range(DEPTH):
        j = layer // 2
        if layer % 2 == 0:
            mix = mixer_hgrn2_gla(x, lb_table[layer], e_w_in[j], e_gla_w2[j], e_gla_b[j],
                                  e_hgrn_norm[j], e_gla_norm[j], e_w_out[j])
        else:
            mix = mixer_rwkv7_gdn(x, o_w_in[j], o_rwkv_mu[j], o_rwkv_w0[j], o_rwkv_w2[j], o_rwkv_a0[j],
                                  o_rwkv_a2[j], o_rwkv_g2[j], o_rwkv_kk[j], o_rwkv_ka[j], o_rwkv_rk[j],
                                  o_rwkv_gn_w[j], o_rwkv_gn_b[j], o_gdn_conv[j], o_gdn_a_log[j],
                                  o_gdn_dt_bias[j], o_gdn_norm[j], o_w_out[j])
        x = layer_norm(ALPHA * x + mix, ln_mix_w[layer], ln_mix_b[layer])
        x = layer_norm(ALPHA * x + conv_ffn(x, ffn_w_up[layer], ffn_conv[layer], ffn_w_down[layer]),
                       ln_ffn_w[layer], ln_ffn_b[layer])
        x = x + jax.nn.sigmoid(x @ ple_w_gate[layer]) * (p[layer] @ ple_w_proj[layer])
    return x
```

```python
import functools

import jax
import jax.numpy as jnp
from jax import lax
from jax.experimental import pallas as pl
from jax.experimental.pallas import tpu as pltpu

F32 = jnp.float32
BF16 = jnp.bfloat16
HIGHEST = lax.Precision.HIGHEST

NN = (((1,), (0,)), ((), ()))
NT = (((1,), (1,)), ((), ()))
TN = (((0,), (0,)), ((), ()))

D_MODEL = 2048
GROUP_W = 1024
CHUNK = 64
LANES = 128
SUBLANES = 8
HGRN_HEADS = 8
GLA_HEADS = 4
GLA_DK = 128
GLA_DV = 256
GLA_TAU = 16.0
RWKV_HEAD = 64
RWKV_GN_EPS = 64e-5
GDN_HEADS = 8
GDN_HEAD = 128
FFN_DIM = 5632
LN_EPS = 1e-5
RMS_EPS = 1e-6
L2_EPS = 1e-6

VMEM_LIMIT = 56 * 1024 * 1024


def _mm(a, b, dims=NN):
    return lax.dot_general(a.astype(BF16), b.astype(BF16), dims, preferred_element_type=F32)


def _mm_hi(a, b, dims=NN):
    return lax.dot_general(a, b, dims, precision=HIGHEST, preferred_element_type=F32)


def _sigmoid(x):
    return 1.0 / (1.0 + jnp.exp(-x))


def _silu(x):
    return x * _sigmoid(x)


def _softplus(x):
    return jnp.maximum(x, 0.0) + jnp.log(1.0 + jnp.exp(-jnp.abs(x)))


def _iota2(shape, dim):
    return lax.broadcasted_iota(jnp.int32, shape, dim)


def _cparams(sem):
    return pltpu.CompilerParams(dimension_semantics=sem, vmem_limit_bytes=VMEM_LIMIT)


def _matmul_kernel(x_ref, w_ref, o_ref):
    o_ref[...] = _mm(x_ref[...], w_ref[...]).astype(o_ref.dtype)


def _matmul(x, w, out_dtype, tm, tn):
    m, k = x.shape
    n = w.shape[1]
    return pl.pallas_call(
        _matmul_kernel,
        out_shape=jax.ShapeDtypeStruct((m, n), out_dtype),
        grid=(m // tm, n // tn),
        in_specs=[pl.BlockSpec((tm, k), lambda i, j: (i, 0)),
                  pl.BlockSpec((k, tn), lambda i, j: (0, j))],
        out_specs=pl.BlockSpec((tm, tn), lambda i, j: (i, j)),
        compiler_params=_cparams(("parallel", "parallel")),
        name="proj_matmul",
    )(x, w)


def _layer_norm_rows(y, w, b):
    mu = jnp.mean(y, axis=-1, keepdims=True)
    yc = y - mu
    var = jnp.mean(yc * yc, axis=-1, keepdims=True)
    return yc * lax.rsqrt(var + LN_EPS) * w + b


def _mix_out_kernel(oa_ref, ob_ref, w_ref, x_ref, lw_ref, lb_ref, o32_ref, o16_ref, *, alpha):
    half = oa_ref.shape[1]
    acc = _mm(oa_ref[...], w_ref[0:half, :]) + _mm(ob_ref[...], w_ref[half:2 * half, :])
    y = _layer_norm_rows(alpha * x_ref[...] + acc, lw_ref[...], lb_ref[...])
    o32_ref[...] = y
    o16_ref[...] = y.astype(BF16)


def _mix_out(oa, ob, w, x, lw, lb, alpha, tm=512):
    m, half = oa.shape
    n = w.shape[1]
    row = lambda i: (i, 0)
    fixed = lambda i: (0, 0)
    return pl.pallas_call(
        functools.partial(_mix_out_kernel, alpha=alpha),
        out_shape=(jax.ShapeDtypeStruct((m, n), F32), jax.ShapeDtypeStruct((m, n), BF16)),
        grid=(m // tm,),
        in_specs=[pl.BlockSpec((tm, half), row), pl.BlockSpec((tm, half), row),
                  pl.BlockSpec((2 * half, n), fixed), pl.BlockSpec((tm, n), row),
                  pl.BlockSpec((1, n), fixed), pl.BlockSpec((1, n), fixed)],
        out_specs=(pl.BlockSpec((tm, n), row), pl.BlockSpec((tm, n), row)),
        compiler_params=_cparams(("parallel",)),
        name="mix_out_ln",
    )(oa, ob, w, x, lw, lb)


def _ffn_up_kernel(x_ref, xh_ref, wg_ref, wv_ref, cg_ref, cv_ref, o_ref, *, tiles_per_seq):
    first = (pl.program_id(0) % tiles_per_seq) == 0
    tm = x_ref.shape[0]

    def conv(w_ref, c_ref):
        u = _mm(x_ref[...], w_ref[...])
        halo = _mm(xh_ref[...], w_ref[...])
        halo = jnp.where(first, 0.0, halo)
        ext = jnp.concatenate([halo, u], axis=0)
        p1 = pltpu.roll(ext, 1, 0)[SUBLANES:SUBLANES + tm]
        p2 = pltpu.roll(ext, 2, 0)[SUBLANES:SUBLANES + tm]
        c = c_ref[...]
        return u * c[2:3] + p1 * c[1:2] + p2 * c[0:1]

    g = conv(wg_ref, cg_ref)
    v = conv(wv_ref, cv_ref)
    o_ref[...] = (_silu(g) * v).astype(o_ref.dtype)


def _ffn_up(xb, w_up, conv_w, seq, tm=1024, tf=512):
    m, k = xb.shape
    f = w_up.shape[1] // 2
    nf = f // tf
    hb = tm // SUBLANES
    return pl.pallas_call(
        functools.partial(_ffn_up_kernel, tiles_per_seq=seq // tm),
        out_shape=jax.ShapeDtypeStruct((m, f), BF16),
        grid=(m // tm, nf),
        in_specs=[pl.BlockSpec((tm, k), lambda i, j: (i, 0)),
                  pl.BlockSpec((SUBLANES, k), lambda i, j: (jnp.maximum(i * hb - 1, 0), 0)),
                  pl.BlockSpec((k, tf), lambda i, j: (0, j)),
                  pl.BlockSpec((k, tf), lambda i, j: (0, nf + j)),
                  pl.BlockSpec((3, tf), lambda i, j: (0, j)),
                  pl.BlockSpec((3, tf), lambda i, j: (0, nf + j))],
        out_specs=pl.BlockSpec((tm, tf), lambda i, j: (i, j)),
        compiler_params=_cparams(("parallel", "parallel")),
        name="ffn_up_conv_gate",
    )(xb, xb, w_up, w_up, conv_w, conv_w)


def _ffn_down_kernel(h_ref, w_ref, x_ref, lw_ref, lb_ref, o32_ref, o16_ref, acc_ref, *, alpha, nk):
    k = pl.program_id(1)

    @pl.when(k == 0)
    def _():
        acc_ref[...] = jnp.zeros_like(acc_ref)

    acc_ref[...] += _mm(h_ref[...], w_ref[...])

    @pl.when(k == nk - 1)
    def _():
        y = _layer_norm_rows(alpha * x_ref[...] + acc_ref[...], lw_ref[...], lb_ref[...])
        o32_ref[...] = y
        o16_ref[...] = y.astype(BF16)


def _ffn_down(h, w, x, lw, lb, alpha, tm=512, tk=1408):
    m, kdim = h.shape
    n = w.shape[1]
    nk = kdim // tk
    return pl.pallas_call(
        functools.partial(_ffn_down_kernel, alpha=alpha, nk=nk),
        out_shape=(jax.ShapeDtypeStruct((m, n), F32), jax.ShapeDtypeStruct((m, n), BF16)),
        grid=(m // tm, nk),
        in_specs=[pl.BlockSpec((tm, tk), lambda i, k: (i, k)),
                  pl.BlockSpec((tk, n), lambda i, k: (k, 0)),
                  pl.BlockSpec((tm, n), lambda i, k: (i, 0)),
                  pl.BlockSpec((1, n), lambda i, k: (0, 0)),
                  pl.BlockSpec((1, n), lambda i, k: (0, 0))],
        out_specs=(pl.BlockSpec((tm, n), lambda i, k: (i, 0)),
                   pl.BlockSpec((tm, n), lambda i, k: (i, 0))),
        scratch_shapes=[pltpu.VMEM((tm, n), F32)],
        compiler_params=_cparams(("parallel", "arbitrary")),
        name="ffn_down_ln",
    )(h, w, x, lw, lb)


def _ple_kernel(xb_ref, wg_ref, p_ref, wp_ref, x_ref, o32_ref, o16_ref):
    gate = _sigmoid(_mm(xb_ref[...], wg_ref[...]))
    y = x_ref[...] + gate * _mm(p_ref[...], wp_ref[...])
    o32_ref[...] = y
    o16_ref[...] = y.astype(BF16)


def _ple(xb, x, pb, wg, wp, tm=1024, tn=512):
    m, k = xb.shape
    n = wg.shape[1]
    kp = pb.shape[1]
    return pl.pallas_call(
        _ple_kernel,
        out_shape=(jax.ShapeDtypeStruct((m, n), F32), jax.ShapeDtypeStruct((m, n), BF16)),
        grid=(m // tm, n // tn),
        in_specs=[pl.BlockSpec((tm, k), lambda i, j: (i, 0)),
                  pl.BlockSpec((k, tn), lambda i, j: (0, j)),
                  pl.BlockSpec((tm, kp), lambda i, j: (i, 0)),
                  pl.BlockSpec((kp, tn), lambda i, j: (0, j)),
                  pl.BlockSpec((tm, tn), lambda i, j: (i, j))],
        out_specs=(pl.BlockSpec((tm, tn), lambda i, j: (i, j)),
                   pl.BlockSpec((tm, tn), lambda i, j: (i, j))),
        compiler_params=_cparams(("parallel", "parallel")),
        name="ple_gate",
    )(xb, wg, pb, wp, x)


def _gla_chunk(q, k, g, v, st):
    c = q.shape[0]
    row = _iota2((c, c), 0)
    col = _iota2((c, c), 1)
    causal = col <= row
    gc = _mm_hi(causal.astype(F32), g)
    g_mid = gc[c // 2:c // 2 + 1]
    g_last = gc[c - 1:c]
    a = _mm(q * jnp.exp(gc - g_mid), k * jnp.exp(g_mid - gc), NT)
    a = jnp.where(causal, a, 0.0)
    o = _mm(a, v) + _mm(q * jnp.exp(gc), st, NT)
    st_new = st * jnp.exp(g_last) + _mm(v, k * jnp.exp(g_last - gc), TN)
    return o, st_new


def _head_rms(o, gain, gate):
    o = o * lax.rsqrt(jnp.mean(o * o, axis=-1, keepdims=True) + RMS_EPS)
    return o * gain * _silu(gate)


def _hgrn2_kernel(q_ref, f_ref, i_ref, g_ref, lg_ref, nw_ref, o_ref, st_ref, *, layer, heads, nchunk):
    @pl.when(pl.program_id(2) == 0)
    def _():
        st_ref[...] = jnp.zeros_like(st_ref)

    lg = lg_ref[...]
    e = jnp.exp(lg - jnp.max(lg, axis=0, keepdims=True))
    lb = jnp.sum(e[0:layer + 1], axis=0, keepdims=True) / jnp.sum(e, axis=0, keepdims=True)

    def body(c, carry):
        r0 = pl.multiple_of(c * CHUNK, CHUNK)
        rows = pl.ds(r0, CHUNK)
        for h in range(heads):
            cols = slice(h * LANES, (h + 1) * LANES)
            f = lb[:, cols] + (1.0 - lb[:, cols]) * _sigmoid(f_ref[0, rows, cols])
            o, st = _gla_chunk(_silu(q_ref[0, rows, cols]), 1.0 - f, jnp.log(f),
                               i_ref[0, rows, cols], st_ref[h])
            st_ref[h] = st
            o_ref[0, rows, cols] = _head_rms(o, nw_ref[:, cols], g_ref[0, rows, cols]).astype(o_ref.dtype)
        return carry

    lax.fori_loop(0, nchunk, body, 0)


def _hgrn2(proj, logits, norm_w, layer, tb=256, heads=2):
    b, t, _ = proj.shape
    w = LANES * heads
    per = GROUP_W // w

    def col(section):
        return lambda bi, hi, ti: (bi, ti, section * per + hi)

    return pl.pallas_call(
        functools.partial(_hgrn2_kernel, layer=layer, heads=heads, nchunk=tb // CHUNK),
        out_shape=jax.ShapeDtypeStruct((b, t, GROUP_W), BF16),
        grid=(b, per, t // tb),
        in_specs=[pl.BlockSpec((1, tb, w), col(0)), pl.BlockSpec((1, tb, w), col(1)),
                  pl.BlockSpec((1, tb, w), col(2)), pl.BlockSpec((1, tb, w), col(3)),
                  pl.BlockSpec((logits.shape[0], w), lambda bi, hi, ti: (0, hi)),
                  pl.BlockSpec((1, w), lambda bi, hi, ti: (0, hi))],
        out_specs=pl.BlockSpec((1, tb, w), lambda bi, hi, ti: (bi, ti, hi)),
        scratch_shapes=[pltpu.VMEM((heads, LANES, LANES), F32)],
        compiler_params=_cparams(("parallel", "parallel", "arbitrary")),
        name="hgrn2_chunk",
    )(proj, proj, proj, proj, logits, norm_w)


def _gla_kernel(q_ref, k_ref, v_ref, r_ref, gd_ref, w2_ref, b_ref, nw_ref, o_ref, st_ref, *, nchunk):
    @pl.when(pl.program_id(2) == 0)
    def _():
        st_ref[...] = jnp.zeros_like(st_ref)

    def body(c, carry):
        r0 = pl.multiple_of(c * CHUNK, CHUNK)
        rows = pl.ds(r0, CHUNK)
        z = _mm_hi(gd_ref[0, rows, :], w2_ref[...]) + b_ref[...]
        g = -_softplus(-z) * (1.0 / GLA_TAU)
        o, st = _gla_chunk(q_ref[0, rows, :] * GLA_DK ** -0.5, k_ref[0, rows, :], g,
                           v_ref[0, rows, :], st_ref[...])
        st_ref[...] = st
        o_ref[0, rows, :] = _head_rms(o, nw_ref[...], r_ref[0, rows, :]).astype(o_ref.dtype)
        return carry

    lax.fori_loop(0, nchunk, body, 0)


def _gla(proj, gd, w2, bias, norm_w, tb=256):
    b, t, _ = proj.shape
    qk0 = 4 * GROUP_W // GLA_DK
    v0 = (4 * GROUP_W + 2 * GLA_HEADS * GLA_DK) // GLA_DV
    return pl.pallas_call(
        functools.partial(_gla_kernel, nchunk=tb // CHUNK),
        out_shape=jax.ShapeDtypeStruct((b, t, GROUP_W), BF16),
        grid=(b, GLA_HEADS, t // tb),
        in_specs=[pl.BlockSpec((1, tb, GLA_DK), lambda bi, hi, ti: (bi, ti, qk0 + hi)),
                  pl.BlockSpec((1, tb, GLA_DK), lambda bi, hi, ti: (bi, ti, qk0 + GLA_HEADS + hi)),
                  pl.BlockSpec((1, tb, GLA_DV), lambda bi, hi, ti: (bi, ti, v0 + hi)),
                  pl.BlockSpec((1, tb, GLA_DV), lambda bi, hi, ti: (bi, ti, v0 + GLA_HEADS + hi)),
                  pl.BlockSpec((1, tb, LANES), lambda bi, hi, ti: (bi, ti, 0)),
                  pl.BlockSpec((LANES, GLA_DK), lambda bi, hi, ti: (0, hi)),
                  pl.BlockSpec((1, GLA_DK), lambda bi, hi, ti: (0, hi)),
                  pl.BlockSpec((1, GLA_DV), lambda bi, hi, ti: (0, hi))],
        out_specs=pl.BlockSpec((1, tb, GLA_DV), lambda bi, hi, ti: (bi, ti, hi)),
        scratch_shapes=[pltpu.VMEM((GLA_DV, GLA_DK), F32)],
        compiler_params=_cparams(("parallel", "parallel", "arbitrary")),
        name="gla_chunk",
    )(proj, proj, proj, proj, gd, w2, bias, norm_w)


def _neumann_inverse(a):
    n = a.shape[0]
    eye = (_iota2((n, n), 0) == _iota2((n, n), 1)).astype(F32)
    t = eye + a
    p = a
    steps = max(1, (CHUNK - 1).bit_length() - 1)
    for _ in range(steps):
        p = _mm(p, p)
        t = t + _mm(t, p)
    return t


def _shifted(x, carry_ref, shift):
    tb = x.shape[0]
    ext = jnp.concatenate([carry_ref[...], x], axis=0)
    return pltpu.roll(ext, shift, 0)[SUBLANES:SUBLANES + tb]


def _gdn_kernel(q_ref, k_ref, v_ref, z_ref, ab_ref, cq_ref, ck_ref, cv_ref, al_ref, dt_ref, nw_ref,
                o_ref, st_ref, cq_s, ck_s, cv_s, qs, ks, vs, *, nchunk):
    head = pl.program_id(1)

    @pl.when(pl.program_id(2) == 0)
    def _():
        st_ref[...] = jnp.zeros_like(st_ref)
        cq_s[...] = jnp.zeros_like(cq_s)
        ck_s[...] = jnp.zeros_like(ck_s)
        cv_s[...] = jnp.zeros_like(cv_s)

    tb = q_ref.shape[1]

    def conv_silu(x_ref, carry, w_ref):
        x = x_ref[0]
        w = w_ref[...]
        y = x * w[3:4]
        for j in (1, 2, 3):
            y = y + _shifted(x, carry, j) * w[3 - j:4 - j]
        carry[...] = x[tb - SUBLANES:tb]
        return _silu(y)

    def l2n(x):
        return x * lax.rsqrt(jnp.sum(x * x, axis=-1, keepdims=True) + L2_EPS)

    qs[...] = l2n(conv_silu(q_ref, cq_s, cq_ref)) * GDN_HEAD ** -0.5
    ks[...] = l2n(conv_silu(k_ref, ck_s, ck_ref))
    vs[...] = conv_silu(v_ref, cv_s, cv_ref)

    c = CHUNK
    row = _iota2((c, c), 0)
    col = _iota2((c, c), 1)
    incl = col <= row
    strict = col < row
    lane = _iota2((c, LANES), 1)
    pick_a = (lane == head).astype(F32)
    pick_b = (lane == head + GDN_HEADS).astype(F32)
    pick_row = ((_iota2((SUBLANES, LANES), 1) == head) & (_iota2((SUBLANES, LANES), 0) == 0)).astype(F32)

    def body(ci, carry):
        r0 = pl.multiple_of(ci * c, c)
        rows = pl.ds(r0, c)
        ab = ab_ref[0, rows, :]
        gmat = -jnp.exp(al_ref[...]) * _softplus(ab + dt_ref[...])
        gcum = _mm_hi(incl.astype(F32), gmat)
        g_col = jnp.sum(gcum * pick_a, axis=1, keepdims=True)
        g_row = _mm_hi(pick_row, gcum, NT)[0:1]
        g_last = g_col[c - 1:c]
        beta = _sigmoid(jnp.sum(ab * pick_b, axis=1, keepdims=True))
        gam = jnp.where(incl, jnp.exp(jnp.where(incl, g_col - g_row, 0.0)), 0.0)

        q = qs[rows, :]
        k = ks[rows, :]
        v = vs[rows, :]
        kb = k * beta
        low = jnp.where(strict, _mm(kb, k, NT) * gam, 0.0)
        tinv = _neumann_inverse(-low)
        u = _mm(tinv, v * beta)
        w = _mm(tinv, kb * jnp.exp(g_col))
        aqk = jnp.where(incl, _mm(q, k, NT) * gam, 0.0)
        s = st_ref[...]
        v_new = u - _mm(w, s)
        o = _mm(q * jnp.exp(g_col), s) + _mm(aqk, v_new)
        st_ref[...] = s * jnp.exp(g_last) + _mm(k * jnp.exp(g_last - g_col), v_new, TN)
        o_ref[0, rows, :] = _head_rms(o, nw_ref[...], z_ref[0, rows, :]).astype(o_ref.dtype)
        return carry

    lax.fori_loop(0, nchunk, body, 0)


def _gdn(proj, small, conv_w, alog_pad, dt_pad, norm_w, tb=256):
    b, t, _ = proj.shape
    base = 3 * GROUP_W // GDN_HEAD
    hd = GDN_HEADS
    ab_blk = (small.shape[2] - LANES) // LANES

    def col(section):
        return lambda bi, hi, ti: (bi, ti, base + section * hd + hi)

    blk = pl.BlockSpec((1, tb, GDN_HEAD), col(0))
    return pl.pallas_call(
        functools.partial(_gdn_kernel, nchunk=tb // CHUNK),
        out_shape=jax.ShapeDtypeStruct((b, t, GROUP_W), BF16),
        grid=(b, hd, t // tb),
        in_specs=[blk, pl.BlockSpec((1, tb, GDN_HEAD), col(1)), pl.BlockSpec((1, tb, GDN_HEAD), col(2)),
                  pl.BlockSpec((1, tb, GDN_HEAD), col(3)),
                  pl.BlockSpec((1, tb, LANES), lambda bi, hi, ti: (bi, ti, ab_blk)),
                  pl.BlockSpec((4, GDN_HEAD), lambda bi, hi, ti: (0, hi)),
                  pl.BlockSpec((4, GDN_HEAD), lambda bi, hi, ti: (0, hd + hi)),
                  pl.BlockSpec((4, GDN_HEAD), lambda bi, hi, ti: (0, 2 * hd + hi)),
                  pl.BlockSpec((1, LANES), lambda bi, hi, ti: (0, 0)),
                  pl.BlockSpec((1, LANES), lambda bi, hi, ti: (0, 0)),
                  pl.BlockSpec((1, GDN_HEAD), lambda bi, hi, ti: (0, hi))],
        out_specs=pl.BlockSpec((1, tb, GDN_HEAD), lambda bi, hi, ti: (bi, ti, hi)),
        scratch_shapes=[pltpu.VMEM((GDN_HEAD, GDN_HEAD), F32)]
        + [pltpu.VMEM((SUBLANES, GDN_HEAD), F32)] * 3
        + [pltpu.VMEM((tb, GDN_HEAD), F32)] * 3,
        compiler_params=_cparams(("parallel", "parallel", "arbitrary")),
        name="gdn_chunk",
    )(proj, proj, proj, proj, small, conv_w, conv_w, conv_w, alog_pad, dt_pad, norm_w)


def _rwkv_kernel(r_ref, k_ref, v_ref, sm_ref, mur_ref, muk_ref, muv_ref, mus_ref, w0_ref, w2_ref, a0_ref,
                 a2_ref, g2_ref, kk_ref, ka_ref, rk_ref, gw_ref, gb_ref, o_ref,
                 st_ref, cr_s, ck_s, cv_s, cs_s, rs, lws, k2s, vs, aas, bbs, gts, ys, *, nchunk, lora):
    @pl.when(pl.program_id(2) == 0)
    def _():
        for ref in (st_ref, cr_s, ck_s, cv_s, cs_s):
            ref[...] = jnp.zeros_like(ref)

    tb = r_ref.shape[1]
    half = RWKV_HEAD
    block_diag = (_iota2((LANES, LANES), 0) // half) == (_iota2((LANES, LANES), 1) // half)
    ones_bd = block_diag.astype(BF16)

    def seg_sum(x):
        hi = x.astype(BF16)
        lo = (x - hi.astype(F32)).astype(BF16)
        return (lax.dot_general(hi, ones_bd, NN, preferred_element_type=F32)
                + lax.dot_general(lo, ones_bd, NN, preferred_element_type=F32))

    def token_shift(x, carry, mu):
        prev = _shifted(x, carry, 1)
        carry[...] = x[tb - SUBLANES:tb]
        return x + (prev - x) * mu

    r = token_shift(r_ref[0], cr_s, mur_ref[...])
    k = token_shift(k_ref[0], ck_s, muk_ref[...])
    v = token_shift(v_ref[0], cv_s, muv_ref[...])
    sm = token_shift(sm_ref[0], cs_s, mus_ref[...])
    wd = sm[:, 0:lora]
    ad = sm[:, lora:2 * lora]
    gd = sm[:, 2 * lora:2 * lora + g2_ref.shape[0]]
    z = w0_ref[...] + _mm_hi(jnp.tanh(wd), w2_ref[...])
    lw = -jnp.exp(-_softplus(-z) - 0.5)
    ag = _sigmoid(a0_ref[...] + _mm_hi(ad, a2_ref[...]))
    gate = _mm(_sigmoid(gd), g2_ref[...])
    kk = k * kk_ref[...]
    kk = kk * lax.rsqrt(seg_sum(kk * kk) + L2_EPS)
    k2 = k * (1.0 + (ag - 1.0) * ka_ref[...])
    rs[...] = r
    lws[...] = lw
    k2s[...] = k2
    vs[...] = v
    aas[...] = -kk
    bbs[...] = kk * ag
    gts[...] = gate

    c = CHUNK
    m0 = _iota2((c, LANES), 1) < half
    row = _iota2((c, LANES), 0)
    sub = _iota2((c, LANES), 1) % c
    strict = sub < row
    incl = sub <= row
    tri = (_iota2((c, c), 1) <= _iota2((c, c), 0)).astype(F32)

    def body(ci, carry):
        r0 = pl.multiple_of(ci * c, c)
        rows = pl.ds(r0, c)
        rr = rs[rows, :]
        lwc = lws[rows, :]
        kc = k2s[rows, :]
        vc = vs[rows, :]
        aa = aas[rows, :]
        bb = bbs[rows, :]
        gc = _mm_hi(tri, lwc)
        gx = gc - lwc
        g_mid = gc[c // 2 - 1:c // 2]
        g_last = gc[c - 1:c]
        e_in = jnp.exp(gc - g_mid)
        e_out = jnp.exp(g_mid - gc)
        at = aa * jnp.exp(gx - g_mid)
        rt = rr * e_in
        bt = bb * e_out
        kt = kc * e_out
        lhs0 = jnp.concatenate([jnp.where(m0, at, 0.0), jnp.where(m0, rt, 0.0)], axis=0)
        lhs1 = jnp.concatenate([jnp.where(m0, 0.0, at), jnp.where(m0, 0.0, rt)], axis=0)
        res0 = _mm(lhs0, jnp.concatenate([bt, kt], axis=0), NT)
        res1 = _mm(lhs1, jnp.concatenate([kt, bt], axis=0), NT)
        top0 = jnp.where(strict, res0[0:c], 0.0)
        top1 = jnp.where(strict, res1[0:c], 0.0)
        a_bd = jnp.concatenate([jnp.where(m0, top0, 0.0), jnp.where(m0, 0.0, top1)], axis=0)
        ak_x = jnp.concatenate([jnp.where(m0, 0.0, top0), jnp.where(m0, top1, 0.0)], axis=0)
        wmat = jnp.concatenate([jnp.where(incl, res0[c:2 * c], 0.0),
                                jnp.where(incl, res1[c:2 * c], 0.0)], axis=0)
        tinv = _neumann_inverse(a_bd)

        ht = st_ref[...]
        xh = _mm(jnp.concatenate([aa * jnp.exp(gx), rr * jnp.exp(gc)], axis=0), ht, NT)
        xa = xh[0:c]
        xr = xh[c:2 * c]
        v_sw = jnp.concatenate([jnp.where(m0, 0.0, vc), jnp.where(m0, vc, 0.0)], axis=0)
        rhs = jnp.concatenate([jnp.where(m0, xa, 0.0), jnp.where(m0, 0.0, xa)], axis=0) + _mm(ak_x, v_sw)
        u_st = _mm(tinv, rhs)
        y_st = _mm(wmat, u_st + v_sw)
        ys[rows, :] = jnp.where(m0, y_st[0:c], y_st[c:2 * c]) + xr
        u = u_st[0:c] + u_st[c:2 * c]
        e_end = jnp.exp(g_last - gc)
        upd = _mm(jnp.concatenate([u, vc], axis=0),
                  jnp.concatenate([bb * e_end, kc * e_end], axis=0), TN)
        st_ref[...] = ht * jnp.exp(g_last) + jnp.where(block_diag, upd, 0.0)
        return carry

    lax.fori_loop(0, nchunk, body, 0)

    y = ys[...]
    mean = seg_sum(y) * (1.0 / half)
    yc = y - mean
    var = seg_sum(yc * yc) * (1.0 / half)
    yn = yc * lax.rsqrt(var + RWKV_GN_EPS) * gw_ref[...] + gb_ref[...]
    v = vs[...]
    bonus = seg_sum(rs[...] * k2s[...] * rk_ref[...]) * v
    o_ref[0] = ((yn + bonus) * gts[...]).astype(o_ref.dtype)


def _rwkv(proj, small, mu_main, mu_small, w0, w2p, a0, a2p, g2, kk, ka, rk, gw, gb, lora, tb=256):
    b, t, _ = proj.shape
    pairs = GROUP_W // LANES
    ws = small.shape[2]
    glora = g2.shape[0]

    def col(section):
        return lambda bi, hi, ti: (bi, ti, section * pairs + hi)

    def vec(section=0):
        return pl.BlockSpec((1, LANES), lambda bi, hi, ti: (0, section * pairs + hi))

    row_blk = lambda s: pl.BlockSpec((1, tb, LANES), col(s))
    return pl.pallas_call(
        functools.partial(_rwkv_kernel, nchunk=tb // CHUNK, lora=lora),
        out_shape=jax.ShapeDtypeStruct((b, t, GROUP_W), BF16),
        grid=(b, pairs, t // tb),
        in_specs=[row_blk(0), row_blk(1), row_blk(2),
                  pl.BlockSpec((1, tb, ws), lambda bi, hi, ti: (bi, ti, 0)),
                  vec(0), vec(1), vec(2),
                  pl.BlockSpec((1, ws), lambda bi, hi, ti: (0, 0)),
                  vec(), pl.BlockSpec((lora, LANES), lambda bi, hi, ti: (0, hi)),
                  vec(), pl.BlockSpec((lora, LANES), lambda bi, hi, ti: (0, hi)),
                  pl.BlockSpec((glora, LANES), lambda bi, hi, ti: (0, hi)),
                  vec(), vec(), vec(), vec(), vec()],
        out_specs=pl.BlockSpec((1, tb, LANES), lambda bi, hi, ti: (bi, ti, hi)),
        scratch_shapes=[pltpu.VMEM((LANES, LANES), F32)]
        + [pltpu.VMEM((SUBLANES, LANES), F32)] * 3
        + [pltpu.VMEM((SUBLANES, ws), F32)]
        + [pltpu.VMEM((tb, LANES), F32)] * 8,
        compiler_params=_cparams(("parallel", "parallel", "arbitrary")),
        name="rwkv7_chunk",
    )(proj, proj, proj, small, mu_main, mu_main, mu_main, mu_small, w0, w2p, a0, a2p, g2,
      kk, ka, rk, gw, gb)


def _pad_cols(a, width):
    return jnp.pad(a, ((0, 0), (0, width - a.shape[1])))


def _pad_rows(a, height):
    return jnp.pad(a, ((0, height - a.shape[0]), (0, 0)))


def _row(v):
    return v.reshape(1, -1).astype(F32)


def kernel(x, p, hgrn_lb_logits, e_w_in, e_gla_w2, e_gla_b, e_hgrn_norm, e_gla_norm, e_w_out, o_w_in, o_rwkv_mu, o_rwkv_w0, o_rwkv_w2, o_rwkv_a0, o_rwkv_a2, o_rwkv_g2, o_rwkv_kk, o_rwkv_ka, o_rwkv_rk, o_rwkv_gn_w, o_rwkv_gn_b, o_gdn_conv, o_gdn_a_log, o_gdn_dt_bias, o_gdn_norm, o_w_out, ln_mix_w, ln_mix_b, ln_ffn_w, ln_ffn_b, ffn_w_up, ffn_conv, ffn_w_down, ple_w_proj, ple_w_gate):
    bsz, seq, d = x.shape
    m = bsz * seq
    depth = ln_mix_w.shape[0]
    alpha = (2 * depth) ** 0.25
    gw = GROUP_W
    x32 = x.reshape(m, d)
    xb = x32.astype(BF16)
    for layer in range(depth):
        j = layer // 2
        if layer % 2 == 0:
            w_in = e_w_in[j]
            rank = e_gla_w2.shape[1]
            gd0 = 4 * gw + 2 * GLA_HEADS * GLA_DK + gw
            w_main = jnp.concatenate([w_in[:, :gd0], w_in[:, gd0 + rank:]], axis=1).astype(BF16)
            w_gd = _pad_cols(w_in[:, gd0:gd0 + rank], LANES).astype(BF16)
            proj = _matmul(xb, w_main, F32, 1024, 1024).reshape(bsz, seq, -1)
            gd = _matmul(xb, w_gd, F32, 1024, LANES).reshape(bsz, seq, LANES)
            o_a = _hgrn2(proj, hgrn_lb_logits.astype(F32), _row(e_hgrn_norm[j]), layer)
            o_b = _gla(proj, gd, _pad_rows(e_gla_w2[j], LANES), _row(e_gla_b[j]), _row(e_gla_norm[j]))
            w_out = e_w_out[j]
        else:
            w_in = o_w_in[j]
            lora_w = o_rwkv_w2.shape[1]
            lora_a = o_rwkv_a2.shape[1]
            lora_g = o_rwkv_g2.shape[1]
            lora = LANES * (-(-max(lora_w, lora_a) // LANES))
            c_wd = 3 * gw
            c_ad = c_wd + lora_w
            c_gd = c_ad + lora_a
            c_qkv = c_gd + lora_g
            c_z = c_qkv + 3 * gw
            c_ab = c_z + gw
            w_main = jnp.concatenate([w_in[:, :c_wd], w_in[:, c_qkv:c_ab]], axis=1).astype(BF16)
            w_small = jnp.concatenate([_pad_cols(w_in[:, c_wd:c_ad], lora), _pad_cols(w_in[:, c_ad:c_gd], lora),
                                       w_in[:, c_gd:c_qkv], _pad_cols(w_in[:, c_ab:], LANES)], axis=1).astype(BF16)
            mu = o_rwkv_mu[j].reshape(1, -1)
            mu_small = jnp.concatenate([_pad_cols(mu[:, c_wd:c_ad], lora), _pad_cols(mu[:, c_ad:c_gd], lora),
                                        mu[:, c_gd:c_qkv], jnp.zeros((1, LANES), F32)], axis=1)
            proj = _matmul(xb, w_main, F32, 1024, 1024).reshape(bsz, seq, -1)
            small = _matmul(xb, w_small, F32, 1024, w_small.shape[1]).reshape(bsz, seq, -1)
            o_a = _rwkv(proj, small, mu[:, :c_wd], mu_small, _row(o_rwkv_w0[j]),
                        _pad_rows(o_rwkv_w2[j], lora), _row(o_rwkv_a0[j]), _pad_rows(o_rwkv_a2[j], lora),
                        o_rwkv_g2[j], _row(o_rwkv_kk[j]), _row(o_rwkv_ka[j]), _row(o_rwkv_rk[j]),
                        _row(o_rwkv_gn_w[j]), _row(o_rwkv_gn_b[j]), lora)
            o_b = _gdn(proj, small, o_gdn_conv[j].astype(F32),
                       _pad_cols(_row(o_gdn_a_log[j]), LANES), _pad_cols(_row(o_gdn_dt_bias[j]), LANES),
                       _row(o_gdn_norm[j]))
            w_out = o_w_out[j]
        x32, xb = _mix_out(o_a.reshape(m, gw), o_b.reshape(m, gw), w_out.astype(BF16), x32,
                           _row(ln_mix_w[layer]), _row(ln_mix_b[layer]), alpha)
        h = _ffn_up(xb, ffn_w_up[layer].astype(BF16), ffn_conv[layer].astype(F32), seq)
        x32, xb = _ffn_down(h, ffn_w_down[layer].astype(BF16), x32, _row(ln_ffn_w[layer]),
                            _row(ln_ffn_b[layer]), alpha)
        x32, xb = _ple(xb, x32, p[layer].reshape(m, -1).astype(BF16), ple_w_gate[layer].astype(BF16),
                       ple_w_proj[layer].astype(BF16))
    return x32.reshape(bsz, seq, d)
```

```python
import functools

import jax
import jax.numpy as jnp
from jax import lax
from jax.experimental import pallas as pl
from jax.experimental.pallas import tpu as pltpu

F32 = jnp.float32
BF16 = jnp.bfloat16
HIGHEST = lax.Precision.HIGHEST

NN = (((1,), (0,)), ((), ()))
NT = (((1,), (1,)), ((), ()))
TN = (((0,), (0,)), ((), ()))

D_MODEL = 2048
GROUP_W = 1024
CHUNK = 64
LANES = 128
SUBLANES = 8
HGRN_HEADS = 8
GLA_HEADS = 4
GLA_DK = 128
GLA_DV = 256
GLA_TAU = 16.0
RWKV_HEAD = 64
RWKV_GN_EPS = 64e-5
GDN_HEADS = 8
GDN_HEAD = 128
FFN_DIM = 5632
LN_EPS = 1e-5
RMS_EPS = 1e-6
L2_EPS = 1e-6

VMEM_LIMIT = 56 * 1024 * 1024


def _mm(a, b, dims=NN):
    return lax.dot_general(a.astype(BF16), b.astype(BF16), dims, preferred_element_type=F32)


def _mm_hi(a, b, dims=NN):
    return lax.dot_general(a, b, dims, precision=HIGHEST, preferred_element_type=F32)


def _sigmoid(x):
    return 1.0 / (1.0 + jnp.exp(-x))


def _silu(x):
    return x * _sigmoid(x)


def _softplus(x):
    return jnp.maximum(x, 0.0) + jnp.log(1.0 + jnp.exp(-jnp.abs(x)))


def _iota2(shape, dim):
    return lax.broadcasted_iota(jnp.int32, shape, dim)


def _cparams(sem):
    return pltpu.CompilerParams(dimension_semantics=sem, vmem_limit_bytes=VMEM_LIMIT)


def _matmul_kernel(x_ref, w_ref, o_ref):
    o_ref[...] = _mm(x_ref[...], w_ref[...]).astype(o_ref.dtype)


def _matmul(x, w, out_dtype, tm, tn):
    m, k = x.shape
    n = w.shape[1]
    return pl.pallas_call(
        _matmul_kernel,
        out_shape=jax.ShapeDtypeStruct((m, n), out_dtype),
        grid=(m // tm, n // tn),
        in_specs=[pl.BlockSpec((tm, k), lambda i, j: (i, 0)),
                  pl.BlockSpec((k, tn), lambda i, j: (0, j))],
        out_specs=pl.BlockSpec((tm, tn), lambda i, j: (i, j)),
        compiler_params=_cparams(("parallel", "parallel")),
        name="proj_matmul",
    )(x, w)


def _layer_norm_rows(y, w, b):
    mu = jnp.mean(y, axis=-1, keepdims=True)
    yc = y - mu
    var = jnp.mean(yc * yc, axis=-1, keepdims=True)
    return yc * lax.rsqrt(var + LN_EPS) * w + b


def _mix_out_kernel(oa_ref, ob_ref, w_ref, x_ref, lw_ref, lb_ref, o32_ref, o16_ref, *, alpha):
    half = oa_ref.shape[1]
    acc = _mm(oa_ref[...], w_ref[0:half, :]) + _mm(ob_ref[...], w_ref[half:2 * half, :])
    y = _layer_norm_rows(alpha * x_ref[...] + acc, lw_ref[...], lb_ref[...])
    o32_ref[...] = y
    o16_ref[...] = y.astype(BF16)


def _mix_out(oa, ob, w, x, lw, lb, alpha, tm=512):
    m, half = oa.shape
    n = w.shape[1]
    row = lambda i: (i, 0)
    fixed = lambda i: (0, 0)
    return pl.pallas_call(
        functools.partial(_mix_out_kernel, alpha=alpha),
        out_shape=(jax.ShapeDtypeStruct((m, n), F32), jax.ShapeDtypeStruct((m, n), BF16)),
        grid=(m // tm,),
        in_specs=[pl.BlockSpec((tm, half), row), pl.BlockSpec((tm, half), row),
                  pl.BlockSpec((2 * half, n), fixed), pl.BlockSpec((tm, n), row),
                  pl.BlockSpec((1, n), fixed), pl.BlockSpec((1, n), fixed)],
        out_specs=(pl.BlockSpec((tm, n), row), pl.BlockSpec((tm, n), row)),
        compiler_params=_cparams(("parallel",)),
        name="mix_out_ln",
    )(oa, ob, w, x, lw, lb)


def _ffn_up_kernel(x_ref, xh_ref, wg_ref, wv_ref, cg_ref, cv_ref, o_ref, *, tiles_per_seq):
    first = (pl.program_id(0) % tiles_per_seq) == 0
    tm = x_ref.shape[0]

    def conv(w_ref, c_ref):
        u = _mm(x_ref[...], w_ref[...])
        halo = _mm(xh_ref[...], w_ref[...])
        halo = jnp.where(first, 0.0, halo)
        ext = jnp.concatenate([halo, u], axis=0)
        p1 = pltpu.roll(ext, 1, 0)[SUBLANES:SUBLANES + tm]
        p2 = pltpu.roll(ext, 2, 0)[SUBLANES:SUBLANES + tm]
        c = c_ref[...]
        return u * c[2:3] + p1 * c[1:2] + p2 * c[0:1]

    g = conv(wg_ref, cg_ref)
    v = conv(wv_ref, cv_ref)
    o_ref[...] = (_silu(g) * v).astype(o_ref.dtype)


def _ffn_up(xb, w_up, conv_w, seq, tm=1024, tf=512):
    m, k = xb.shape
    f = w_up.shape[1] // 2
    nf = f // tf
    hb = tm // SUBLANES
    return pl.pallas_call(
        functools.partial(_ffn_up_kernel, tiles_per_seq=seq // tm),
        out_shape=jax.ShapeDtypeStruct((m, f), BF16),
        grid=(m // tm, nf),
        in_specs=[pl.BlockSpec((tm, k), lambda i, j: (i, 0)),
                  pl.BlockSpec((SUBLANES, k), lambda i, j: (jnp.maximum(i * hb - 1, 0), 0)),
                  pl.BlockSpec((k, tf), lambda i, j: (0, j)),
                  pl.BlockSpec((k, tf), lambda i, j: (0, nf + j)),
                  pl.BlockSpec((3, tf), lambda i, j: (0, j)),
                  pl.BlockSpec((3, tf), lambda i, j: (0, nf + j))],
        out_specs=pl.BlockSpec((tm, tf), lambda i, j: (i, j)),
        compiler_params=_cparams(("parallel", "parallel")),
        name="ffn_up_conv_gate",
    )(xb, xb, w_up, w_up, conv_w, conv_w)


def _ffn_down_kernel(h_ref, w_ref, x_ref, lw_ref, lb_ref, o32_ref, o16_ref, acc_ref, *, alpha, nk):
    k = pl.program_id(1)

    @pl.when(k == 0)
    def _():
        acc_ref[...] = jnp.zeros_like(acc_ref)

    acc_ref[...] += _mm(h_ref[...], w_ref[...])

    @pl.when(k == nk - 1)
    def _():
        y = _layer_norm_rows(alpha * x_ref[...] + acc_ref[...], lw_ref[...], lb_ref[...])
        o32_ref[...] = y
        o16_ref[...] = y.astype(BF16)


def _ffn_down(h, w, x, lw, lb, alpha, tm=512, tk=1408):
    m, kdim = h.shape
    n = w.shape[1]
    nk = kdim // tk
    return pl.pallas_call(
        functools.partial(_ffn_down_kernel, alpha=alpha, nk=nk),
        out_shape=(jax.ShapeDtypeStruct((m, n), F32), jax.ShapeDtypeStruct((m, n), BF16)),
        grid=(m // tm, nk),
        in_specs=[pl.BlockSpec((tm, tk), lambda i, k: (i, k)),
                  pl.BlockSpec((tk, n), lambda i, k: (k, 0)),
                  pl.BlockSpec((tm, n), lambda i, k: (i, 0)),
                  pl.BlockSpec((1, n), lambda i, k: (0, 0)),
                  pl.BlockSpec((1, n), lambda i, k: (0, 0))],
        out_specs=(pl.BlockSpec((tm, n), lambda i, k: (i, 0)),
                   pl.BlockSpec((tm, n), lambda i, k: (i, 0))),
        scratch_shapes=[pltpu.VMEM((tm, n), F32)],
        compiler_params=_cparams(("parallel", "arbitrary")),
        name="ffn_down_ln",
    )(h, w, x, lw, lb)


def _ple_kernel(xb_ref, wg_ref, p_ref, wp_ref, x_ref, o32_ref, o16_ref):
    gate = _sigmoid(_mm(xb_ref[...], wg_ref[...]))
    y = x_ref[...] + gate * _mm(p_ref[...], wp_ref[...])
    o32_ref[...] = y
    o16_ref[...] = y.astype(BF16)


def _ple(xb, x, pb, wg, wp, tm=1024, tn=512):
    m, k = xb.shape
    n = wg.shape[1]
    kp = pb.shape[1]
    return pl.pallas_call(
        _ple_kernel,
        out_shape=(jax.ShapeDtypeStruct((m, n), F32), jax.ShapeDtypeStruct((m, n), BF16)),
        grid=(m // tm, n // tn),
        in_specs=[pl.BlockSpec((tm, k), lambda i, j: (i, 0)),
                  pl.BlockSpec((k, tn), lambda i, j: (0, j)),
                  pl.BlockSpec((tm, kp), lambda i, j: (i, 0)),
                  pl.BlockSpec((kp, tn), lambda i, j: (0, j)),
                  pl.BlockSpec((tm, tn), lambda i, j: (i, j))],
        out_specs=(pl.BlockSpec((tm, tn), lambda i, j: (i, j)),
                   pl.BlockSpec((tm, tn), lambda i, j: (i, j))),
        compiler_params=_cparams(("parallel", "parallel")),
        name="ple_gate",
    )(xb, wg, pb, wp, x)


def _gla_chunk(q, k, g, v, st):
    c = q.shape[0]
    row = _iota2((c, c), 0)
    col = _iota2((c, c), 1)
    causal = col <= row
    gc = _mm_hi(causal.astype(F32), g)
    g_mid = gc[c // 2:c // 2 + 1]
    g_last = gc[c - 1:c]
    a = _mm(q * jnp.exp(gc - g_mid), k * jnp.exp(g_mid - gc), NT)
    a = jnp.where(causal, a, 0.0)
    o = _mm(a, v) + _mm(q * jnp.exp(gc), st, NT)
    st_new = st * jnp.exp(g_last) + _mm(v, k * jnp.exp(g_last - gc), TN)
    return o, st_new


def _head_rms(o, gain, gate):
    o = o * lax.rsqrt(jnp.mean(o * o, axis=-1, keepdims=True) + RMS_EPS)
    return o * gain * _silu(gate)


def _hgrn2_kernel(q_ref, f_ref, i_ref, g_ref, lg_ref, nw_ref, o_ref, st_ref, *, layer, heads, nchunk):
    @pl.when(pl.program_id(2) == 0)
    def _():
        st_ref[...] = jnp.zeros_like(st_ref)

    lg = lg_ref[...]
    e = jnp.exp(lg - jnp.max(lg, axis=0, keepdims=True))
    lb = jnp.sum(e[0:layer + 1], axis=0, keepdims=True) / jnp.sum(e, axis=0, keepdims=True)

    def body(c, carry):
        r0 = pl.multiple_of(c * CHUNK, CHUNK)
        rows = pl.ds(r0, CHUNK)
        for h in range(heads):
            cols = slice(h * LANES, (h + 1) * LANES)
            f = lb[:, cols] + (1.0 - lb[:, cols]) * _sigmoid(f_ref[0, rows, cols])
            o, st = _gla_chunk(_silu(q_ref[0, rows, cols]), 1.0 - f, jnp.log(f),
                               i_ref[0, rows, cols], st_ref[h])
            st_ref[h] = st
            o_ref[0, rows, cols] = _head_rms(o, nw_ref[:, cols], g_ref[0, rows, cols]).astype(o_ref.dtype)
        return carry

    lax.fori_loop(0, nchunk, body, 0, unroll=True)


def _hgrn2(proj, logits, norm_w, layer, tb=256, heads=2):
    b, t, _ = proj.shape
    w = LANES * heads
    per = GROUP_W // w

    def col(section):
        return lambda bi, hi, ti: (bi, ti, section * per + hi)

    return pl.pallas_call(
        functools.partial(_hgrn2_kernel, layer=layer, heads=heads, nchunk=tb // CHUNK),
        out_shape=jax.ShapeDtypeStruct((b, t, GROUP_W), BF16),
        grid=(b, per, t // tb),
        in_specs=[pl.BlockSpec((1, tb, w), col(0)), pl.BlockSpec((1, tb, w), col(1)),
                  pl.BlockSpec((1, tb, w), col(2)), pl.BlockSpec((1, tb, w), col(3)),
                  pl.BlockSpec((logits.shape[0], w), lambda bi, hi, ti: (0, hi)),
                  pl.BlockSpec((1, w), lambda bi, hi, ti: (0, hi))],
        out_specs=pl.BlockSpec((1, tb, w), lambda bi, hi, ti: (bi, ti, hi)),
        scratch_shapes=[pltpu.VMEM((heads, LANES, LANES), F32)],
        compiler_params=_cparams(("parallel", "parallel", "arbitrary")),
        name="hgrn2_chunk",
    )(proj, proj, proj, proj, logits, norm_w)


def _gla_kernel(q_ref, k_ref, v_ref, r_ref, gd_ref, w2_ref, b_ref, nw_ref, o_ref, st_ref, *, nchunk):
    @pl.when(pl.program_id(2) == 0)
    def _():
        st_ref[...] = jnp.zeros_like(st_ref)

    def body(c, carry):
        r0 = pl.multiple_of(c * CHUNK, CHUNK)
        rows = pl.ds(r0, CHUNK)
        z = _mm_hi(gd_ref[0, rows, :], w2_ref[...]) + b_ref[...]
        g = -_softplus(-z) * (1.0 / GLA_TAU)
        o, st = _gla_chunk(q_ref[0, rows, :] * GLA_DK ** -0.5, k_ref[0, rows, :], g,
                           v_ref[0, rows, :], st_ref[...])
        st_ref[...] = st
        o_ref[0, rows, :] = _head_rms(o, nw_ref[...], r_ref[0, rows, :]).astype(o_ref.dtype)
        return carry

    lax.fori_loop(0, nchunk, body, 0, unroll=True)


def _gla(proj, gd, w2, bias, norm_w, tb=256):
    b, t, _ = proj.shape
    qk0 = 4 * GROUP_W // GLA_DK
    v0 = (4 * GROUP_W + 2 * GLA_HEADS * GLA_DK) // GLA_DV
    return pl.pallas_call(
        functools.partial(_gla_kernel, nchunk=tb // CHUNK),
        out_shape=jax.ShapeDtypeStruct((b, t, GROUP_W), BF16),
        grid=(b, GLA_HEADS, t // tb),
        in_specs=[pl.BlockSpec((1, tb, GLA_DK), lambda bi, hi, ti: (bi, ti, qk0 + hi)),
                  pl.BlockSpec((1, tb, GLA_DK), lambda bi, hi, ti: (bi, ti, qk0 + GLA_HEADS + hi)),
                  pl.BlockSpec((1, tb, GLA_DV), lambda bi, hi, ti: (bi, ti, v0 + hi)),
                  pl.BlockSpec((1, tb, GLA_DV), lambda bi, hi, ti: (bi, ti, v0 + GLA_HEADS + hi)),
                  pl.BlockSpec((1, tb, LANES), lambda bi, hi, ti: (bi, ti, 0)),
                  pl.BlockSpec((LANES, GLA_DK), lambda bi, hi, ti: (0, hi)),
                  pl.BlockSpec((1, GLA_DK), lambda bi, hi, ti: (0, hi)),
                  pl.BlockSpec((1, GLA_DV), lambda bi, hi, ti: (0, hi))],
        out_specs=pl.BlockSpec((1, tb, GLA_DV), lambda bi, hi, ti: (bi, ti, hi)),
        scratch_shapes=[pltpu.VMEM((GLA_DV, GLA_DK), F32)],
        compiler_params=_cparams(("parallel", "parallel", "arbitrary")),
        name="gla_chunk",
    )(proj, proj, proj, proj, gd, w2, bias, norm_w)


def _neumann_inverses(mats):
    n = mats[0].shape[0]
    eye = (_iota2((n, n), 0) == _iota2((n, n), 1)).astype(F32)
    ts = [eye + a for a in mats]
    ps = list(mats)
    steps = max(1, (CHUNK - 1).bit_length() - 1)
    for _ in range(steps):
        ps = [_mm(p, p) for p in ps]
        ts = [t + _mm(t, p) for t, p in zip(ts, ps)]
    return ts


def _neumann_inverse(a):
    return _neumann_inverses([a])[0]


def _shifted(x, carry_ref, shift):
    tb = x.shape[0]
    ext = jnp.concatenate([carry_ref[...], x], axis=0)
    return pltpu.roll(ext, shift, 0)[SUBLANES:SUBLANES + tb]


def _gdn_kernel(q_ref, k_ref, v_ref, z_ref, ab_ref, cq_ref, ck_ref, cv_ref, al_ref, dt_ref, nw_ref,
                o_ref, st_ref, cq_s, ck_s, cv_s, *, nchunk):
    head = pl.program_id(1)

    @pl.when(pl.program_id(2) == 0)
    def _():
        st_ref[...] = jnp.zeros_like(st_ref)
        cq_s[...] = jnp.zeros_like(cq_s)
        ck_s[...] = jnp.zeros_like(ck_s)
        cv_s[...] = jnp.zeros_like(cv_s)

    tb = q_ref.shape[1]

    def conv_silu(x_ref, carry, w_ref):
        x = x_ref[0]
        w = w_ref[...]
        y = x * w[3:4]
        for j in (1, 2, 3):
            y = y + _shifted(x, carry, j) * w[3 - j:4 - j]
        carry[...] = x[tb - SUBLANES:tb]
        return _silu(y)

    def l2n(x):
        return x * lax.rsqrt(jnp.sum(x * x, axis=-1, keepdims=True) + L2_EPS)

    q = l2n(conv_silu(q_ref, cq_s, cq_ref)) * GDN_HEAD ** -0.5
    k = l2n(conv_silu(k_ref, ck_s, ck_ref))
    v = conv_silu(v_ref, cv_s, cv_ref)

    c = CHUNK
    incl = _iota2((c, c), 1) <= _iota2((c, c), 0)
    strict = _iota2((c, c), 1) < _iota2((c, c), 0)
    rb = _iota2((tb, tb), 0)
    cb = _iota2((tb, tb), 1)
    chunk_tri = ((cb <= rb) & ((cb // c) == (rb // c))).astype(F32)
    lane = _iota2((tb, LANES), 1)
    ab = ab_ref[0]
    gmat = -jnp.exp(al_ref[...]) * _softplus(ab + dt_ref[...])
    gcum = _mm_hi(chunk_tri, gmat)
    g_col = jnp.sum(jnp.where(lane == head, gcum, 0.0), axis=1, keepdims=True)
    beta = _sigmoid(jnp.sum(jnp.where(lane == head + GDN_HEADS, ab, 0.0), axis=1, keepdims=True))
    pick = (_iota2((c, LANES), 1) == head).astype(F32)
    eg = jnp.exp(g_col)
    kb = k * beta
    rhs = jnp.concatenate([v * beta, kb * eg], axis=1)
    qg = q * eg

    lows, aqks, kgts, decays = [], [], [], []
    for ci in range(nchunk):
        sl = slice(ci * c, (ci + 1) * c)
        g_c = g_col[sl]
        g_rows = _mm_hi(pick, gcum[sl], NT)
        gam = jnp.where(incl, jnp.exp(jnp.where(incl, g_c - g_rows, 0.0)), 0.0)
        lows.append(jnp.where(strict, _mm(kb[sl], k[sl], NT) * gam, 0.0))
        aqks.append(jnp.where(incl, _mm(q[sl], k[sl], NT) * gam, 0.0).astype(BF16))
        g_last = g_c[c - 1:c]
        kgts.append((k[sl] * jnp.exp(g_last - g_c)).T.astype(BF16))
        decays.append(jnp.exp(g_last))
    tinvs = _neumann_inverses([-low for low in lows])
    o0s, qps, mts, nts = [], [], [], []
    for ci in range(nchunk):
        sl = slice(ci * c, (ci + 1) * c)
        uw = _mm(tinvs[ci], rhs[sl])
        ow = _mm(aqks[ci], uw)
        kw = _mm(kgts[ci], uw)
        o0s.append(ow[:, 0:GDN_HEAD])
        qps.append((qg[sl] - ow[:, GDN_HEAD:2 * GDN_HEAD]).astype(BF16))
        nts.append(kw[:, 0:GDN_HEAD])
        mts.append((-kw[:, GDN_HEAD:2 * GDN_HEAD]).astype(BF16))

    s = st_ref[...]
    outs = []
    for ci in range(nchunk):
        sb = s.astype(BF16)
        outs.append(o0s[ci] + _mm(qps[ci], sb))
        s = s * decays[ci] + _mm(mts[ci], sb) + nts[ci]
    st_ref[...] = s
    o = jnp.concatenate(outs, axis=0)
    o_ref[0] = _head_rms(o, nw_ref[...], z_ref[0]).astype(o_ref.dtype)


def _gdn(proj, small, conv_w, alog_pad, dt_pad, norm_w, tb=256):
    b, t, _ = proj.shape
    base = 3 * GROUP_W // GDN_HEAD
    hd = GDN_HEADS
    ab_blk = (small.shape[2] - LANES) // LANES

    def col(section):
        return lambda bi, hi, ti: (bi, ti, base + section * hd + hi)

    blk = pl.BlockSpec((1, tb, GDN_HEAD), col(0))
    return pl.pallas_call(
        functools.partial(_gdn_kernel, nchunk=tb // CHUNK),
        out_shape=jax.ShapeDtypeStruct((b, t, GROUP_W), BF16),
        grid=(b, hd, t // tb),
        in_specs=[blk, pl.BlockSpec((1, tb, GDN_HEAD), col(1)), pl.BlockSpec((1, tb, GDN_HEAD), col(2)),
                  pl.BlockSpec((1, tb, GDN_HEAD), col(3)),
                  pl.BlockSpec((1, tb, LANES), lambda bi, hi, ti: (bi, ti, ab_blk)),
                  pl.BlockSpec((4, GDN_HEAD), lambda bi, hi, ti: (0, hi)),
                  pl.BlockSpec((4, GDN_HEAD), lambda bi, hi, ti: (0, hd + hi)),
                  pl.BlockSpec((4, GDN_HEAD), lambda bi, hi, ti: (0, 2 * hd + hi)),
                  pl.BlockSpec((1, LANES), lambda bi, hi, ti: (0, 0)),
                  pl.BlockSpec((1, LANES), lambda bi, hi, ti: (0, 0)),
                  pl.BlockSpec((1, GDN_HEAD), lambda bi, hi, ti: (0, hi))],
        out_specs=pl.BlockSpec((1, tb, GDN_HEAD), lambda bi, hi, ti: (bi, ti, hi)),
        scratch_shapes=[pltpu.VMEM((GDN_HEAD, GDN_HEAD), F32)]
        + [pltpu.VMEM((SUBLANES, GDN_HEAD), F32)] * 3,
        compiler_params=_cparams(("parallel", "parallel", "arbitrary")),
        name="gdn_chunk",
    )(proj, proj, proj, proj, small, conv_w, conv_w, conv_w, alog_pad, dt_pad, norm_w)


def _rwkv_kernel(r_ref, k_ref, v_ref, sm_ref, mur_ref, muk_ref, muv_ref, mus_ref, w0_ref, w2_ref, a0_ref,
                 a2_ref, g2_ref, kk_ref, ka_ref, rk_ref, gw_ref, gb_ref, o_ref,
                 st_ref, cr_s, ck_s, cv_s, cs_s, *, nchunk, lora):
    @pl.when(pl.program_id(2) == 0)
    def _():
        for ref in (st_ref, cr_s, ck_s, cv_s, cs_s):
            ref[...] = jnp.zeros_like(ref)

    tb = r_ref.shape[1]
    half = RWKV_HEAD
    block_diag = (_iota2((LANES, LANES), 0) // half) == (_iota2((LANES, LANES), 1) // half)
    ones_bd = block_diag.astype(BF16)

    def seg_sum(x):
        hi = x.astype(BF16)
        lo = (x - hi.astype(F32)).astype(BF16)
        return (lax.dot_general(hi, ones_bd, NN, preferred_element_type=F32)
                + lax.dot_general(lo, ones_bd, NN, preferred_element_type=F32))

    def token_shift(x, carry, mu):
        prev = _shifted(x, carry, 1)
        carry[...] = x[tb - SUBLANES:tb]
        return x + (prev - x) * mu

    r = token_shift(r_ref[0], cr_s, mur_ref[...])
    k = token_shift(k_ref[0], ck_s, muk_ref[...])
    v = token_shift(v_ref[0], cv_s, muv_ref[...])
    sm = token_shift(sm_ref[0], cs_s, mus_ref[...])
    wd = sm[:, 0:lora]
    ad = sm[:, lora:2 * lora]
    gd = sm[:, 2 * lora:2 * lora + g2_ref.shape[0]]
    z = w0_ref[...] + _mm_hi(jnp.tanh(wd), w2_ref[...])
    lw = -jnp.exp(-_softplus(-z) - 0.5)
    ag = _sigmoid(a0_ref[...] + _mm_hi(ad, a2_ref[...]))
    gate = _mm(_sigmoid(gd), g2_ref[...])
    kk = k * kk_ref[...]
    kk = kk * lax.rsqrt(seg_sum(kk * kk) + L2_EPS)
    k2 = k * (1.0 + (ag - 1.0) * ka_ref[...])
    aa_all = -kk
    bb_all = kk * ag

    c = CHUNK
    m0 = _iota2((c, LANES), 1) < half
    row = _iota2((c, LANES), 0)
    sub = _iota2((c, LANES), 1) % c
    strict = sub < row
    incl = sub <= row
    rb = _iota2((tb, tb), 0)
    cb = _iota2((tb, tb), 1)
    chunk_tri = ((cb <= rb) & ((cb // c) == (rb // c))).astype(F32)
    gc_all = _mm_hi(chunk_tri, lw)
    gx_all = gc_all - lw

    def stack(x):
        return jnp.concatenate([jnp.where(m0, x, 0.0), jnp.where(m0, 0.0, x)], axis=0)

    def unstack(x):
        return jnp.where(m0, x[0:c], x[c:2 * c])

    a_bds, ak_xs, wmats, a_sts, v_sws, r_abss, nvs, bes, decays = [], [], [], [], [], [], [], [], []
    for ci in range(nchunk):
        sl = slice(ci * c, (ci + 1) * c)
        rr, kc, vc, aa, bb = r[sl], k2[sl], v[sl], aa_all[sl], bb_all[sl]
        gc, gx = gc_all[sl], gx_all[sl]
        g_mid = gc[c // 2 - 1:c // 2]
        g_last = gc[c - 1:c]
        e_out = jnp.exp(g_mid - gc)
        at = aa * jnp.exp(gx - g_mid)
        rt = rr * jnp.exp(gc - g_mid)
        bt = bb * e_out
        kt = kc * e_out
        lhs0 = jnp.concatenate([jnp.where(m0, at, 0.0), jnp.where(m0, rt, 0.0)], axis=0)
        lhs1 = jnp.concatenate([jnp.where(m0, 0.0, at), jnp.where(m0, 0.0, rt)], axis=0)
        res0 = _mm(lhs0, jnp.concatenate([bt, kt], axis=0), NT)
        res1 = _mm(lhs1, jnp.concatenate([kt, bt], axis=0), NT)
        top0 = jnp.where(strict, res0[0:c], 0.0)
        top1 = jnp.where(strict, res1[0:c], 0.0)
        a_bds.append(jnp.concatenate([jnp.where(m0, top0, 0.0), jnp.where(m0, 0.0, top1)], axis=0))
        ak_xs.append(jnp.concatenate([jnp.where(m0, 0.0, top0), jnp.where(m0, top1, 0.0)], axis=0))
        wmats.append(jnp.concatenate([jnp.where(incl, res0[c:2 * c], 0.0),
                                      jnp.where(incl, res1[c:2 * c], 0.0)], axis=0))
        a_sts.append(stack(aa * jnp.exp(gx)))
        v_sws.append(jnp.concatenate([jnp.where(m0, 0.0, vc), jnp.where(m0, vc, 0.0)], axis=0))
        r_abss.append(rr * jnp.exp(gc))
        e_end = jnp.exp(g_last - gc)
        bes.append(bb * e_end)
        nvs.append((vc, kc * e_end))
        decays.append(jnp.exp(g_last))
    tinvs = _neumann_inverses(a_bds)

    y0s, qps, gbds, ncs = [], [], [], []
    for ci in range(nchunk):
        rhs_v = _mm(ak_xs[ci], v_sws[ci])
        uw = _mm(tinvs[ci], jnp.concatenate([rhs_v, a_sts[ci]], axis=1))
        u_v = uw[:, 0:LANES]
        wh = uw[:, LANES:2 * LANES]
        yw = _mm(wmats[ci], jnp.concatenate([u_v + v_sws[ci], wh], axis=1))
        y0s.append(unstack(yw[:, 0:LANES]))
        qps.append((unstack(yw[:, LANES:2 * LANES]) + r_abss[ci]).astype(BF16))
        gbds.append(jnp.where(block_diag, _mm(unstack(wh), bes[ci], TN), 0.0).astype(BF16))
        vc, ke = nvs[ci]
        ncs.append(jnp.where(block_diag, _mm(jnp.concatenate([unstack(u_v), vc], axis=0),
                                             jnp.concatenate([bes[ci], ke], axis=0), TN), 0.0))

    ht = st_ref[...]
    ys = []
    for ci in range(nchunk):
        htb = ht.astype(BF16)
        ys.append(y0s[ci] + _mm(qps[ci], htb, NT))
        ht = ht * decays[ci] + _mm(htb, gbds[ci]) + ncs[ci]
    st_ref[...] = ht

    y = jnp.concatenate(ys, axis=0)
    mean = seg_sum(y) * (1.0 / half)
    yc = y - mean
    var = seg_sum(yc * yc) * (1.0 / half)
    yn = yc * lax.rsqrt(var + RWKV_GN_EPS) * gw_ref[...] + gb_ref[...]
    bonus = seg_sum(r * k2 * rk_ref[...]) * v
    o_ref[0] = ((yn + bonus) * gate).astype(o_ref.dtype)


def _rwkv(proj, small, mu_main, mu_small, w0, w2p, a0, a2p, g2, kk, ka, rk, gw, gb, lora, tb=256):
    b, t, _ = proj.shape
    pairs = GROUP_W // LANES
    ws = small.shape[2]
    glora = g2.shape[0]

    def col(section):
        return lambda bi, hi, ti: (bi, ti, section * pairs + hi)

    def vec(section=0):
        return pl.BlockSpec((1, LANES), lambda bi, hi, ti: (0, section * pairs + hi))

    row_blk = lambda s: pl.BlockSpec((1, tb, LANES), col(s))
    return pl.pallas_call(
        functools.partial(_rwkv_kernel, nchunk=tb // CHUNK, lora=lora),
        out_shape=jax.ShapeDtypeStruct((b, t, GROUP_W), BF16),
        grid=(b, pairs, t // tb),
        in_specs=[row_blk(0), row_blk(1), row_blk(2),
                  pl.BlockSpec((1, tb, ws), lambda bi, hi, ti: (bi, ti, 0)),
                  vec(0), vec(1), vec(2),
                  pl.BlockSpec((1, ws), lambda bi, hi, ti: (0, 0)),
                  vec(), pl.BlockSpec((lora, LANES), lambda bi, hi, ti: (0, hi)),
                  vec(), pl.BlockSpec((lora, LANES), lambda bi, hi, ti: (0, hi)),
                  pl.BlockSpec((glora, LANES), lambda bi, hi, ti: (0, hi)),
                  vec(), vec(), vec(), vec(), vec()],
        out_specs=pl.BlockSpec((1, tb, LANES), lambda bi, hi, ti: (bi, ti, hi)),
        scratch_shapes=[pltpu.VMEM((LANES, LANES), F32)]
        + [pltpu.VMEM((SUBLANES, LANES), F32)] * 3
        + [pltpu.VMEM((SUBLANES, ws), F32)],
        compiler_params=_cparams(("parallel", "parallel", "arbitrary")),
        name="rwkv7_chunk",
    )(proj, proj, proj, small, mu_main, mu_main, mu_main, mu_small, w0, w2p, a0, a2p, g2,
      kk, ka, rk, gw, gb)


def _pad_cols(a, width):
    return jnp.pad(a, ((0, 0), (0, width - a.shape[1])))


def _pad_rows(a, height):
    return jnp.pad(a, ((0, height - a.shape[0]), (0, 0)))


def _row(v):
    return v.reshape(1, -1).astype(F32)


def kernel(x, p, hgrn_lb_logits, e_w_in, e_gla_w2, e_gla_b, e_hgrn_norm, e_gla_norm, e_w_out, o_w_in, o_rwkv_mu, o_rwkv_w0, o_rwkv_w2, o_rwkv_a0, o_rwkv_a2, o_rwkv_g2, o_rwkv_kk, o_rwkv_ka, o_rwkv_rk, o_rwkv_gn_w, o_rwkv_gn_b, o_gdn_conv, o_gdn_a_log, o_gdn_dt_bias, o_gdn_norm, o_w_out, ln_mix_w, ln_mix_b, ln_ffn_w, ln_ffn_b, ffn_w_up, ffn_conv, ffn_w_down, ple_w_proj, ple_w_gate):
    bsz, seq, d = x.shape
    m = bsz * seq
    depth = ln_mix_w.shape[0]
    alpha = (2 * depth) ** 0.25
    gw = GROUP_W
    x32 = x.reshape(m, d)
    xb = x32.astype(BF16)
    for layer in range(depth):
        j = layer // 2
        if layer % 2 == 0:
            w_in = e_w_in[j]
            rank = e_gla_w2.shape[1]
            gd0 = 4 * gw + 2 * GLA_HEADS * GLA_DK + gw
            w_main = jnp.concatenate([w_in[:, :gd0], w_in[:, gd0 + rank:]], axis=1).astype(BF16)
            w_gd = _pad_cols(w_in[:, gd0:gd0 + rank], LANES).astype(BF16)
            proj = _matmul(xb, w_main, F32, 1024, 1024).reshape(bsz, seq, -1)
            gd = _matmul(xb, w_gd, F32, 1024, LANES).reshape(bsz, seq, LANES)
            o_a = _hgrn2(proj, hgrn_lb_logits.astype(F32), _row(e_hgrn_norm[j]), layer)
            o_b = _gla(proj, gd, _pad_rows(e_gla_w2[j], LANES), _row(e_gla_b[j]), _row(e_gla_norm[j]))
            w_out = e_w_out[j]
        else:
            w_in = o_w_in[j]
            lora_w = o_rwkv_w2.shape[1]
            lora_a = o_rwkv_a2.shape[1]
            lora_g = o_rwkv_g2.shape[1]
            lora = LANES * (-(-max(lora_w, lora_a) // LANES))
            c_wd = 3 * gw
            c_ad = c_wd + lora_w
            c_gd = c_ad + lora_a
            c_qkv = c_gd + lora_g
            c_z = c_qkv + 3 * gw
            c_ab = c_z + gw
            w_main = jnp.concatenate([w_in[:, :c_wd], w_in[:, c_qkv:c_ab]], axis=1).astype(BF16)
            w_small = jnp.concatenate([_pad_cols(w_in[:, c_wd:c_ad], lora), _pad_cols(w_in[:, c_ad:c_gd], lora),
                                       w_in[:, c_gd:c_qkv], _pad_cols(w_in[:, c_ab:], LANES)], axis=1).astype(BF16)
            mu = o_rwkv_mu[j].reshape(1, -1)
            mu_small = jnp.concatenate([_pad_cols(mu[:, c_wd:c_ad], lora), _pad_cols(mu[:, c_ad:c_gd], lora),
                                        mu[:, c_gd:c_qkv], jnp.zeros((1, LANES), F32)], axis=1)
            proj = _matmul(xb, w_main, F32, 1024, 1024).reshape(bsz, seq, -1)
            small = _matmul(xb, w_small, F32, 1024, w_small.shape[1]).reshape(bsz, seq, -1)
            o_a = _rwkv(proj, small, mu[:, :c_wd], mu_small, _row(o_rwkv_w0[j]),
                        _pad_rows(o_rwkv_w2[j], lora), _row(o_rwkv_a0[j]), _pad_rows(o_rwkv_a2[j], lora),
                        o_rwkv_g2[j], _row(o_rwkv_kk[j]), _row(o_rwkv_ka[j]), _row(o_rwkv_rk[j]),
                        _row(o_rwkv_gn_w[j]), _row(o_rwkv_gn_b[j]), lora)
            o_b = _gdn(proj, small, o_gdn_conv[j].astype(F32),
                       _pad_cols(_row(o_gdn_a_log[j]), LANES), _pad_cols(_row(o_gdn_dt_bias[j]), LANES),
                       _row(o_gdn_norm[j]))
            w_out = o_w_out[j]
        x32, xb = _mix_out(o_a.reshape(m, gw), o_b.reshape(m, gw), w_out.astype(BF16), x32,
                           _row(ln_mix_w[layer]), _row(ln_mix_b[layer]), alpha)
        h = _ffn_up(xb, ffn_w_up[layer].astype(BF16), ffn_conv[layer].astype(F32), seq)
        x32, xb = _ffn_down(h, ffn_w_down[layer].astype(BF16), x32, _row(ln_ffn_w[layer]),
                            _row(ln_ffn_b[layer]), alpha)
        x32, xb = _ple(xb, x32, p[layer].reshape(m, -1).astype(BF16), ple_w_gate[layer].astype(BF16),
                       ple_w_proj[layer].astype(BF16))
    return x32.reshape(bsz, seq, d)
```

```python
import functools

import jax
import jax.numpy as jnp
from jax import lax
from jax.experimental import pallas as pl
from jax.experimental.pallas import tpu as pltpu

F32 = jnp.float32
BF16 = jnp.bfloat16
HIGHEST = lax.Precision.HIGHEST

NN = (((1,), (0,)), ((), ()))
NT = (((1,), (1,)), ((), ()))
TN = (((0,), (0,)), ((), ()))

D_MODEL = 2048
GROUP_W = 1024
CHUNK = 64
LANES = 128
SUBLANES = 8
HGRN_HEADS = 8
GLA_HEADS = 4
GLA_DK = 128
GLA_DV = 256
GLA_TAU = 16.0
RWKV_HEAD = 64
RWKV_GN_EPS = 64e-5
GDN_HEADS = 8
GDN_HEAD = 128
FFN_DIM = 5632
LN_EPS = 1e-5
RMS_EPS = 1e-6
L2_EPS = 1e-6

VMEM_LIMIT = 56 * 1024 * 1024


def _mm(a, b, dims=NN):
    return lax.dot_general(a.astype(BF16), b.astype(BF16), dims, preferred_element_type=F32)


def _mm_hi(a, b, dims=NN):
    return lax.dot_general(a, b, dims, precision=HIGHEST, preferred_element_type=F32)


def _sigmoid(x):
    return 1.0 / (1.0 + jnp.exp(-x))


def _silu(x):
    return x * _sigmoid(x)


def _softplus(x):
    return jnp.maximum(x, 0.0) + jnp.log(1.0 + jnp.exp(-jnp.abs(x)))


def _iota2(shape, dim):
    return lax.broadcasted_iota(jnp.int32, shape, dim)


def _cparams(sem):
    return pltpu.CompilerParams(dimension_semantics=sem, vmem_limit_bytes=VMEM_LIMIT)


def _matmul_kernel(x_ref, w_ref, o_ref):
    o_ref[...] = _mm(x_ref[...], w_ref[...]).astype(o_ref.dtype)


def _matmul(x, w, out_dtype, tm, tn):
    m, k = x.shape
    n = w.shape[1]
    return pl.pallas_call(
        _matmul_kernel,
        out_shape=jax.ShapeDtypeStruct((m, n), out_dtype),
        grid=(m // tm, n // tn),
        in_specs=[pl.BlockSpec((tm, k), lambda i, j: (i, 0)),
                  pl.BlockSpec((k, tn), lambda i, j: (0, j))],
        out_specs=pl.BlockSpec((tm, tn), lambda i, j: (i, j)),
        compiler_params=_cparams(("parallel", "parallel")),
        name="proj_matmul",
    )(x, w)


def _layer_norm_rows(y, w, b):
    mu = jnp.mean(y, axis=-1, keepdims=True)
    yc = y - mu
    var = jnp.mean(yc * yc, axis=-1, keepdims=True)
    return yc * lax.rsqrt(var + LN_EPS) * w + b


def _mix_out_kernel(oa_ref, ob_ref, w_ref, x_ref, lw_ref, lb_ref, o32_ref, o16_ref, *, alpha):
    half = oa_ref.shape[1]
    acc = _mm(oa_ref[...], w_ref[0:half, :]) + _mm(ob_ref[...], w_ref[half:2 * half, :])
    y = _layer_norm_rows(alpha * x_ref[...] + acc, lw_ref[...], lb_ref[...])
    o32_ref[...] = y
    o16_ref[...] = y.astype(BF16)


def _mix_out(oa, ob, w, x, lw, lb, alpha, tm=512):
    m, half = oa.shape
    n = w.shape[1]
    row = lambda i: (i, 0)
    fixed = lambda i: (0, 0)
    return pl.pallas_call(
        functools.partial(_mix_out_kernel, alpha=alpha),
        out_shape=(jax.ShapeDtypeStruct((m, n), F32), jax.ShapeDtypeStruct((m, n), BF16)),
        grid=(m // tm,),
        in_specs=[pl.BlockSpec((tm, half), row), pl.BlockSpec((tm, half), row),
                  pl.BlockSpec((2 * half, n), fixed), pl.BlockSpec((tm, n), row),
                  pl.BlockSpec((1, n), fixed), pl.BlockSpec((1, n), fixed)],
        out_specs=(pl.BlockSpec((tm, n), row), pl.BlockSpec((tm, n), row)),
        compiler_params=_cparams(("parallel",)),
        name="mix_out_ln",
    )(oa, ob, w, x, lw, lb)


def _ffn_up_kernel(x_ref, xh_ref, wg_ref, wv_ref, cg_ref, cv_ref, o_ref, *, tiles_per_seq):
    first = (pl.program_id(0) % tiles_per_seq) == 0
    tm = x_ref.shape[0]

    def conv(w_ref, c_ref):
        u = _mm(x_ref[...], w_ref[...])
        halo = _mm(xh_ref[...], w_ref[...])
        halo = jnp.where(first, 0.0, halo)
        ext = jnp.concatenate([halo, u], axis=0)
        p1 = pltpu.roll(ext, 1, 0)[SUBLANES:SUBLANES + tm]
        p2 = pltpu.roll(ext, 2, 0)[SUBLANES:SUBLANES + tm]
        c = c_ref[...]
        return u * c[2:3] + p1 * c[1:2] + p2 * c[0:1]

    g = conv(wg_ref, cg_ref)
    v = conv(wv_ref, cv_ref)
    o_ref[...] = (_silu(g) * v).astype(o_ref.dtype)


def _ffn_up(xb, w_up, conv_w, seq, tm=1024, tf=512):
    m, k = xb.shape
    f = w_up.shape[1] // 2
    nf = f // tf
    hb = tm // SUBLANES
    return pl.pallas_call(
        functools.partial(_ffn_up_kernel, tiles_per_seq=seq // tm),
        out_shape=jax.ShapeDtypeStruct((m, f), BF16),
        grid=(m // tm, nf),
        in_specs=[pl.BlockSpec((tm, k), lambda i, j: (i, 0)),
                  pl.BlockSpec((SUBLANES, k), lambda i, j: (jnp.maximum(i * hb - 1, 0), 0)),
                  pl.BlockSpec((k, tf), lambda i, j: (0, j)),
                  pl.BlockSpec((k, tf), lambda i, j: (0, nf + j)),
                  pl.BlockSpec((3, tf), lambda i, j: (0, j)),
                  pl.BlockSpec((3, tf), lambda i, j: (0, nf + j))],
        out_specs=pl.BlockSpec((tm, tf), lambda i, j: (i, j)),
        compiler_params=_cparams(("parallel", "parallel")),
        name="ffn_up_conv_gate",
    )(xb, xb, w_up, w_up, conv_w, conv_w)


def _ffn_down_kernel(h_ref, w_ref, x_ref, lw_ref, lb_ref, o32_ref, o16_ref, acc_ref, *, alpha, nk):
    k = pl.program_id(1)

    @pl.when(k == 0)
    def _():
        acc_ref[...] = jnp.zeros_like(acc_ref)

    acc_ref[...] += _mm(h_ref[...], w_ref[...])

    @pl.when(k == nk - 1)
    def _():
        y = _layer_norm_rows(alpha * x_ref[...] + acc_ref[...], lw_ref[...], lb_ref[...])
        o32_ref[...] = y
        o16_ref[...] = y.astype(BF16)


def _ffn_down(h, w, x, lw, lb, alpha, tm=512, tk=1408):
    m, kdim = h.shape
    n = w.shape[1]
    nk = kdim // tk
    return pl.pallas_call(
        functools.partial(_ffn_down_kernel, alpha=alpha, nk=nk),
        out_shape=(jax.ShapeDtypeStruct((m, n), F32), jax.ShapeDtypeStruct((m, n), BF16)),
        grid=(m // tm, nk),
        in_specs=[pl.BlockSpec((tm, tk), lambda i, k: (i, k)),
                  pl.BlockSpec((tk, n), lambda i, k: (k, 0)),
                  pl.BlockSpec((tm, n), lambda i, k: (i, 0)),
                  pl.BlockSpec((1, n), lambda i, k: (0, 0)),
                  pl.BlockSpec((1, n), lambda i, k: (0, 0))],
        out_specs=(pl.BlockSpec((tm, n), lambda i, k: (i, 0)),
                   pl.BlockSpec((tm, n), lambda i, k: (i, 0))),
        scratch_shapes=[pltpu.VMEM((tm, n), F32)],
        compiler_params=_cparams(("parallel", "arbitrary")),
        name="ffn_down_ln",
    )(h, w, x, lw, lb)


def _ple_kernel(xb_ref, wg_ref, p_ref, wp_ref, x_ref, o32_ref, o16_ref):
    gate = _sigmoid(_mm(xb_ref[...], wg_ref[...]))
    y = x_ref[...] + gate * _mm(p_ref[...], wp_ref[...])
    o32_ref[...] = y
    o16_ref[...] = y.astype(BF16)


def _ple(xb, x, pb, wg, wp, tm=1024, tn=512):
    m, k = xb.shape
    n = wg.shape[1]
    kp = pb.shape[1]
    return pl.pallas_call(
        _ple_kernel,
        out_shape=(jax.ShapeDtypeStruct((m, n), F32), jax.ShapeDtypeStruct((m, n), BF16)),
        grid=(m // tm, n // tn),
        in_specs=[pl.BlockSpec((tm, k), lambda i, j: (i, 0)),
                  pl.BlockSpec((k, tn), lambda i, j: (0, j)),
                  pl.BlockSpec((tm, kp), lambda i, j: (i, 0)),
                  pl.BlockSpec((kp, tn), lambda i, j: (0, j)),
                  pl.BlockSpec((tm, tn), lambda i, j: (i, j))],
        out_specs=(pl.BlockSpec((tm, tn), lambda i, j: (i, j)),
                   pl.BlockSpec((tm, tn), lambda i, j: (i, j))),
        compiler_params=_cparams(("parallel", "parallel")),
        name="ple_gate",
    )(xb, wg, pb, wp, x)


def _chunk_tri(tb):
    rb = _iota2((tb, tb), 0)
    cb = _iota2((tb, tb), 1)
    return ((cb <= rb) & ((cb // CHUNK) == (rb // CHUNK))).astype(F32)


def _gla_block(q, k, g, v, st_ref, heads, dk, dv):
    tb = q.shape[0]
    c = CHUNK
    causal = _iota2((c, c), 1) <= _iota2((c, c), 0)
    gc_all = _mm_hi(_chunk_tri(tb), g)
    outs = []
    for h in range(heads):
        kc = slice(h * dk, (h + 1) * dk)
        vc = slice(h * dv, (h + 1) * dv)
        st = st_ref[h]
        o_h = []
        for ci in range(tb // c):
            rows = slice(ci * c, (ci + 1) * c)
            gc = gc_all[rows, kc]
            qq, kk, vv = q[rows, kc], k[rows, kc], v[rows, vc]
            g_mid = gc[c // 2:c // 2 + 1]
            g_last = gc[c - 1:c]
            a = _mm(qq * jnp.exp(gc - g_mid), kk * jnp.exp(g_mid - gc), NT)
            a = jnp.where(causal, a, 0.0)
            o_h.append(_mm(a, vv) + _mm(qq * jnp.exp(gc), st, NT))
            st = st * jnp.exp(g_last) + _mm(vv, kk * jnp.exp(g_last - gc), TN)
        st_ref[h] = st
        outs.append(jnp.concatenate(o_h, axis=0))
    return outs


def _head_rms(o, gain, gate):
    o = o * lax.rsqrt(jnp.mean(o * o, axis=-1, keepdims=True) + RMS_EPS)
    return o * gain * _silu(gate)


def _hgrn2_kernel(q_ref, f_ref, i_ref, g_ref, lg_ref, nw_ref, o_ref, st_ref, *, layer, heads, nchunk):
    @pl.when(pl.program_id(2) == 0)
    def _():
        st_ref[...] = jnp.zeros_like(st_ref)

    lg = lg_ref[...]
    e = jnp.exp(lg - jnp.max(lg, axis=0, keepdims=True))
    lb = jnp.sum(e[0:layer + 1], axis=0, keepdims=True) / jnp.sum(e, axis=0, keepdims=True)

    f = lb + (1.0 - lb) * _sigmoid(f_ref[0])
    outs = _gla_block(_silu(q_ref[0]), 1.0 - f, jnp.log(f), i_ref[0], st_ref, heads, LANES, LANES)
    for h in range(heads):
        cols = slice(h * LANES, (h + 1) * LANES)
        o_ref[0, :, cols] = _head_rms(outs[h], nw_ref[:, cols], g_ref[0, :, cols]).astype(o_ref.dtype)


def _hgrn2(proj, logits, norm_w, layer, tb=256, heads=4):
    b, t, _ = proj.shape
    w = LANES * heads
    per = GROUP_W // w

    def col(section):
        return lambda bi, hi, ti: (bi, ti, section * per + hi)

    return pl.pallas_call(
        functools.partial(_hgrn2_kernel, layer=layer, heads=heads, nchunk=tb // CHUNK),
        out_shape=jax.ShapeDtypeStruct((b, t, GROUP_W), BF16),
        grid=(b, per, t // tb),
        in_specs=[pl.BlockSpec((1, tb, w), col(0)), pl.BlockSpec((1, tb, w), col(1)),
                  pl.BlockSpec((1, tb, w), col(2)), pl.BlockSpec((1, tb, w), col(3)),
                  pl.BlockSpec((logits.shape[0], w), lambda bi, hi, ti: (0, hi)),
                  pl.BlockSpec((1, w), lambda bi, hi, ti: (0, hi))],
        out_specs=pl.BlockSpec((1, tb, w), lambda bi, hi, ti: (bi, ti, hi)),
        scratch_shapes=[pltpu.VMEM((heads, LANES, LANES), F32)],
        compiler_params=_cparams(("parallel", "parallel", "arbitrary")),
        name="hgrn2_chunk",
    )(proj, proj, proj, proj, logits, norm_w)


def _gla_kernel(q_ref, k_ref, v_ref, r_ref, gd_ref, w2_ref, b_ref, nw_ref, o_ref, st_ref, *, heads):
    @pl.when(pl.program_id(2) == 0)
    def _():
        st_ref[...] = jnp.zeros_like(st_ref)

    z = _mm_hi(gd_ref[0], w2_ref[...]) + b_ref[...]
    g = -_softplus(-z) * (1.0 / GLA_TAU)
    outs = _gla_block(q_ref[0] * GLA_DK ** -0.5, k_ref[0], g, v_ref[0], st_ref, heads, GLA_DK, GLA_DV)
    for h in range(heads):
        cols = slice(h * GLA_DV, (h + 1) * GLA_DV)
        o_ref[0, :, cols] = _head_rms(outs[h], nw_ref[:, cols], r_ref[0, :, cols]).astype(o_ref.dtype)


def _gla(proj, gd, w2, bias, norm_w, tb=256, heads=2):
    b, t, _ = proj.shape
    wk = GLA_DK * heads
    wv = GLA_DV * heads
    groups = GLA_HEADS // heads
    qk0 = 4 * GROUP_W // wk
    v0 = (4 * GROUP_W + 2 * GLA_HEADS * GLA_DK) // wv
    return pl.pallas_call(
        functools.partial(_gla_kernel, heads=heads),
        out_shape=jax.ShapeDtypeStruct((b, t, GROUP_W), BF16),
        grid=(b, groups, t // tb),
        in_specs=[pl.BlockSpec((1, tb, wk), lambda bi, hi, ti: (bi, ti, qk0 + hi)),
                  pl.BlockSpec((1, tb, wk), lambda bi, hi, ti: (bi, ti, qk0 + groups + hi)),
                  pl.BlockSpec((1, tb, wv), lambda bi, hi, ti: (bi, ti, v0 + hi)),
                  pl.BlockSpec((1, tb, wv), lambda bi, hi, ti: (bi, ti, v0 + groups + hi)),
                  pl.BlockSpec((1, tb, LANES), lambda bi, hi, ti: (bi, ti, 0)),
                  pl.BlockSpec((LANES, wk), lambda bi, hi, ti: (0, hi)),
                  pl.BlockSpec((1, wk), lambda bi, hi, ti: (0, hi)),
                  pl.BlockSpec((1, wv), lambda bi, hi, ti: (0, hi))],
        out_specs=pl.BlockSpec((1, tb, wv), lambda bi, hi, ti: (bi, ti, hi)),
        scratch_shapes=[pltpu.VMEM((heads, GLA_DV, GLA_DK), F32)],
        compiler_params=_cparams(("parallel", "parallel", "arbitrary")),
        name="gla_chunk",
    )(proj, proj, proj, proj, gd, w2, bias, norm_w)


def _neumann_inverses(mats):
    n = mats[0].shape[0]
    eye = (_iota2((n, n), 0) == _iota2((n, n), 1)).astype(F32)
    ts = [eye + a for a in mats]
    ps = list(mats)
    steps = max(1, (CHUNK - 1).bit_length() - 1)
    for _ in range(steps):
        ps = [_mm(p, p) for p in ps]
        ts = [t + _mm(t, p) for t, p in zip(ts, ps)]
    return ts


def _neumann_inverse(a):
    return _neumann_inverses([a])[0]


def _shifted(x, carry_ref, shift):
    tb = x.shape[0]
    ext = jnp.concatenate([carry_ref[...], x], axis=0)
    return pltpu.roll(ext, shift, 0)[SUBLANES:SUBLANES + tb]


def _gdn_kernel(q_ref, k_ref, v_ref, z_ref, ab_ref, cq_ref, ck_ref, cv_ref, al_ref, dt_ref, nw_ref,
                o_ref, st_ref, cq_s, ck_s, cv_s, *, nchunk, heads):
    head0 = pl.program_id(1) * heads

    @pl.when(pl.program_id(2) == 0)
    def _():
        st_ref[...] = jnp.zeros_like(st_ref)
        cq_s[...] = jnp.zeros_like(cq_s)
        ck_s[...] = jnp.zeros_like(ck_s)
        cv_s[...] = jnp.zeros_like(cv_s)

    tb = q_ref.shape[1]

    def conv_silu(x_ref, carry, w_ref):
        x = x_ref[0]
        w = w_ref[...]
        y = x * w[3:4]
        for j in (1, 2, 3):
            y = y + _shifted(x, carry, j) * w[3 - j:4 - j]
        carry[...] = x[tb - SUBLANES:tb]
        return _silu(y)

    def l2n(x):
        return x * lax.rsqrt(jnp.sum(x * x, axis=-1, keepdims=True) + L2_EPS)

    q_all = conv_silu(q_ref, cq_s, cq_ref)
    k_all = conv_silu(k_ref, ck_s, ck_ref)
    v_all = conv_silu(v_ref, cv_s, cv_ref)

    c = CHUNK
    incl = _iota2((c, c), 1) <= _iota2((c, c), 0)
    strict = _iota2((c, c), 1) < _iota2((c, c), 0)
    rb = _iota2((tb, tb), 0)
    cb = _iota2((tb, tb), 1)
    chunk_tri = ((cb <= rb) & ((cb // c) == (rb // c))).astype(F32)
    lane = _iota2((tb, LANES), 1)
    ab = ab_ref[0]
    gmat = -jnp.exp(al_ref[...]) * _softplus(ab + dt_ref[...])
    gcum = _mm_hi(chunk_tri, gmat)
    items = [(h, ci) for h in range(heads) for ci in range(nchunk)]
    per_head = []
    for h in range(heads):
        cols = slice(h * GDN_HEAD, (h + 1) * GDN_HEAD)
        q = l2n(q_all[:, cols]) * GDN_HEAD ** -0.5
        k = l2n(k_all[:, cols])
        g_col = jnp.sum(jnp.where(lane == head0 + h, gcum, 0.0), axis=1, keepdims=True)
        beta = _sigmoid(jnp.sum(jnp.where(lane == head0 + h + GDN_HEADS, ab, 0.0), axis=1, keepdims=True))
        eg = jnp.exp(g_col)
        kb = k * beta
        per_head.append(dict(q=q, k=k, g_col=g_col, kb=kb, qg=q * eg,
                             rhs=jnp.concatenate([v_all[:, cols] * beta, kb * eg], axis=1),
                             pick=(_iota2((c, LANES), 1) == head0 + h).astype(F32)))

    lows, aqks, kgts, decays = [], [], [], []
    for h, ci in items:
        ph = per_head[h]
        sl = slice(ci * c, (ci + 1) * c)
        g_c = ph["g_col"][sl]
        g_rows = _mm_hi(ph["pick"], gcum[sl], NT)
        gam = jnp.where(incl, jnp.exp(jnp.where(incl, g_c - g_rows, 0.0)), 0.0)
        lows.append(jnp.where(strict, _mm(ph["kb"][sl], ph["k"][sl], NT) * gam, 0.0))
        aqks.append(jnp.where(incl, _mm(ph["q"][sl], ph["k"][sl], NT) * gam, 0.0).astype(BF16))
        g_last = g_c[c - 1:c]
        kgts.append((ph["k"][sl] * jnp.exp(g_last - g_c)).T.astype(BF16))
        decays.append(jnp.exp(g_last))
    tinvs = _neumann_inverses([-low for low in lows])
    o0s, qps, mts, nts = [], [], [], []
    for i, (h, ci) in enumerate(items):
        sl = slice(ci * c, (ci + 1) * c)
        uw = _mm(tinvs[i], per_head[h]["rhs"][sl])
        ow = _mm(aqks[i], uw)
        kw = _mm(kgts[i], uw)
        o0s.append(ow[:, 0:GDN_HEAD])
        qps.append((per_head[h]["qg"][sl] - ow[:, GDN_HEAD:2 * GDN_HEAD]).astype(BF16))
        nts.append(kw[:, 0:GDN_HEAD])
        mts.append((-kw[:, GDN_HEAD:2 * GDN_HEAD]).astype(BF16))

    states = [st_ref[h] for h in range(heads)]
    outs = [[] for _ in range(heads)]
    for ci in range(nchunk):
        for h in range(heads):
            i = h * nchunk + ci
            sb = states[h].astype(BF16)
            outs[h].append(o0s[i] + _mm(qps[i], sb))
            states[h] = states[h] * decays[i] + _mm(mts[i], sb) + nts[i]
    for h in range(heads):
        cols = slice(h * GDN_HEAD, (h + 1) * GDN_HEAD)
        st_ref[h] = states[h]
        o = jnp.concatenate(outs[h], axis=0)
        o_ref[0, :, cols] = _head_rms(o, nw_ref[:, cols], z_ref[0, :, cols]).astype(o_ref.dtype)


def _gdn(proj, small, conv_w, alog_pad, dt_pad, norm_w, tb=256, heads=8):
    b, t, _ = proj.shape
    w = GDN_HEAD * heads
    groups = GDN_HEADS // heads
    base = 3 * GROUP_W // w
    ab_blk = (small.shape[2] - LANES) // LANES

    def col(section):
        return lambda bi, hi, ti: (bi, ti, base + section * groups + hi)

    def cw(section):
        return pl.BlockSpec((4, w), lambda bi, hi, ti: (0, section * groups + hi))

    return pl.pallas_call(
        functools.partial(_gdn_kernel, nchunk=tb // CHUNK, heads=heads),
        out_shape=jax.ShapeDtypeStruct((b, t, GROUP_W), BF16),
        grid=(b, groups, t // tb),
        in_specs=[pl.BlockSpec((1, tb, w), col(0)), pl.BlockSpec((1, tb, w), col(1)),
                  pl.BlockSpec((1, tb, w), col(2)), pl.BlockSpec((1, tb, w), col(3)),
                  pl.BlockSpec((1, tb, LANES), lambda bi, hi, ti: (bi, ti, ab_blk)),
                  cw(0), cw(1), cw(2),
                  pl.BlockSpec((1, LANES), lambda bi, hi, ti: (0, 0)),
                  pl.BlockSpec((1, LANES), lambda bi, hi, ti: (0, 0)),
                  pl.BlockSpec((1, w), lambda bi, hi, ti: (0, hi))],
        out_specs=pl.BlockSpec((1, tb, w), lambda bi, hi, ti: (bi, ti, hi)),
        scratch_shapes=[pltpu.VMEM((heads, GDN_HEAD, GDN_HEAD), F32)]
        + [pltpu.VMEM((SUBLANES, w), F32)] * 3,
        compiler_params=_cparams(("parallel", "parallel", "arbitrary")),
        name="gdn_chunk",
    )(proj, proj, proj, proj, small, conv_w, conv_w, conv_w, alog_pad, dt_pad, norm_w)


def _rwkv_kernel(r_ref, k_ref, v_ref, sm_ref, mur_ref, muk_ref, muv_ref, mus_ref, w0_ref, w2_ref, a0_ref,
                 a2_ref, g2_ref, kk_ref, ka_ref, rk_ref, gw_ref, gb_ref, o_ref,
                 st_ref, cr_s, ck_s, cv_s, cs_s, *, nchunk, lora):
    @pl.when(pl.program_id(2) == 0)
    def _():
        for ref in (st_ref, cr_s, ck_s, cv_s, cs_s):
            ref[...] = jnp.zeros_like(ref)

    tb = r_ref.shape[1]
    half = RWKV_HEAD
    block_diag = (_iota2((LANES, LANES), 0) // half) == (_iota2((LANES, LANES), 1) // half)
    ones_bd = block_diag.astype(BF16)

    def seg_sum(x):
        hi = x.astype(BF16)
        lo = (x - hi.astype(F32)).astype(BF16)
        parts = []
        for j in range(x.shape[1] // LANES):
            cols = slice(j * LANES, (j + 1) * LANES)
            parts.append(lax.dot_general(hi[:, cols], ones_bd, NN, preferred_element_type=F32)
                         + lax.dot_general(lo[:, cols], ones_bd, NN, preferred_element_type=F32))
        return parts[0] if len(parts) == 1 else jnp.concatenate(parts, axis=1)

    def token_shift(x, carry, mu):
        prev = _shifted(x, carry, 1)
        carry[...] = x[tb - SUBLANES:tb]
        return x + (prev - x) * mu

    r = token_shift(r_ref[0], cr_s, mur_ref[...])
    k = token_shift(k_ref[0], ck_s, muk_ref[...])
    v = token_shift(v_ref[0], cv_s, muv_ref[...])
    sm = token_shift(sm_ref[0], cs_s, mus_ref[...])
    wd = sm[:, 0:lora]
    ad = sm[:, lora:2 * lora]
    gd = sm[:, 2 * lora:2 * lora + g2_ref.shape[0]]
    z = w0_ref[...] + _mm_hi(jnp.tanh(wd), w2_ref[...])
    lw = -jnp.exp(-_softplus(-z) - 0.5)
    ag = _sigmoid(a0_ref[...] + _mm_hi(ad, a2_ref[...]))
    gate = _mm(_sigmoid(gd), g2_ref[...])
    kk = k * kk_ref[...]
    kk = kk * lax.rsqrt(seg_sum(kk * kk) + L2_EPS)
    k2 = k * (1.0 + (ag - 1.0) * ka_ref[...])
    aa_all = -kk
    bb_all = kk * ag
    npair = r.shape[1] // LANES

    c = CHUNK
    m0 = _iota2((c, LANES), 1) < half
    row = _iota2((c, LANES), 0)
    sub = _iota2((c, LANES), 1) % c
    strict = sub < row
    incl = sub <= row
    rb = _iota2((tb, tb), 0)
    cb = _iota2((tb, tb), 1)
    chunk_tri = ((cb <= rb) & ((cb // c) == (rb // c))).astype(F32)
    gc_all = _mm_hi(chunk_tri, lw)
    gx_all = gc_all - lw

    def stack(x):
        return jnp.concatenate([jnp.where(m0, x, 0.0), jnp.where(m0, 0.0, x)], axis=0)

    def unstack(x):
        return jnp.where(m0, x[0:c], x[c:2 * c])

    items = [(pi, ci) for pi in range(npair) for ci in range(nchunk)]
    a_bds, ak_xs, wmats, a_sts, v_sws, r_abss, nvs, bes, decays = [], [], [], [], [], [], [], [], []
    for pi, ci in items:
        sl = (slice(ci * c, (ci + 1) * c), slice(pi * LANES, (pi + 1) * LANES))
        rr, kc, vc, aa, bb = r[sl], k2[sl], v[sl], aa_all[sl], bb_all[sl]
        gc, gx = gc_all[sl], gx_all[sl]
        g_mid = gc[c // 2 - 1:c // 2]
        g_last = gc[c - 1:c]
        e_out = jnp.exp(g_mid - gc)
        at = aa * jnp.exp(gx - g_mid)
        rt = rr * jnp.exp(gc - g_mid)
        bt = bb * e_out
        kt = kc * e_out
        lhs0 = jnp.concatenate([jnp.where(m0, at, 0.0), jnp.where(m0, rt, 0.0)], axis=0)
        lhs1 = jnp.concatenate([jnp.where(m0, 0.0, at), jnp.where(m0, 0.0, rt)], axis=0)
        res0 = _mm(lhs0, jnp.concatenate([bt, kt], axis=0), NT)
        res1 = _mm(lhs1, jnp.concatenate([kt, bt], axis=0), NT)
        top0 = jnp.where(strict, res0[0:c], 0.0)
        top1 = jnp.where(strict, res1[0:c], 0.0)
        a_bds.append(jnp.concatenate([jnp.where(m0, top0, 0.0), jnp.where(m0, 0.0, top1)], axis=0))
        ak_xs.append(jnp.concatenate([jnp.where(m0, 0.0, top0), jnp.where(m0, top1, 0.0)], axis=0))
        wmats.append(jnp.concatenate([jnp.where(incl, res0[c:2 * c], 0.0),
                                      jnp.where(incl, res1[c:2 * c], 0.0)], axis=0))
        a_sts.append(stack(aa * jnp.exp(gx)))
        v_sws.append(jnp.concatenate([jnp.where(m0, 0.0, vc), jnp.where(m0, vc, 0.0)], axis=0))
        r_abss.append(rr * jnp.exp(gc))
        e_end = jnp.exp(g_last - gc)
        bes.append(bb * e_end)
        nvs.append((vc, kc * e_end))
        decays.append(jnp.exp(g_last))
    tinvs = _neumann_inverses(a_bds)

    y0s, qps, gbds, ncs = [], [], [], []
    for ci in range(len(items)):
        rhs_v = _mm(ak_xs[ci], v_sws[ci])
        uw = _mm(tinvs[ci], jnp.concatenate([rhs_v, a_sts[ci]], axis=1))
        u_v = uw[:, 0:LANES]
        wh = uw[:, LANES:2 * LANES]
        yw = _mm(wmats[ci], jnp.concatenate([u_v + v_sws[ci], wh], axis=1))
        y0s.append(unstack(yw[:, 0:LANES]))
        qps.append((unstack(yw[:, LANES:2 * LANES]) + r_abss[ci]).astype(BF16))
        gbds.append(jnp.where(block_diag, _mm(unstack(wh), bes[ci], TN), 0.0).astype(BF16))
        vc, ke = nvs[ci]
        ncs.append(jnp.where(block_diag, _mm(jnp.concatenate([unstack(u_v), vc], axis=0),
                                             jnp.concatenate([bes[ci], ke], axis=0), TN), 0.0))

    hts = [st_ref[pi] for pi in range(npair)]
    ys = [[] for _ in range(npair)]
    for ci in range(nchunk):
        for pi in range(npair):
            i = pi * nchunk + ci
            htb = hts[pi].astype(BF16)
            ys[pi].append(y0s[i] + _mm(qps[i], htb, NT))
            hts[pi] = hts[pi] * decays[i] + _mm(htb, gbds[i]) + ncs[i]
    for pi in range(npair):
        st_ref[pi] = hts[pi]

    ycols = [jnp.concatenate(ys[pi], axis=0) for pi in range(npair)]
    y = ycols[0] if npair == 1 else jnp.concatenate(ycols, axis=1)
    mean = seg_sum(y) * (1.0 / half)
    yc = y - mean
    var = seg_sum(yc * yc) * (1.0 / half)
    yn = yc * lax.rsqrt(var + RWKV_GN_EPS) * gw_ref[...] + gb_ref[...]
    bonus = seg_sum(r * k2 * rk_ref[...]) * v
    o_ref[0] = ((yn + bonus) * gate).astype(o_ref.dtype)


def _rwkv(proj, small, mu_main, mu_small, w0, w2p, a0, a2p, g2, kk, ka, rk, gw, gb, lora, tb=256, npair=8):
    b, t, _ = proj.shape
    w = LANES * npair
    groups = GROUP_W // w
    ws = small.shape[2]
    glora = g2.shape[0]

    def col(section):
        return lambda bi, hi, ti: (bi, ti, section * groups + hi)

    def vec(section=0):
        return pl.BlockSpec((1, w), lambda bi, hi, ti: (0, section * groups + hi))

    row_blk = lambda s: pl.BlockSpec((1, tb, w), col(s))
    return pl.pallas_call(
        functools.partial(_rwkv_kernel, nchunk=tb // CHUNK, lora=lora),
        out_shape=jax.ShapeDtypeStruct((b, t, GROUP_W), BF16),
        grid=(b, groups, t // tb),
        in_specs=[row_blk(0), row_blk(1), row_blk(2),
                  pl.BlockSpec((1, tb, ws), lambda bi, hi, ti: (bi, ti, 0)),
                  vec(0), vec(1), vec(2),
                  pl.BlockSpec((1, ws), lambda bi, hi, ti: (0, 0)),
                  vec(), pl.BlockSpec((lora, w), lambda bi, hi, ti: (0, hi)),
                  vec(), pl.BlockSpec((lora, w), lambda bi, hi, ti: (0, hi)),
                  pl.BlockSpec((glora, w), lambda bi, hi, ti: (0, hi)),
                  vec(), vec(), vec(), vec(), vec()],
        out_specs=pl.BlockSpec((1, tb, w), lambda bi, hi, ti: (bi, ti, hi)),
        scratch_shapes=[pltpu.VMEM((npair, LANES, LANES), F32)]
        + [pltpu.VMEM((SUBLANES, w), F32)] * 3
        + [pltpu.VMEM((SUBLANES, ws), F32)],
        compiler_params=_cparams(("parallel", "parallel", "arbitrary")),
        name="rwkv7_chunk",
    )(proj, proj, proj, small, mu_main, mu_main, mu_main, mu_small, w0, w2p, a0, a2p, g2,
      kk, ka, rk, gw, gb)


def _pad_cols(a, width):
    return jnp.pad(a, ((0, 0), (0, width - a.shape[1])))


def _pad_rows(a, height):
    return jnp.pad(a, ((0, height - a.shape[0]), (0, 0)))


def _row(v):
    return v.reshape(1, -1).astype(F32)


def kernel(x, p, hgrn_lb_logits, e_w_in, e_gla_w2, e_gla_b, e_hgrn_norm, e_gla_norm, e_w_out, o_w_in, o_rwkv_mu, o_rwkv_w0, o_rwkv_w2, o_rwkv_a0, o_rwkv_a2, o_rwkv_g2, o_rwkv_kk, o_rwkv_ka, o_rwkv_rk, o_rwkv_gn_w, o_rwkv_gn_b, o_gdn_conv, o_gdn_a_log, o_gdn_dt_bias, o_gdn_norm, o_w_out, ln_mix_w, ln_mix_b, ln_ffn_w, ln_ffn_b, ffn_w_up, ffn_conv, ffn_w_down, ple_w_proj, ple_w_gate):
    bsz, seq, d = x.shape
    m = bsz * seq
    depth = ln_mix_w.shape[0]
    alpha = (2 * depth) ** 0.25
    gw = GROUP_W
    x32 = x.reshape(m, d)
    xb = x32.astype(BF16)
    for layer in range(depth):
        j = layer // 2
        if layer % 2 == 0:
            w_in = e_w_in[j]
            rank = e_gla_w2.shape[1]
            gd0 = 4 * gw + 2 * GLA_HEADS * GLA_DK + gw
            w_main = jnp.concatenate([w_in[:, :gd0], w_in[:, gd0 + rank:]], axis=1).astype(BF16)
            w_gd = _pad_cols(w_in[:, gd0:gd0 + rank], LANES).astype(BF16)
            proj = _matmul(xb, w_main, F32, 1024, 1024).reshape(bsz, seq, -1)
            gd = _matmul(xb, w_gd, F32, 1024, LANES).reshape(bsz, seq, LANES)
            o_a = _hgrn2(proj, hgrn_lb_logits.astype(F32), _row(e_hgrn_norm[j]), layer)
            o_b = _gla(proj, gd, _pad_rows(e_gla_w2[j], LANES), _row(e_gla_b[j]), _row(e_gla_norm[j]))
            w_out = e_w_out[j]
        else:
            w_in = o_w_in[j]
            lora_w = o_rwkv_w2.shape[1]
            lora_a = o_rwkv_a2.shape[1]
            lora_g = o_rwkv_g2.shape[1]
            lora = LANES * (-(-max(lora_w, lora_a) // LANES))
            c_wd = 3 * gw
            c_ad = c_wd + lora_w
            c_gd = c_ad + lora_a
            c_qkv = c_gd + lora_g
            c_z = c_qkv + 3 * gw
            c_ab = c_z + gw
            w_main = jnp.concatenate([w_in[:, :c_wd], w_in[:, c_qkv:c_ab]], axis=1).astype(BF16)
            w_small = jnp.concatenate([_pad_cols(w_in[:, c_wd:c_ad], lora), _pad_cols(w_in[:, c_ad:c_gd], lora),
                                       w_in[:, c_gd:c_qkv], _pad_cols(w_in[:, c_ab:], LANES)], axis=1).astype(BF16)
            mu = o_rwkv_mu[j].reshape(1, -1)
            mu_small = jnp.concatenate([_pad_cols(mu[:, c_wd:c_ad], lora), _pad_cols(mu[:, c_ad:c_gd], lora),
                                        mu[:, c_gd:c_qkv], jnp.zeros((1, LANES), F32)], axis=1)
            proj = _matmul(xb, w_main, F32, 1024, 1024).reshape(bsz, seq, -1)
            small = _matmul(xb, w_small, F32, 1024, w_small.shape[1]).reshape(bsz, seq, -1)
            o_a = _rwkv(proj, small, mu[:, :c_wd], mu_small, _row(o_rwkv_w0[j]),
                        _pad_rows(o_rwkv_w2[j], lora), _row(o_rwkv_a0[j]), _pad_rows(o_rwkv_a2[j], lora),
                        o_rwkv_g2[j], _row(o_rwkv_kk[j]), _row(o_rwkv_ka[j]), _row(o_rwkv_rk[j]),
                        _row(o_rwkv_gn_w[j]), _row(o_rwkv_gn_b[j]), lora)
            o_b = _gdn(proj, small, o_gdn_conv[j].astype(F32),
                       _pad_cols(_row(o_gdn_a_log[j]), LANES), _pad_cols(_row(o_gdn_dt_bias[j]), LANES),
                       _row(o_gdn_norm[j]))
            w_out = o_w_out[j]
        x32, xb = _mix_out(o_a.reshape(m, gw), o_b.reshape(m, gw), w_out.astype(BF16), x32,
                           _row(ln_mix_w[layer]), _row(ln_mix_b[layer]), alpha)
        h = _ffn_up(xb, ffn_w_up[layer].astype(BF16), ffn_conv[layer].astype(F32), seq)
        x32, xb = _ffn_down(h, ffn_w_down[layer].astype(BF16), x32, _row(ln_ffn_w[layer]),
                            _row(ln_ffn_b[layer]), alpha)
        x32, xb = _ple(xb, x32, p[layer].reshape(m, -1).astype(BF16), ple_w_gate[layer].astype(BF16),
                       ple_w_proj[layer].astype(BF16))
    return x32.reshape(bsz, seq, d)
```

```python
import functools

import jax
import jax.numpy as jnp
from jax import lax
from jax.experimental import pallas as pl
from jax.experimental.pallas import tpu as pltpu

F32 = jnp.float32
BF16 = jnp.bfloat16
HIGHEST = lax.Precision.HIGHEST

NN = (((1,), (0,)), ((), ()))
NT = (((1,), (1,)), ((), ()))
TN = (((0,), (0,)), ((), ()))

D_MODEL = 2048
GROUP_W = 1024
CHUNK = 64
LANES = 128
SUBLANES = 8
HGRN_HEADS = 8
GLA_HEADS = 4
GLA_DK = 128
GLA_DV = 256
GLA_TAU = 16.0
RWKV_HEAD = 64
RWKV_GN_EPS = 64e-5
GDN_HEADS = 8
GDN_HEAD = 128
FFN_DIM = 5632
LN_EPS = 1e-5
RMS_EPS = 1e-6
L2_EPS = 1e-6

VMEM_LIMIT = 56 * 1024 * 1024


def _mm(a, b, dims=NN):
    return lax.dot_general(a.astype(BF16), b.astype(BF16), dims, preferred_element_type=F32)


def _mm_hi(a, b, dims=NN):
    return lax.dot_general(a, b, dims, precision=HIGHEST, preferred_element_type=F32)


def _sigmoid(x):
    return 1.0 / (1.0 + jnp.exp(-x))


def _silu(x):
    return x * _sigmoid(x)


def _softplus(x):
    return jnp.maximum(x, 0.0) + jnp.log(1.0 + jnp.exp(-jnp.abs(x)))


def _iota2(shape, dim):
    return lax.broadcasted_iota(jnp.int32, shape, dim)


def _cparams(sem):
    return pltpu.CompilerParams(dimension_semantics=sem, vmem_limit_bytes=VMEM_LIMIT)


def _proj_kernel(x_ref, w_ref, o_ref, wb_ref):
    @pl.when(pl.program_id(1) == 0)
    def _():
        wb_ref[...] = w_ref[...].astype(BF16)

    o_ref[...] = lax.dot_general(x_ref[...], wb_ref[...], NN, preferred_element_type=F32).astype(o_ref.dtype)


def _proj(x, w, n_cols, out_dtype=F32, layer=None, tm=1024, tn=1024):
    m, k = x.shape
    tn = min(tn, n_cols)
    if layer is None:
        w_spec = pl.BlockSpec((k, tn), lambda j, i: (0, j))
    else:
        w_spec = pl.BlockSpec((None, k, tn), lambda j, i: (layer, 0, j))
    return pl.pallas_call(
        _proj_kernel,
        out_shape=jax.ShapeDtypeStruct((m, n_cols), out_dtype),
        grid=(n_cols // tn, m // tm),
        in_specs=[pl.BlockSpec((tm, k), lambda j, i: (i, 0)), w_spec],
        out_specs=pl.BlockSpec((tm, tn), lambda j, i: (i, j)),
        scratch_shapes=[pltpu.VMEM((k, tn), BF16)],
        compiler_params=_cparams(("parallel", "arbitrary")),
        name="proj_matmul",
    )(x, w)


def _layer_norm_rows(y, w, b):
    mu = jnp.mean(y, axis=-1, keepdims=True)
    yc = y - mu
    var = jnp.mean(yc * yc, axis=-1, keepdims=True)
    return yc * lax.rsqrt(var + LN_EPS) * w + b


def _mix_out_kernel(oa_ref, ob_ref, w_ref, x_ref, lw_ref, lb_ref, o32_ref, o16_ref, *, alpha):
    half = oa_ref.shape[1]
    acc = _mm(oa_ref[...], w_ref[0:half, :]) + _mm(ob_ref[...], w_ref[half:2 * half, :])
    y = _layer_norm_rows(alpha * x_ref[...] + acc, lw_ref[...], lb_ref[...])
    o32_ref[...] = y
    o16_ref[...] = y.astype(BF16)


def _mix_out(oa, ob, w, x, lw, lb, alpha, tm=512):
    m, half = oa.shape
    n = w.shape[1]
    row = lambda i: (i, 0)
    fixed = lambda i: (0, 0)
    return pl.pallas_call(
        functools.partial(_mix_out_kernel, alpha=alpha),
        out_shape=(jax.ShapeDtypeStruct((m, n), F32), jax.ShapeDtypeStruct((m, n), BF16)),
        grid=(m // tm,),
        in_specs=[pl.BlockSpec((tm, half), row), pl.BlockSpec((tm, half), row),
                  pl.BlockSpec((2 * half, n), fixed), pl.BlockSpec((tm, n), row),
                  pl.BlockSpec((1, n), fixed), pl.BlockSpec((1, n), fixed)],
        out_specs=(pl.BlockSpec((tm, n), row), pl.BlockSpec((tm, n), row)),
        compiler_params=_cparams(("parallel",)),
        name="mix_out_ln",
    )(oa, ob, w, x, lw, lb)


def _ffn_up_kernel(x_ref, xh_ref, wg_ref, wv_ref, cg_ref, cv_ref, o_ref, wgb_ref, wvb_ref, *, tiles_per_seq):
    i = pl.program_id(1)

    @pl.when(i == 0)
    def _():
        wgb_ref[...] = wg_ref[...].astype(BF16)
        wvb_ref[...] = wv_ref[...].astype(BF16)

    first = (i % tiles_per_seq) == 0
    tm = x_ref.shape[0]

    def conv(w_ref, c_ref):
        u = _mm(x_ref[...], w_ref[...])
        halo = _mm(xh_ref[...], w_ref[...])
        halo = jnp.where(first, 0.0, halo)
        ext = jnp.concatenate([halo, u], axis=0)
        p1 = pltpu.roll(ext, 1, 0)[SUBLANES:SUBLANES + tm]
        p2 = pltpu.roll(ext, 2, 0)[SUBLANES:SUBLANES + tm]
        c = c_ref[...]
        return u * c[2:3] + p1 * c[1:2] + p2 * c[0:1]

    g = conv(wgb_ref, cg_ref)
    v = conv(wvb_ref, cv_ref)
    o_ref[...] = (_silu(g) * v).astype(o_ref.dtype)


def _ffn_up(xb, w_up, conv_w, layer, seq, tm=1024, tf=512):
    m, k = xb.shape
    f = w_up.shape[2] // 2
    nf = f // tf
    hb = tm // SUBLANES
    return pl.pallas_call(
        functools.partial(_ffn_up_kernel, tiles_per_seq=seq // tm),
        out_shape=jax.ShapeDtypeStruct((m, f), BF16),
        grid=(nf, m // tm),
        in_specs=[pl.BlockSpec((tm, k), lambda j, i: (i, 0)),
                  pl.BlockSpec((SUBLANES, k), lambda j, i: (jnp.maximum(i * hb - 1, 0), 0)),
                  pl.BlockSpec((None, k, tf), lambda j, i: (layer, 0, j)),
                  pl.BlockSpec((None, k, tf), lambda j, i: (layer, 0, nf + j)),
                  pl.BlockSpec((None, 3, tf), lambda j, i: (layer, 0, j)),
                  pl.BlockSpec((None, 3, tf), lambda j, i: (layer, 0, nf + j))],
        out_specs=pl.BlockSpec((tm, tf), lambda j, i: (i, j)),
        scratch_shapes=[pltpu.VMEM((k, tf), BF16)] * 2,
        compiler_params=_cparams(("parallel", "arbitrary")),
        name="ffn_up_conv_gate",
    )(xb, xb, w_up, w_up, conv_w, conv_w)


def _ffn_down_kernel(h_ref, w_ref, x_ref, lw_ref, lb_ref, o32_ref, o16_ref, acc_ref, *, alpha, nk):
    k = pl.program_id(1)

    @pl.when(k == 0)
    def _():
        acc_ref[...] = jnp.zeros_like(acc_ref)

    acc_ref[...] += _mm(h_ref[...], w_ref[...])

    @pl.when(k == nk - 1)
    def _():
        y = _layer_norm_rows(alpha * x_ref[...] + acc_ref[...], lw_ref[...], lb_ref[...])
        o32_ref[...] = y
        o16_ref[...] = y.astype(BF16)


def _ffn_down(h, w, x, lw, lb, alpha, tm=512, tk=1408):
    m, kdim = h.shape
    n = w.shape[1]
    nk = kdim // tk
    return pl.pallas_call(
        functools.partial(_ffn_down_kernel, alpha=alpha, nk=nk),
        out_shape=(jax.ShapeDtypeStruct((m, n), F32), jax.ShapeDtypeStruct((m, n), BF16)),
        grid=(m // tm, nk),
        in_specs=[pl.BlockSpec((tm, tk), lambda i, k: (i, k)),
                  pl.BlockSpec((tk, n), lambda i, k: (k, 0)),
                  pl.BlockSpec((tm, n), lambda i, k: (i, 0)),
                  pl.BlockSpec((1, n), lambda i, k: (0, 0)),
                  pl.BlockSpec((1, n), lambda i, k: (0, 0))],
        out_specs=(pl.BlockSpec((tm, n), lambda i, k: (i, 0)),
                   pl.BlockSpec((tm, n), lambda i, k: (i, 0))),
        scratch_shapes=[pltpu.VMEM((tm, n), F32)],
        compiler_params=_cparams(("parallel", "arbitrary")),
        name="ffn_down_ln",
    )(h, w, x, lw, lb)


def _ple_kernel(xb_ref, wg_ref, p_ref, wp_ref, x_ref, o32_ref, o16_ref, wgb_ref):
    @pl.when(pl.program_id(1) == 0)
    def _():
        wgb_ref[...] = wg_ref[...].astype(BF16)

    gate = _sigmoid(_mm(xb_ref[...], wgb_ref[...]))
    y = x_ref[...] + gate * _mm(p_ref[...], wp_ref[...])
    o32_ref[...] = y
    o16_ref[...] = y.astype(BF16)


def _ple(xb, x, p, wg, wp, layer, tm=1024, tn=512):
    m, k = xb.shape
    n = wg.shape[2]
    kp = p.shape[2]
    return pl.pallas_call(
        _ple_kernel,
        out_shape=(jax.ShapeDtypeStruct((m, n), F32), jax.ShapeDtypeStruct((m, n), BF16)),
        grid=(n // tn, m // tm),
        in_specs=[pl.BlockSpec((tm, k), lambda j, i: (i, 0)),
                  pl.BlockSpec((None, k, tn), lambda j, i: (layer, 0, j)),
                  pl.BlockSpec((None, tm, kp), lambda j, i: (layer, i, 0)),
                  pl.BlockSpec((None, kp, tn), lambda j, i: (layer, 0, j)),
                  pl.BlockSpec((tm, tn), lambda j, i: (i, j))],
        out_specs=(pl.BlockSpec((tm, tn), lambda j, i: (i, j)),
                   pl.BlockSpec((tm, tn), lambda j, i: (i, j))),
        scratch_shapes=[pltpu.VMEM((k, tn), BF16)],
        compiler_params=_cparams(("parallel", "arbitrary")),
        name="ple_gate",
    )(xb, wg, p, wp, x)


def _chunk_tri(tb):
    rb = _iota2((tb, tb), 0)
    cb = _iota2((tb, tb), 1)
    return ((cb <= rb) & ((cb // CHUNK) == (rb // CHUNK))).astype(F32)


def _gla_block(q, k, g, v, st_ref, heads, dk, dv):
    tb = q.shape[0]
    c = CHUNK
    causal = _iota2((c, c), 1) <= _iota2((c, c), 0)
    gc_all = _mm_hi(_chunk_tri(tb), g)
    outs = []
    for h in range(heads):
        kc = slice(h * dk, (h + 1) * dk)
        vc = slice(h * dv, (h + 1) * dv)
        st = st_ref[h]
        o_h = []
        for ci in range(tb // c):
            rows = slice(ci * c, (ci + 1) * c)
            gc = gc_all[rows, kc]
            qq, kk, vv = q[rows, kc], k[rows, kc], v[rows, vc]
            g_mid = gc[c // 2:c // 2 + 1]
            g_last = gc[c - 1:c]
            a = _mm(qq * jnp.exp(gc - g_mid), kk * jnp.exp(g_mid - gc), NT)
            a = jnp.where(causal, a, 0.0)
            o_h.append(_mm(a, vv) + _mm(qq * jnp.exp(gc), st, NT))
            st = st * jnp.exp(g_last) + _mm(vv, kk * jnp.exp(g_last - gc), TN)
        st_ref[h] = st
        outs.append(jnp.concatenate(o_h, axis=0))
    return outs


def _head_rms(o, gain, gate):
    o = o * lax.rsqrt(jnp.mean(o * o, axis=-1, keepdims=True) + RMS_EPS)
    return o * gain * _silu(gate)


def _hgrn2_kernel(q_ref, f_ref, i_ref, g_ref, lg_ref, nw_ref, o_ref, st_ref, *, layer, heads, nchunk):
    @pl.when(pl.program_id(2) == 0)
    def _():
        st_ref[...] = jnp.zeros_like(st_ref)

    lg = lg_ref[...]
    e = jnp.exp(lg - jnp.max(lg, axis=0, keepdims=True))
    lb = jnp.sum(e[0:layer + 1], axis=0, keepdims=True) / jnp.sum(e, axis=0, keepdims=True)

    f = lb + (1.0 - lb) * _sigmoid(f_ref[0])
    outs = _gla_block(_silu(q_ref[0]), 1.0 - f, jnp.log(f), i_ref[0], st_ref, heads, LANES, LANES)
    for h in range(heads):
        cols = slice(h * LANES, (h + 1) * LANES)
        o_ref[0, :, cols] = _head_rms(outs[h], nw_ref[:, cols], g_ref[0, :, cols]).astype(o_ref.dtype)


def _hgrn2(proj, logits, norm_w, layer, tb=256, heads=4):
    b, t, _ = proj.shape
    w = LANES * heads
    per = GROUP_W // w

    def col(section):
        return lambda bi, hi, ti: (bi, ti, section * per + hi)

    return pl.pallas_call(
        functools.partial(_hgrn2_kernel, layer=layer, heads=heads, nchunk=tb // CHUNK),
        out_shape=jax.ShapeDtypeStruct((b, t, GROUP_W), BF16),
        grid=(b, per, t // tb),
        in_specs=[pl.BlockSpec((1, tb, w), col(0)), pl.BlockSpec((1, tb, w), col(1)),
                  pl.BlockSpec((1, tb, w), col(2)), pl.BlockSpec((1, tb, w), col(3)),
                  pl.BlockSpec((logits.shape[0], w), lambda bi, hi, ti: (0, hi)),
                  pl.BlockSpec((1, w), lambda bi, hi, ti: (0, hi))],
        out_specs=pl.BlockSpec((1, tb, w), lambda bi, hi, ti: (bi, ti, hi)),
        scratch_shapes=[pltpu.VMEM((heads, LANES, LANES), F32)],
        compiler_params=_cparams(("parallel", "parallel", "arbitrary")),
        name="hgrn2_chunk",
    )(proj, proj, proj, proj, logits, norm_w)


def _gla_kernel(q_ref, k_ref, v_ref, r_ref, gd_ref, w2_ref, b_ref, nw_ref, o_ref, st_ref, *, heads):
    @pl.when(pl.program_id(2) == 0)
    def _():
        st_ref[...] = jnp.zeros_like(st_ref)

    z = _mm_hi(gd_ref[0], w2_ref[...]) + b_ref[...]
    g = -_softplus(-z) * (1.0 / GLA_TAU)
    outs = _gla_block(q_ref[0] * GLA_DK ** -0.5, k_ref[0], g, v_ref[0], st_ref, heads, GLA_DK, GLA_DV)
    for h in range(heads):
        cols = slice(h * GLA_DV, (h + 1) * GLA_DV)
        o_ref[0, :, cols] = _head_rms(outs[h], nw_ref[:, cols], r_ref[0, :, cols]).astype(o_ref.dtype)


def _gla(proj, tail, w2, bias, norm_w, tb=256, heads=2):
    b, t, _ = proj.shape
    wk = GLA_DK * heads
    wv = GLA_DV * heads
    groups = GLA_HEADS // heads
    qk0 = 4 * GROUP_W // wk
    v0 = (4 * GROUP_W + 2 * GLA_HEADS * GLA_DK) // wv
    gd_blk = GROUP_W // LANES
    return pl.pallas_call(
        functools.partial(_gla_kernel, heads=heads),
        out_shape=jax.ShapeDtypeStruct((b, t, GROUP_W), BF16),
        grid=(b, groups, t // tb),
        in_specs=[pl.BlockSpec((1, tb, wk), lambda bi, hi, ti: (bi, ti, qk0 + hi)),
                  pl.BlockSpec((1, tb, wk), lambda bi, hi, ti: (bi, ti, qk0 + groups + hi)),
                  pl.BlockSpec((1, tb, wv), lambda bi, hi, ti: (bi, ti, v0 + hi)),
                  pl.BlockSpec((1, tb, wv), lambda bi, hi, ti: (bi, ti, hi)),
                  pl.BlockSpec((1, tb, LANES), lambda bi, hi, ti: (bi, ti, gd_blk)),
                  pl.BlockSpec((LANES, wk), lambda bi, hi, ti: (0, hi)),
                  pl.BlockSpec((1, wk), lambda bi, hi, ti: (0, hi)),
                  pl.BlockSpec((1, wv), lambda bi, hi, ti: (0, hi))],
        out_specs=pl.BlockSpec((1, tb, wv), lambda bi, hi, ti: (bi, ti, hi)),
        scratch_shapes=[pltpu.VMEM((heads, GLA_DV, GLA_DK), F32)],
        compiler_params=_cparams(("parallel", "parallel", "arbitrary")),
        name="gla_chunk",
    )(proj, proj, proj, tail, tail, w2, bias, norm_w)


def _neumann_inverses(mats):
    n = mats[0].shape[0]
    eye = (_iota2((n, n), 0) == _iota2((n, n), 1)).astype(F32)
    ts = [eye + a for a in mats]
    ps = list(mats)
    steps = max(1, (CHUNK - 1).bit_length() - 1)
    for _ in range(steps):
        ps = [_mm(p, p) for p in ps]
        ts = [t + _mm(t, p) for t, p in zip(ts, ps)]
    return ts


def _neumann_inverse(a):
    return _neumann_inverses([a])[0]


def _shifted(x, carry_ref, shift):
    tb = x.shape[0]
    ext = jnp.concatenate([carry_ref[...], x], axis=0)
    return pltpu.roll(ext, shift, 0)[SUBLANES:SUBLANES + tb]


def _gdn_kernel(q_ref, k_ref, v_ref, z_ref, ab_ref, cq_ref, ck_ref, cv_ref, al_ref, dt_ref, nw_ref,
                o_ref, st_ref, cq_s, ck_s, cv_s, *, nchunk, heads):
    head0 = pl.program_id(1) * heads

    @pl.when(pl.program_id(2) == 0)
    def _():
        st_ref[...] = jnp.zeros_like(st_ref)
        cq_s[...] = jnp.zeros_like(cq_s)
        ck_s[...] = jnp.zeros_like(ck_s)
        cv_s[...] = jnp.zeros_like(cv_s)

    tb = q_ref.shape[1]

    def conv_silu(x_ref, carry, w_ref):
        x = x_ref[0]
        w = w_ref[...]
        y = x * w[3:4]
        for j in (1, 2, 3):
            y = y + _shifted(x, carry, j) * w[3 - j:4 - j]
        carry[...] = x[tb - SUBLANES:tb]
        return _silu(y)

    def l2n(x):
        return x * lax.rsqrt(jnp.sum(x * x, axis=-1, keepdims=True) + L2_EPS)

    q_all = conv_silu(q_ref, cq_s, cq_ref)
    k_all = conv_silu(k_ref, ck_s, ck_ref)
    v_all = conv_silu(v_ref, cv_s, cv_ref)

    c = CHUNK
    incl = _iota2((c, c), 1) <= _iota2((c, c), 0)
    strict = _iota2((c, c), 1) < _iota2((c, c), 0)
    rb = _iota2((tb, tb), 0)
    cb = _iota2((tb, tb), 1)
    chunk_tri = ((cb <= rb) & ((cb // c) == (rb // c))).astype(F32)
    lane = _iota2((tb, LANES), 1)
    ab = ab_ref[0]
    gmat = -jnp.exp(al_ref[...]) * _softplus(ab + dt_ref[...])
    gcum = _mm_hi(chunk_tri, gmat)
    items = [(h, ci) for h in range(heads) for ci in range(nchunk)]
    per_head = []
    for h in range(heads):
        cols = slice(h * GDN_HEAD, (h + 1) * GDN_HEAD)
        q = l2n(q_all[:, cols]) * GDN_HEAD ** -0.5
        k = l2n(k_all[:, cols])
        g_col = jnp.sum(jnp.where(lane == head0 + h, gcum, 0.0), axis=1, keepdims=True)
        beta = _sigmoid(jnp.sum(jnp.where(lane == head0 + h + GDN_HEADS, ab, 0.0), axis=1, keepdims=True))
        eg = jnp.exp(g_col)
        kb = k * beta
        per_head.append(dict(q=q, k=k, g_col=g_col, kb=kb, qg=q * eg,
                             rhs=jnp.concatenate([v_all[:, cols] * beta, kb * eg], axis=1),
                             pick=(_iota2((c, LANES), 1) == head0 + h).astype(F32)))

    lows, aqks, kgts, decays = [], [], [], []
    for h, ci in items:
        ph = per_head[h]
        sl = slice(ci * c, (ci + 1) * c)
        g_c = ph["g_col"][sl]
        g_rows = _mm_hi(ph["pick"], gcum[sl], NT)
        gam = jnp.where(incl, jnp.exp(jnp.where(incl, g_c - g_rows, 0.0)), 0.0)
        lows.append(jnp.where(strict, _mm(ph["kb"][sl], ph["k"][sl], NT) * gam, 0.0))
        aqks.append(jnp.where(incl, _mm(ph["q"][sl], ph["k"][sl], NT) * gam, 0.0).astype(BF16))
        g_last = g_c[c - 1:c]
        kgts.append((ph["k"][sl] * jnp.exp(g_last - g_c)).T.astype(BF16))
        decays.append(jnp.exp(g_last))
    tinvs = _neumann_inverses([-low for low in lows])
    o0s, qps, mts, nts = [], [], [], []
    for i, (h, ci) in enumerate(items):
        sl = slice(ci * c, (ci + 1) * c)
        uw = _mm(tinvs[i], per_head[h]["rhs"][sl])
        ow = _mm(aqks[i], uw)
        kw = _mm(kgts[i], uw)
        o0s.append(ow[:, 0:GDN_HEAD])
        qps.append((per_head[h]["qg"][sl] - ow[:, GDN_HEAD:2 * GDN_HEAD]).astype(BF16))
        nts.append(kw[:, 0:GDN_HEAD])
        mts.append((-kw[:, GDN_HEAD:2 * GDN_HEAD]).astype(BF16))

    states = [st_ref[h] for h in range(heads)]
    outs = [[] for _ in range(heads)]
    for ci in range(nchunk):
        for h in range(heads):
            i = h * nchunk + ci
            sb = states[h].astype(BF16)
            outs[h].append(o0s[i] + _mm(qps[i], sb))
            states[h] = states[h] * decays[i] + _mm(mts[i], sb) + nts[i]
    for h in range(heads):
        cols = slice(h * GDN_HEAD, (h + 1) * GDN_HEAD)
        st_ref[h] = states[h]
        o = jnp.concatenate(outs[h], axis=0)
        o_ref[0, :, cols] = _head_rms(o, nw_ref[:, cols], z_ref[0, :, cols]).astype(o_ref.dtype)


def _gdn(proj, small, conv_w, alog_pad, dt_pad, norm_w, tb=256, heads=8):
    b, t, _ = proj.shape
    w = GDN_HEAD * heads
    groups = GDN_HEADS // heads
    base = 0
    ab_blk = (small.shape[2] - LANES) // LANES

    def col(section):
        return lambda bi, hi, ti: (bi, ti, base + section * groups + hi)

    def cw(section):
        return pl.BlockSpec((4, w), lambda bi, hi, ti: (0, section * groups + hi))

    return pl.pallas_call(
        functools.partial(_gdn_kernel, nchunk=tb // CHUNK, heads=heads),
        out_shape=jax.ShapeDtypeStruct((b, t, GROUP_W), BF16),
        grid=(b, groups, t // tb),
        in_specs=[pl.BlockSpec((1, tb, w), col(0)), pl.BlockSpec((1, tb, w), col(1)),
                  pl.BlockSpec((1, tb, w), col(2)), pl.BlockSpec((1, tb, w), col(3)),
                  pl.BlockSpec((1, tb, LANES), lambda bi, hi, ti: (bi, ti, ab_blk)),
                  cw(0), cw(1), cw(2),
                  pl.BlockSpec((1, LANES), lambda bi, hi, ti: (0, 0)),
                  pl.BlockSpec((1, LANES), lambda bi, hi, ti: (0, 0)),
                  pl.BlockSpec((1, w), lambda bi, hi, ti: (0, hi))],
        out_specs=pl.BlockSpec((1, tb, w), lambda bi, hi, ti: (bi, ti, hi)),
        scratch_shapes=[pltpu.VMEM((heads, GDN_HEAD, GDN_HEAD), F32)]
        + [pltpu.VMEM((SUBLANES, w), F32)] * 3,
        compiler_params=_cparams(("parallel", "parallel", "arbitrary")),
        name="gdn_chunk",
    )(proj, proj, proj, proj, small, conv_w, conv_w, conv_w, alog_pad, dt_pad, norm_w)


def _rwkv_kernel(r_ref, k_ref, v_ref, sm_ref, mur_ref, muk_ref, muv_ref, mus_ref, w0_ref, w2_ref, a0_ref,
                 a2_ref, g2_ref, kk_ref, ka_ref, rk_ref, gw_ref, gb_ref, o_ref,
                 st_ref, cr_s, ck_s, cv_s, cs_s, *, nchunk, lora):
    @pl.when(pl.program_id(2) == 0)
    def _():
        for ref in (st_ref, cr_s, ck_s, cv_s, cs_s):
            ref[...] = jnp.zeros_like(ref)

    tb = r_ref.shape[1]
    half = RWKV_HEAD
    block_diag = (_iota2((LANES, LANES), 0) // half) == (_iota2((LANES, LANES), 1) // half)
    ones_bd = block_diag.astype(BF16)

    def seg_sum(x):
        hi = x.astype(BF16)
        lo = (x - hi.astype(F32)).astype(BF16)
        parts = []
        for j in range(x.shape[1] // LANES):
            cols = slice(j * LANES, (j + 1) * LANES)
            parts.append(lax.dot_general(hi[:, cols], ones_bd, NN, preferred_element_type=F32)
                         + lax.dot_general(lo[:, cols], ones_bd, NN, preferred_element_type=F32))
        return parts[0] if len(parts) == 1 else jnp.concatenate(parts, axis=1)

    def token_shift(x, carry, mu):
        prev = _shifted(x, carry, 1)
        carry[...] = x[tb - SUBLANES:tb]
        return x + (prev - x) * mu

    r = token_shift(r_ref[0], cr_s, mur_ref[...])
    k = token_shift(k_ref[0], ck_s, muk_ref[...])
    v = token_shift(v_ref[0], cv_s, muv_ref[...])
    sm = token_shift(sm_ref[0], cs_s, mus_ref[...])
    wd = sm[:, 0:lora]
    ad = sm[:, lora:2 * lora]
    gd = sm[:, 2 * lora:2 * lora + g2_ref.shape[0]]
    z = w0_ref[...] + _mm_hi(jnp.tanh(wd), w2_ref[...])
    lw = -jnp.exp(-_softplus(-z) - 0.5)
    ag = _sigmoid(a0_ref[...] + _mm_hi(ad, a2_ref[...]))
    gate = _mm(_sigmoid(gd), g2_ref[...])
    kk = k * kk_ref[...]
    kk = kk * lax.rsqrt(seg_sum(kk * kk) + L2_EPS)
    k2 = k * (1.0 + (ag - 1.0) * ka_ref[...])
    aa_all = -kk
    bb_all = kk * ag
    npair = r.shape[1] // LANES

    c = CHUNK
    m0 = _iota2((c, LANES), 1) < half
    row = _iota2((c, LANES), 0)
    sub = _iota2((c, LANES), 1) % c
    strict = sub < row
    incl = sub <= row
    rb = _iota2((tb, tb), 0)
    cb = _iota2((tb, tb), 1)
    chunk_tri = ((cb <= rb) & ((cb // c) == (rb // c))).astype(F32)
    gc_all = _mm_hi(chunk_tri, lw)
    gx_all = gc_all - lw

    def stack(x):
        return jnp.concatenate([jnp.where(m0, x, 0.0), jnp.where(m0, 0.0, x)], axis=0)

    def unstack(x):
        return jnp.where(m0, x[0:c], x[c:2 * c])

    items = [(pi, ci) for pi in range(npair) for ci in range(nchunk)]
    a_bds, ak_xs, wmats, a_sts, v_sws, r_abss, nvs, bes, decays = [], [], [], [], [], [], [], [], []
    for pi, ci in items:
        sl = (slice(ci * c, (ci + 1) * c), slice(pi * LANES, (pi + 1) * LANES))
        rr, kc, vc, aa, bb = r[sl], k2[sl], v[sl], aa_all[sl], bb_all[sl]
        gc, gx = gc_all[sl], gx_all[sl]
        g_mid = gc[c // 2 - 1:c // 2]
        g_last = gc[c - 1:c]
        e_out = jnp.exp(g_mid - gc)
        at = aa * jnp.exp(gx - g_mid)
        rt = rr * jnp.exp(gc - g_mid)
        bt = bb * e_out
        kt = kc * e_out
        lhs0 = jnp.concatenate([jnp.where(m0, at, 0.0), jnp.where(m0, rt, 0.0)], axis=0)
        lhs1 = jnp.concatenate([jnp.where(m0, 0.0, at), jnp.where(m0, 0.0, rt)], axis=0)
        res0 = _mm(lhs0, jnp.concatenate([bt, kt], axis=0), NT)
        res1 = _mm(lhs1, jnp.concatenate([kt, bt], axis=0), NT)
        top0 = jnp.where(strict, res0[0:c], 0.0)
        top1 = jnp.where(strict, res1[0:c], 0.0)
        a_bds.append(jnp.concatenate([jnp.where(m0, top0, 0.0), jnp.where(m0, 0.0, top1)], axis=0))
        ak_xs.append(jnp.concatenate([jnp.where(m0, 0.0, top0), jnp.where(m0, top1, 0.0)], axis=0))
        wmats.append(jnp.concatenate([jnp.where(incl, res0[c:2 * c], 0.0),
                                      jnp.where(incl, res1[c:2 * c], 0.0)], axis=0))
        a_sts.append(stack(aa * jnp.exp(gx)))
        v_sws.append(jnp.concatenate([jnp.where(m0, 0.0, vc), jnp.where(m0, vc, 0.0)], axis=0))
        r_abss.append(rr * jnp.exp(gc))
        e_end = jnp.exp(g_last - gc)
        bes.append(bb * e_end)
        nvs.append((vc, kc * e_end))
        decays.append(jnp.exp(g_last))
    tinvs = _neumann_inverses(a_bds)

    y0s, qps, gbds, ncs = [], [], [], []
    for ci in range(len(items)):
        rhs_v = _mm(ak_xs[ci], v_sws[ci])
        uw = _mm(tinvs[ci], jnp.concatenate([rhs_v, a_sts[ci]], axis=1))
        u_v = uw[:, 0:LANES]
        wh = uw[:, LANES:2 * LANES]
        yw = _mm(wmats[ci], jnp.concatenate([u_v + v_sws[ci], wh], axis=1))
        y0s.append(unstack(yw[:, 0:LANES]))
        qps.append((unstack(yw[:, LANES:2 * LANES]) + r_abss[ci]).astype(BF16))
        gbds.append(jnp.where(block_diag, _mm(unstack(wh), bes[ci], TN), 0.0).astype(BF16))
        vc, ke = nvs[ci]
        ncs.append(jnp.where(block_diag, _mm(jnp.concatenate([unstack(u_v), vc], axis=0),
                                             jnp.concatenate([bes[ci], ke], axis=0), TN), 0.0))

    hts = [st_ref[pi] for pi in range(npair)]
    ys = [[] for _ in range(npair)]
    for ci in range(nchunk):
        for pi in range(npair):
            i = pi * nchunk + ci
            htb = hts[pi].astype(BF16)
            ys[pi].append(y0s[i] + _mm(qps[i], htb, NT))
            hts[pi] = hts[pi] * decays[i] + _mm(htb, gbds[i]) + ncs[i]
    for pi in range(npair):
        st_ref[pi] = hts[pi]

    ycols = [jnp.concatenate(ys[pi], axis=0) for pi in range(npair)]
    y = ycols[0] if npair == 1 else jnp.concatenate(ycols, axis=1)
    mean = seg_sum(y) * (1.0 / half)
    yc = y - mean
    var = seg_sum(yc * yc) * (1.0 / half)
    yn = yc * lax.rsqrt(var + RWKV_GN_EPS) * gw_ref[...] + gb_ref[...]
    bonus = seg_sum(r * k2 * rk_ref[...]) * v
    o_ref[0] = ((yn + bonus) * gate).astype(o_ref.dtype)


def _rwkv(proj, small, mu_main, mu_small, w0, w2p, a0, a2p, g2, kk, ka, rk, gw, gb, lora, tb=256, npair=8):
    b, t, _ = proj.shape
    w = LANES * npair
    groups = GROUP_W // w
    ws = small.shape[2]
    glora = g2.shape[0]

    def col(section):
        return lambda bi, hi, ti: (bi, ti, section * groups + hi)

    def vec(section=0):
        return pl.BlockSpec((1, w), lambda bi, hi, ti: (0, section * groups + hi))

    row_blk = lambda s: pl.BlockSpec((1, tb, w), col(s))
    return pl.pallas_call(
        functools.partial(_rwkv_kernel, nchunk=tb // CHUNK, lora=lora),
        out_shape=jax.ShapeDtypeStruct((b, t, GROUP_W), BF16),
        grid=(b, groups, t // tb),
        in_specs=[row_blk(0), row_blk(1), row_blk(2),
                  pl.BlockSpec((1, tb, ws), lambda bi, hi, ti: (bi, ti, 0)),
                  vec(0), vec(1), vec(2),
                  pl.BlockSpec((1, ws), lambda bi, hi, ti: (0, 0)),
                  vec(), pl.BlockSpec((lora, w), lambda bi, hi, ti: (0, hi)),
                  vec(), pl.BlockSpec((lora, w), lambda bi, hi, ti: (0, hi)),
                  pl.BlockSpec((glora, w), lambda bi, hi, ti: (0, hi)),
                  vec(), vec(), vec(), vec(), vec()],
        out_specs=pl.BlockSpec((1, tb, w), lambda bi, hi, ti: (bi, ti, hi)),
        scratch_shapes=[pltpu.VMEM((npair, LANES, LANES), F32)]
        + [pltpu.VMEM((SUBLANES, w), F32)] * 3
        + [pltpu.VMEM((SUBLANES, ws), F32)],
        compiler_params=_cparams(("parallel", "parallel", "arbitrary")),
        name="rwkv7_chunk",
    )(proj, proj, proj, small, mu_main, mu_main, mu_main, mu_small, w0, w2p, a0, a2p, g2,
      kk, ka, rk, gw, gb)


def _pad_cols(a, width):
    return jnp.pad(a, ((0, 0), (0, width - a.shape[1])))


def _pad_rows(a, height):
    return jnp.pad(a, ((0, height - a.shape[0]), (0, 0)))


def _row(v):
    return v.reshape(1, -1).astype(F32)


def kernel(x, p, hgrn_lb_logits, e_w_in, e_gla_w2, e_gla_b, e_hgrn_norm, e_gla_norm, e_w_out, o_w_in, o_rwkv_mu, o_rwkv_w0, o_rwkv_w2, o_rwkv_a0, o_rwkv_a2, o_rwkv_g2, o_rwkv_kk, o_rwkv_ka, o_rwkv_rk, o_rwkv_gn_w, o_rwkv_gn_b, o_gdn_conv, o_gdn_a_log, o_gdn_dt_bias, o_gdn_norm, o_w_out, ln_mix_w, ln_mix_b, ln_ffn_w, ln_ffn_b, ffn_w_up, ffn_conv, ffn_w_down, ple_w_proj, ple_w_gate):
    bsz, seq, d = x.shape
    m = bsz * seq
    depth = ln_mix_w.shape[0]
    alpha = (2 * depth) ** 0.25
    gw = GROUP_W
    x32 = x.reshape(m, d)
    xb = x32.astype(BF16)
    for layer in range(depth):
        j = layer // 2
        if layer % 2 == 0:
            rank = e_gla_w2.shape[1]
            gd0 = 4 * gw + 2 * GLA_HEADS * GLA_DK + gw
            w_tail = jnp.concatenate([e_w_in[j][:, gd0 + rank:], _pad_cols(e_w_in[j][:, gd0:gd0 + rank], LANES)],
                                     axis=1)
            proj = _proj(xb, e_w_in, gd0, layer=j).reshape(bsz, seq, -1)
            tail = _proj(xb, w_tail, w_tail.shape[1], tn=w_tail.shape[1]).reshape(bsz, seq, -1)
            o_a = _hgrn2(proj, hgrn_lb_logits.astype(F32), _row(e_hgrn_norm[j]), layer)
            o_b = _gla(proj, tail, _pad_rows(e_gla_w2[j], LANES), _row(e_gla_b[j]), _row(e_gla_norm[j]))
            w_out = e_w_out[j]
        else:
            w_in = o_w_in[j]
            lora_w = o_rwkv_w2.shape[1]
            lora_a = o_rwkv_a2.shape[1]
            lora_g = o_rwkv_g2.shape[1]
            lora = LANES * (-(-max(lora_w, lora_a) // LANES))
            c_wd = 3 * gw
            c_ad = c_wd + lora_w
            c_gd = c_ad + lora_a
            c_qkv = c_gd + lora_g
            c_z = c_qkv + 3 * gw
            c_ab = c_z + gw
            w_small = jnp.concatenate([_pad_cols(w_in[:, c_wd:c_ad], lora), _pad_cols(w_in[:, c_ad:c_gd], lora),
                                       w_in[:, c_gd:c_qkv], _pad_cols(w_in[:, c_ab:], LANES)], axis=1)
            mu = o_rwkv_mu[j].reshape(1, -1)
            mu_small = jnp.concatenate([_pad_cols(mu[:, c_wd:c_ad], lora), _pad_cols(mu[:, c_ad:c_gd], lora),
                                        mu[:, c_gd:c_qkv], jnp.zeros((1, LANES), F32)], axis=1)
            proj_rkv = _proj(xb, o_w_in, c_wd, layer=j).reshape(bsz, seq, -1)
            proj_gdn = _proj(xb, w_in[:, c_qkv:c_ab], c_ab - c_qkv).reshape(bsz, seq, -1)
            small = _proj(xb, w_small, w_small.shape[1], tn=w_small.shape[1]).reshape(bsz, seq, -1)
            o_a = _rwkv(proj_rkv, small, mu[:, :c_wd], mu_small, _row(o_rwkv_w0[j]),
                        _pad_rows(o_rwkv_w2[j], lora), _row(o_rwkv_a0[j]), _pad_rows(o_rwkv_a2[j], lora),
                        o_rwkv_g2[j], _row(o_rwkv_kk[j]), _row(o_rwkv_ka[j]), _row(o_rwkv_rk[j]),
                        _row(o_rwkv_gn_w[j]), _row(o_rwkv_gn_b[j]), lora)
            o_b = _gdn(proj_gdn, small, o_gdn_conv[j].astype(F32),
                       _pad_cols(_row(o_gdn_a_log[j]), LANES), _pad_cols(_row(o_gdn_dt_bias[j]), LANES),
                       _row(o_gdn_norm[j]))
            w_out = o_w_out[j]
        x32, xb = _mix_out(o_a.reshape(m, gw), o_b.reshape(m, gw), w_out.astype(BF16), x32,
                           _row(ln_mix_w[layer]), _row(ln_mix_b[layer]), alpha)
        h = _ffn_up(xb, ffn_w_up, ffn_conv, layer, seq)
        x32, xb = _ffn_down(h, ffn_w_down[layer].astype(BF16), x32, _row(ln_ffn_w[layer]),
                            _row(ln_ffn_b[layer]), alpha)
        x32, xb = _ple(xb, x32, p.reshape(depth, m, -1), ple_w_gate, ple_w_proj, layer)
    return x32.reshape(bsz, seq, d)
```

```python
import functools

import jax
import jax.numpy as jnp
from jax import lax
from jax.experimental import pallas as pl
from jax.experimental.pallas import tpu as pltpu

F32 = jnp.float32
BF16 = jnp.bfloat16

NN = (((1,), (0,)), ((), ()))
NT = (((1,), (1,)), ((), ()))
TN = (((0,), (0,)), ((), ()))

D_MODEL = 2048
GROUP_W = 1024
CHUNK = 64
LANES = 128
SUBLANES = 8
HGRN_HEADS = 8
GLA_HEADS = 4
GLA_DK = 128
GLA_DV = 256
GLA_TAU = 16.0
RWKV_HEAD = 64
RWKV_GN_EPS = 64e-5
GDN_HEADS = 8
GDN_HEAD = 128
FFN_DIM = 5632
LN_EPS = 1e-5
RMS_EPS = 1e-6
L2_EPS = 1e-6

VMEM_LIMIT = 56 * 1024 * 1024


def _mm(a, b, dims=NN):
    return lax.dot_general(a.astype(BF16), b.astype(BF16), dims, preferred_element_type=F32)


def _split_bf16(x, parts):
    out = []
    for _ in range(parts - 1):
        h = x.astype(BF16)
        out.append(h)
        x = x - h.astype(F32)
    out.append(x.astype(BF16))
    return out


def _bdot(a, b, dims):
    return lax.dot_general(a, b, dims, preferred_element_type=F32)


def _mm_sel(sel, b, dims=NN):
    s = sel.astype(BF16)
    b1, b2, b3 = _split_bf16(b, 3)
    return (_bdot(s, b3, dims) + _bdot(s, b2, dims)) + _bdot(s, b1, dims)


def _mm_hi(a, b, dims=NN):
    ah, al = _split_bf16(a, 2)
    bh, bl = _split_bf16(b, 2)
    return (_bdot(al, bh, dims) + _bdot(ah, bl, dims)) + _bdot(ah, bh, dims)


def _sigmoid(x):
    return 1.0 / (1.0 + jnp.exp(-x))


def _silu(x):
    return x * _sigmoid(x)


def _softplus(x):
    return jnp.maximum(x, 0.0) + jnp.log(1.0 + jnp.exp(-jnp.abs(x)))


def _iota2(shape, dim):
    return lax.broadcasted_iota(jnp.int32, shape, dim)


def _cparams(sem):
    return pltpu.CompilerParams(dimension_semantics=sem, vmem_limit_bytes=VMEM_LIMIT)


def _proj_kernel(x_ref, w_ref, o_ref, wb_ref):
    @pl.when(pl.program_id(1) == 0)
    def _():
        wb_ref[...] = w_ref[...].astype(BF16)

    o_ref[...] = lax.dot_general(x_ref[...], wb_ref[...], NN, preferred_element_type=F32).astype(o_ref.dtype)


def _proj(x, w, n_cols, out_dtype=F32, layer=None, tm=1024, tn=1024):
    m, k = x.shape
    tn = min(tn, n_cols)
    if layer is None:
        w_spec = pl.BlockSpec((k, tn), lambda j, i: (0, j))
    else:
        w_spec = pl.BlockSpec((None, k, tn), lambda j, i: (layer, 0, j))
    return pl.pallas_call(
        _proj_kernel,
        out_shape=jax.ShapeDtypeStruct((m, n_cols), out_dtype),
        grid=(n_cols // tn, m // tm),
        in_specs=[pl.BlockSpec((tm, k), lambda j, i: (i, 0)), w_spec],
        out_specs=pl.BlockSpec((tm, tn), lambda j, i: (i, j)),
        scratch_shapes=[pltpu.VMEM((k, tn), BF16)],
        compiler_params=_cparams(("parallel", "arbitrary")),
        name="proj_matmul",
    )(x, w)


def _layer_norm_rows(y, w, b):
    mu = jnp.mean(y, axis=-1, keepdims=True)
    yc = y - mu
    var = jnp.mean(yc * yc, axis=-1, keepdims=True)
    return yc * lax.rsqrt(var + LN_EPS) * w + b


def _mix_out_kernel(oa_ref, ob_ref, w_ref, x_ref, lw_ref, lb_ref, o32_ref, o16_ref, *, alpha):
    half = oa_ref.shape[1]
    acc = _mm(oa_ref[...], w_ref[0:half, :]) + _mm(ob_ref[...], w_ref[half:2 * half, :])
    y = _layer_norm_rows(alpha * x_ref[...] + acc, lw_ref[...], lb_ref[...])
    o32_ref[...] = y
    o16_ref[...] = y.astype(BF16)


def _mix_out(oa, ob, w, x, lw, lb, alpha, tm=512):
    m, half = oa.shape
    n = w.shape[1]
    row = lambda i: (i, 0)
    fixed = lambda i: (0, 0)
    return pl.pallas_call(
        functools.partial(_mix_out_kernel, alpha=alpha),
        out_shape=(jax.ShapeDtypeStruct((m, n), F32), jax.ShapeDtypeStruct((m, n), BF16)),
        grid=(m // tm,),
        in_specs=[pl.BlockSpec((tm, half), row), pl.BlockSpec((tm, half), row),
                  pl.BlockSpec((2 * half, n), fixed), pl.BlockSpec((tm, n), row),
                  pl.BlockSpec((1, n), fixed), pl.BlockSpec((1, n), fixed)],
        out_specs=(pl.BlockSpec((tm, n), row), pl.BlockSpec((tm, n), row)),
        compiler_params=_cparams(("parallel",)),
        name="mix_out_ln",
    )(oa, ob, w, x, lw, lb)


def _ffn_up_kernel(x_ref, wg_ref, wv_ref, cg_ref, cv_ref, o_ref, wgb_ref, wvb_ref, hg_ref, hv_ref,
                   *, tiles_per_seq):
    i = pl.program_id(1)
    tm = x_ref.shape[0]

    @pl.when(i == 0)
    def _():
        wgb_ref[...] = wg_ref[...].astype(BF16)
        wvb_ref[...] = wv_ref[...].astype(BF16)
        hg_ref[...] = jnp.zeros_like(hg_ref)
        hv_ref[...] = jnp.zeros_like(hv_ref)

    first = (i % tiles_per_seq) == 0

    def conv(w_ref, h_ref, c_ref):
        u = lax.dot_general(x_ref[...], w_ref[...], NN, preferred_element_type=F32)
        halo = jnp.where(first, 0.0, h_ref[...])
        h_ref[...] = u[tm - SUBLANES:tm]
        ext = jnp.concatenate([halo, u], axis=0)
        p1 = pltpu.roll(ext, 1, 0)[SUBLANES:SUBLANES + tm]
        p2 = pltpu.roll(ext, 2, 0)[SUBLANES:SUBLANES + tm]
        c = c_ref[...]
        return u * c[2:3] + p1 * c[1:2] + p2 * c[0:1]

    g = conv(wgb_ref, hg_ref, cg_ref)
    v = conv(wvb_ref, hv_ref, cv_ref)
    o_ref[...] = (_silu(g) * v).astype(o_ref.dtype)


def _ffn_up(xb, w_up, conv_w, layer, seq, tm=1024, tf=512):
    m, k = xb.shape
    f = w_up.shape[2] // 2
    nf = f // tf
    return pl.pallas_call(
        functools.partial(_ffn_up_kernel, tiles_per_seq=seq // tm),
        out_shape=jax.ShapeDtypeStruct((m, f), BF16),
        grid=(nf, m // tm),
        in_specs=[pl.BlockSpec((tm, k), lambda j, i: (i, 0)),
                  pl.BlockSpec((None, k, tf), lambda j, i: (layer, 0, j)),
                  pl.BlockSpec((None, k, tf), lambda j, i: (layer, 0, nf + j)),
                  pl.BlockSpec((None, 3, tf), lambda j, i: (layer, 0, j)),
                  pl.BlockSpec((None, 3, tf), lambda j, i: (layer, 0, nf + j))],
        out_specs=pl.BlockSpec((tm, tf), lambda j, i: (i, j)),
        scratch_shapes=[pltpu.VMEM((k, tf), BF16)] * 2 + [pltpu.VMEM((SUBLANES, tf), F32)] * 2,
        compiler_params=_cparams(("parallel", "arbitrary")),
        name="ffn_up_conv_gate",
    )(xb, w_up, w_up, conv_w, conv_w)


def _ffn_down_kernel(h_ref, w_ref, x_ref, lw_ref, lb_ref, o32_ref, o16_ref, acc_ref, *, alpha, nk):
    k = pl.program_id(1)

    @pl.when(k == 0)
    def _():
        acc_ref[...] = jnp.zeros_like(acc_ref)

    acc_ref[...] += _mm(h_ref[...], w_ref[...])

    @pl.when(k == nk - 1)
    def _():
        y = _layer_norm_rows(alpha * x_ref[...] + acc_ref[...], lw_ref[...], lb_ref[...])
        o32_ref[...] = y
        o16_ref[...] = y.astype(BF16)


def _ffn_down(h, w, x, lw, lb, alpha, tm=512, tk=2816):
    m, kdim = h.shape
    n = w.shape[1]
    nk = kdim // tk
    return pl.pallas_call(
        functools.partial(_ffn_down_kernel, alpha=alpha, nk=nk),
        out_shape=(jax.ShapeDtypeStruct((m, n), F32), jax.ShapeDtypeStruct((m, n), BF16)),
        grid=(m // tm, nk),
        in_specs=[pl.BlockSpec((tm, tk), lambda i, k: (i, k)),
                  pl.BlockSpec((tk, n), lambda i, k: (k, 0)),
                  pl.BlockSpec((tm, n), lambda i, k: (i, 0)),
                  pl.BlockSpec((1, n), lambda i, k: (0, 0)),
                  pl.BlockSpec((1, n), lambda i, k: (0, 0))],
        out_specs=(pl.BlockSpec((tm, n), lambda i, k: (i, 0)),
                   pl.BlockSpec((tm, n), lambda i, k: (i, 0))),
        scratch_shapes=[pltpu.VMEM((tm, n), F32)],
        compiler_params=_cparams(("parallel", "arbitrary")),
        name="ffn_down_ln",
    )(h, w, x, lw, lb)


def _ple_kernel(xb_ref, wg_ref, p_ref, wp_ref, x_ref, o32_ref, o16_ref, wgb_ref):
    @pl.when(pl.program_id(1) == 0)
    def _():
        wgb_ref[...] = wg_ref[...].astype(BF16)

    gate = _sigmoid(_mm(xb_ref[...], wgb_ref[...]))
    y = x_ref[...] + gate * _mm(p_ref[...], wp_ref[...])
    o32_ref[...] = y
    o16_ref[...] = y.astype(BF16)


def _ple(xb, x, p, wg, wp, layer, tm=1024, tn=512):
    m, k = xb.shape
    n = wg.shape[2]
    kp = p.shape[2]
    return pl.pallas_call(
        _ple_kernel,
        out_shape=(jax.ShapeDtypeStruct((m, n), F32), jax.ShapeDtypeStruct((m, n), BF16)),
        grid=(n // tn, m // tm),
        in_specs=[pl.BlockSpec((tm, k), lambda j, i: (i, 0)),
                  pl.BlockSpec((None, k, tn), lambda j, i: (layer, 0, j)),
                  pl.BlockSpec((None, tm, kp), lambda j, i: (layer, i, 0)),
                  pl.BlockSpec((None, kp, tn), lambda j, i: (layer, 0, j)),
                  pl.BlockSpec((tm, tn), lambda j, i: (i, j))],
        out_specs=(pl.BlockSpec((tm, tn), lambda j, i: (i, j)),
                   pl.BlockSpec((tm, tn), lambda j, i: (i, j))),
        scratch_shapes=[pltpu.VMEM((k, tn), BF16)],
        compiler_params=_cparams(("parallel", "arbitrary")),
        name="ple_gate",
    )(xb, wg, p, wp, x)


def _chunk_tri(tb):
    rb = _iota2((tb, tb), 0)
    cb = _iota2((tb, tb), 1)
    return ((cb <= rb) & ((cb // CHUNK) == (rb // CHUNK))).astype(F32)


def _gla_block(q, k, g, v, st_ref, heads, dk, dv):
    tb = q.shape[0]
    c = CHUNK
    causal = _iota2((c, c), 1) <= _iota2((c, c), 0)
    gc_all = _mm_sel(_chunk_tri(tb), g)
    outs = []
    for h in range(heads):
        kc = slice(h * dk, (h + 1) * dk)
        vc = slice(h * dv, (h + 1) * dv)
        st = st_ref[h]
        o_h = []
        for ci in range(tb // c):
            rows = slice(ci * c, (ci + 1) * c)
            gc = gc_all[rows, kc]
            qq, kk, vv = q[rows, kc], k[rows, kc], v[rows, vc]
            g_mid = gc[c // 2:c // 2 + 1]
            g_last = gc[c - 1:c]
            a = _mm(qq * jnp.exp(gc - g_mid), kk * jnp.exp(g_mid - gc), NT)
            a = jnp.where(causal, a, 0.0)
            o_h.append(_mm(a, vv) + _mm(qq * jnp.exp(gc), st, NT))
            st = st * jnp.exp(g_last) + _mm(vv, kk * jnp.exp(g_last - gc), TN)
        st_ref[h] = st
        outs.append(jnp.concatenate(o_h, axis=0))
    return outs


def _head_rms(o, gain, gate):
    o = o * lax.rsqrt(jnp.mean(o * o, axis=-1, keepdims=True) + RMS_EPS)
    return o * gain * _silu(gate)


def _hgrn2_kernel(q_ref, f_ref, i_ref, g_ref, lg_ref, nw_ref, o_ref, st_ref, *, layer, heads, nchunk):
    @pl.when(pl.program_id(2) == 0)
    def _():
        st_ref[...] = jnp.zeros_like(st_ref)

    lg = lg_ref[...]
    e = jnp.exp(lg - jnp.max(lg, axis=0, keepdims=True))
    lb = jnp.sum(e[0:layer + 1], axis=0, keepdims=True) / jnp.sum(e, axis=0, keepdims=True)

    f = lb + (1.0 - lb) * _sigmoid(f_ref[0])
    outs = _gla_block(_silu(q_ref[0]), 1.0 - f, jnp.log(f), i_ref[0], st_ref, heads, LANES, LANES)
    for h in range(heads):
        cols = slice(h * LANES, (h + 1) * LANES)
        o_ref[0, :, cols] = _head_rms(outs[h], nw_ref[:, cols], g_ref[0, :, cols]).astype(o_ref.dtype)


def _hgrn2(proj, logits, norm_w, layer, tb=256, heads=4):
    b, t, _ = proj.shape
    w = LANES * heads
    per = GROUP_W // w

    def col(section):
        return lambda bi, hi, ti: (bi, ti, section * per + hi)

    return pl.pallas_call(
        functools.partial(_hgrn2_kernel, layer=layer, heads=heads, nchunk=tb // CHUNK),
        out_shape=jax.ShapeDtypeStruct((b, t, GROUP_W), BF16),
        grid=(b, per, t // tb),
        in_specs=[pl.BlockSpec((1, tb, w), col(0)), pl.BlockSpec((1, tb, w), col(1)),
                  pl.BlockSpec((1, tb, w), col(2)), pl.BlockSpec((1, tb, w), col(3)),
                  pl.BlockSpec((logits.shape[0], w), lambda bi, hi, ti: (0, hi)),
                  pl.BlockSpec((1, w), lambda bi, hi, ti: (0, hi))],
        out_specs=pl.BlockSpec((1, tb, w), lambda bi, hi, ti: (bi, ti, hi)),
        scratch_shapes=[pltpu.VMEM((heads, LANES, LANES), F32)],
        compiler_params=_cparams(("parallel", "parallel", "arbitrary")),
        name="hgrn2_chunk",
    )(proj, proj, proj, proj, logits, norm_w)


def _gla_kernel(q_ref, k_ref, v_ref, r_ref, gd_ref, w2_ref, b_ref, nw_ref, o_ref, st_ref, *, heads):
    @pl.when(pl.program_id(2) == 0)
    def _():
        st_ref[...] = jnp.zeros_like(st_ref)

    z = _mm_hi(gd_ref[0], w2_ref[...]) + b_ref[...]
    g = -_softplus(-z) * (1.0 / GLA_TAU)
    outs = _gla_block(q_ref[0] * GLA_DK ** -0.5, k_ref[0], g, v_ref[0], st_ref, heads, GLA_DK, GLA_DV)
    for h in range(heads):
        cols = slice(h * GLA_DV, (h + 1) * GLA_DV)
        o_ref[0, :, cols] = _head_rms(outs[h], nw_ref[:, cols], r_ref[0, :, cols]).astype(o_ref.dtype)


def _gla(proj, tail, w2, bias, norm_w, tb=256, heads=2):
    b, t, _ = proj.shape
    wk = GLA_DK * heads
    wv = GLA_DV * heads
    groups = GLA_HEADS // heads
    qk0 = 4 * GROUP_W // wk
    v0 = (4 * GROUP_W + 2 * GLA_HEADS * GLA_DK) // wv
    gd_blk = GROUP_W // LANES
    return pl.pallas_call(
        functools.partial(_gla_kernel, heads=heads),
        out_shape=jax.ShapeDtypeStruct((b, t, GROUP_W), BF16),
        grid=(b, groups, t // tb),
        in_specs=[pl.BlockSpec((1, tb, wk), lambda bi, hi, ti: (bi, ti, qk0 + hi)),
                  pl.BlockSpec((1, tb, wk), lambda bi, hi, ti: (bi, ti, qk0 + groups + hi)),
                  pl.BlockSpec((1, tb, wv), lambda bi, hi, ti: (bi, ti, v0 + hi)),
                  pl.BlockSpec((1, tb, wv), lambda bi, hi, ti: (bi, ti, hi)),
                  pl.BlockSpec((1, tb, LANES), lambda bi, hi, ti: (bi, ti, gd_blk)),
                  pl.BlockSpec((LANES, wk), lambda bi, hi, ti: (0, hi)),
                  pl.BlockSpec((1, wk), lambda bi, hi, ti: (0, hi)),
                  pl.BlockSpec((1, wv), lambda bi, hi, ti: (0, hi))],
        out_specs=pl.BlockSpec((1, tb, wv), lambda bi, hi, ti: (bi, ti, hi)),
        scratch_shapes=[pltpu.VMEM((heads, GLA_DV, GLA_DK), F32)],
        compiler_params=_cparams(("parallel", "parallel", "arbitrary")),
        name="gla_chunk",
    )(proj, proj, proj, tail, tail, w2, bias, norm_w)


def _neumann_inverses(mats):
    n = mats[0].shape[0]
    eye = (_iota2((n, n), 0) == _iota2((n, n), 1)).astype(F32)
    ts = [eye + a for a in mats]
    ps = list(mats)
    steps = max(1, (CHUNK - 1).bit_length() - 1)
    for _ in range(steps):
        ps = [_mm(p, p) for p in ps]
        ts = [t + _mm(t, p) for t, p in zip(ts, ps)]
    return ts


def _neumann_inverse(a):
    return _neumann_inverses([a])[0]


def _shifted(x, carry_ref, shift):
    tb = x.shape[0]
    ext = jnp.concatenate([carry_ref[...], x], axis=0)
    return pltpu.roll(ext, shift, 0)[SUBLANES:SUBLANES + tb]


def _gdn_kernel(q_ref, k_ref, v_ref, z_ref, ab_ref, cq_ref, ck_ref, cv_ref, al_ref, dt_ref, nw_ref,
                o_ref, st_ref, cq_s, ck_s, cv_s, *, nchunk, heads):
    head0 = pl.program_id(1) * heads

    @pl.when(pl.program_id(2) == 0)
    def _():
        st_ref[...] = jnp.zeros_like(st_ref)
        cq_s[...] = jnp.zeros_like(cq_s)
        ck_s[...] = jnp.zeros_like(ck_s)
        cv_s[...] = jnp.zeros_like(cv_s)

    tb = q_ref.shape[1]

    def conv_silu(x_ref, carry, w_ref):
        x = x_ref[0]
        w = w_ref[...]
        y = x * w[3:4]
        for j in (1, 2, 3):
            y = y + _shifted(x, carry, j) * w[3 - j:4 - j]
        carry[...] = x[tb - SUBLANES:tb]
        return _silu(y)

    def l2n(x):
        return x * lax.rsqrt(jnp.sum(x * x, axis=-1, keepdims=True) + L2_EPS)

    q_all = conv_silu(q_ref, cq_s, cq_ref)
    k_all = conv_silu(k_ref, ck_s, ck_ref)
    v_all = conv_silu(v_ref, cv_s, cv_ref)

    c = CHUNK
    incl = _iota2((c, c), 1) <= _iota2((c, c), 0)
    strict = _iota2((c, c), 1) < _iota2((c, c), 0)
    rb = _iota2((tb, tb), 0)
    cb = _iota2((tb, tb), 1)
    chunk_tri = ((cb <= rb) & ((cb // c) == (rb // c))).astype(F32)
    lane = _iota2((tb, LANES), 1)
    ab = ab_ref[0]
    gmat = -jnp.exp(al_ref[...]) * _softplus(ab + dt_ref[...])
    gcum = _mm_sel(chunk_tri, gmat)
    items = [(h, ci) for h in range(heads) for ci in range(nchunk)]
    per_head = []
    for h in range(heads):
        cols = slice(h * GDN_HEAD, (h + 1) * GDN_HEAD)
        q = l2n(q_all[:, cols]) * GDN_HEAD ** -0.5
        k = l2n(k_all[:, cols])
        g_col = jnp.sum(jnp.where(lane == head0 + h, gcum, 0.0), axis=1, keepdims=True)
        beta = _sigmoid(jnp.sum(jnp.where(lane == head0 + h + GDN_HEADS, ab, 0.0), axis=1, keepdims=True))
        eg = jnp.exp(g_col)
        kb = k * beta
        per_head.append(dict(q=q, k=k, g_col=g_col, kb=kb, qg=q * eg,
                             rhs=jnp.concatenate([v_all[:, cols] * beta, kb * eg], axis=1),
                             pick=(_iota2((c, LANES), 1) == head0 + h).astype(F32)))

    lows, aqks, kgts, decays = [], [], [], []
    for h, ci in items:
        ph = per_head[h]
        sl = slice(ci * c, (ci + 1) * c)
        g_c = ph["g_col"][sl]
        g_rows = _mm_sel(ph["pick"], gcum[sl], NT)
        gam = jnp.where(incl, jnp.exp(jnp.where(incl, g_c - g_rows, 0.0)), 0.0)
        lows.append(jnp.where(strict, _mm(ph["kb"][sl], ph["k"][sl], NT) * gam, 0.0))
        aqks.append(jnp.where(incl, _mm(ph["q"][sl], ph["k"][sl], NT) * gam, 0.0).astype(BF16))
        g_last = g_c[c - 1:c]
        kgts.append((ph["k"][sl] * jnp.exp(g_last - g_c)).T.astype(BF16))
        decays.append(jnp.exp(g_last))
    tinvs = _neumann_inverses([-low for low in lows])
    o0s, qps, mts, nts = [], [], [], []
    for i, (h, ci) in enumerate(items):
        sl = slice(ci * c, (ci + 1) * c)
        uw = _mm(tinvs[i], per_head[h]["rhs"][sl])
        ow = _mm(aqks[i], uw)
        kw = _mm(kgts[i], uw)
        o0s.append(ow[:, 0:GDN_HEAD])
        qps.append((per_head[h]["qg"][sl] - ow[:, GDN_HEAD:2 * GDN_HEAD]).astype(BF16))
        nts.append(kw[:, 0:GDN_HEAD])
        mts.append((-kw[:, GDN_HEAD:2 * GDN_HEAD]).astype(BF16))

    states = [st_ref[h] for h in range(heads)]
    outs = [[] for _ in range(heads)]
    for ci in range(nchunk):
        for h in range(heads):
            i = h * nchunk + ci
            sb = states[h].astype(BF16)
            outs[h].append(o0s[i] + _mm(qps[i], sb))
            states[h] = states[h] * decays[i] + _mm(mts[i], sb) + nts[i]
    for h in range(heads):
        cols = slice(h * GDN_HEAD, (h + 1) * GDN_HEAD)
        st_ref[h] = states[h]
        o = jnp.concatenate(outs[h], axis=0)
        o_ref[0, :, cols] = _head_rms(o, nw_ref[:, cols], z_ref[0, :, cols]).astype(o_ref.dtype)


def _gdn(proj, small, conv_w, alog_pad, dt_pad, norm_w, tb=256, heads=8):
    b, t, _ = proj.shape
    w = GDN_HEAD * heads
    groups = GDN_HEADS // heads
    base = 0
    ab_blk = (small.shape[2] - LANES) // LANES

    def col(section):
        return lambda bi, hi, ti: (bi, ti, base + section * groups + hi)

    def cw(section):
        return pl.BlockSpec((4, w), lambda bi, hi, ti: (0, section * groups + hi))

    return pl.pallas_call(
        functools.partial(_gdn_kernel, nchunk=tb // CHUNK, heads=heads),
        out_shape=jax.ShapeDtypeStruct((b, t, GROUP_W), BF16),
        grid=(b, groups, t // tb),
        in_specs=[pl.BlockSpec((1, tb, w), col(0)), pl.BlockSpec((1, tb, w), col(1)),
                  pl.BlockSpec((1, tb, w), col(2)), pl.BlockSpec((1, tb, w), col(3)),
                  pl.BlockSpec((1, tb, LANES), lambda bi, hi, ti: (bi, ti, ab_blk)),
                  cw(0), cw(1), cw(2),
                  pl.BlockSpec((1, LANES), lambda bi, hi, ti: (0, 0)),
                  pl.BlockSpec((1, LANES), lambda bi, hi, ti: (0, 0)),
                  pl.BlockSpec((1, w), lambda bi, hi, ti: (0, hi))],
        out_specs=pl.BlockSpec((1, tb, w), lambda bi, hi, ti: (bi, ti, hi)),
        scratch_shapes=[pltpu.VMEM((heads, GDN_HEAD, GDN_HEAD), F32)]
        + [pltpu.VMEM((SUBLANES, w), F32)] * 3,
        compiler_params=_cparams(("parallel", "parallel", "arbitrary")),
        name="gdn_chunk",
    )(proj, proj, proj, proj, small, conv_w, conv_w, conv_w, alog_pad, dt_pad, norm_w)


def _rwkv_kernel(r_ref, k_ref, v_ref, sm_ref, mur_ref, muk_ref, muv_ref, mus_ref, w0_ref, w2_ref, a0_ref,
                 a2_ref, g2_ref, kk_ref, ka_ref, rk_ref, gw_ref, gb_ref, o_ref,
                 st_ref, cr_s, ck_s, cv_s, cs_s, *, nchunk, lora):
    @pl.when(pl.program_id(2) == 0)
    def _():
        for ref in (st_ref, cr_s, ck_s, cv_s, cs_s):
            ref[...] = jnp.zeros_like(ref)

    tb = r_ref.shape[1]
    half = RWKV_HEAD
    block_diag = (_iota2((LANES, LANES), 0) // half) == (_iota2((LANES, LANES), 1) // half)
    ones_bd = block_diag.astype(BF16)

    def seg_sum(x):
        hi = x.astype(BF16)
        lo = (x - hi.astype(F32)).astype(BF16)
        parts = []
        for j in range(x.shape[1] // LANES):
            cols = slice(j * LANES, (j + 1) * LANES)
            parts.append(lax.dot_general(hi[:, cols], ones_bd, NN, preferred_element_type=F32)
                         + lax.dot_general(lo[:, cols], ones_bd, NN, preferred_element_type=F32))
        return parts[0] if len(parts) == 1 else jnp.concatenate(parts, axis=1)

    def token_shift(x, carry, mu):
        prev = _shifted(x, carry, 1)
        carry[...] = x[tb - SUBLANES:tb]
        return x + (prev - x) * mu

    r = token_shift(r_ref[0], cr_s, mur_ref[...])
    k = token_shift(k_ref[0], ck_s, muk_ref[...])
    v = token_shift(v_ref[0], cv_s, muv_ref[...])
    sm = token_shift(sm_ref[0], cs_s, mus_ref[...])
    wd = sm[:, 0:lora]
    ad = sm[:, lora:2 * lora]
    gd = sm[:, 2 * lora:2 * lora + g2_ref.shape[0]]
    z = w0_ref[...] + _mm_hi(jnp.tanh(wd), w2_ref[...])
    lw = -jnp.exp(-_softplus(-z) - 0.5)
    ag = _sigmoid(a0_ref[...] + _mm_hi(ad, a2_ref[...]))
    gate = _mm(_sigmoid(gd), g2_ref[...])
    kk = k * kk_ref[...]
    kk = kk * lax.rsqrt(seg_sum(kk * kk) + L2_EPS)
    k2 = k * (1.0 + (ag - 1.0) * ka_ref[...])
    aa_all = -kk
    bb_all = kk * ag
    npair = r.shape[1] // LANES

    c = CHUNK
    m0 = _iota2((c, LANES), 1) < half
    row = _iota2((c, LANES), 0)
    sub = _iota2((c, LANES), 1) % c
    strict = sub < row
    incl = sub <= row
    rb = _iota2((tb, tb), 0)
    cb = _iota2((tb, tb), 1)
    chunk_tri = ((cb <= rb) & ((cb // c) == (rb // c))).astype(F32)
    gc_all = _mm_sel(chunk_tri, lw)
    gx_all = gc_all - lw

    def stack(x):
        return jnp.concatenate([jnp.where(m0, x, 0.0), jnp.where(m0, 0.0, x)], axis=0)

    def unstack(x):
        return jnp.where(m0, x[0:c], x[c:2 * c])

    items = [(pi, ci) for pi in range(npair) for ci in range(nchunk)]
    a_bds, ak_xs, wmats, a_sts, v_sws, r_abss, nvs, bes, decays = [], [], [], [], [], [], [], [], []
    for pi, ci in items:
        sl = (slice(ci * c, (ci + 1) * c), slice(pi * LANES, (pi + 1) * LANES))
        rr, kc, vc, aa, bb = r[sl], k2[sl], v[sl], aa_all[sl], bb_all[sl]
        gc, gx = gc_all[sl], gx_all[sl]
        g_mid = gc[c // 2 - 1:c // 2]
        g_last = gc[c - 1:c]
        e_out = jnp.exp(g_mid - gc)
        at = aa * jnp.exp(gx - g_mid)
        rt = rr * jnp.exp(gc - g_mid)
        bt = bb * e_out
        kt = kc * e_out
        lhs0 = jnp.concatenate([jnp.where(m0, at, 0.0), jnp.where(m0, rt, 0.0)], axis=0)
        lhs1 = jnp.concatenate([jnp.where(m0, 0.0, at), jnp.where(m0, 0.0, rt)], axis=0)
        res0 = _mm(lhs0, jnp.concatenate([bt, kt], axis=0), NT)
        res1 = _mm(lhs1, jnp.concatenate([kt, bt], axis=0), NT)
        top0 = jnp.where(strict, res0[0:c], 0.0)
        top1 = jnp.where(strict, res1[0:c], 0.0)
        a_bds.append(jnp.concatenate([jnp.where(m0, top0, 0.0), jnp.where(m0, 0.0, top1)], axis=0))
        ak_xs.append(jnp.concatenate([jnp.where(m0, 0.0, top0), jnp.where(m0, top1, 0.0)], axis=0))
        wmats.append(jnp.concatenate([jnp.where(incl, res0[c:2 * c], 0.0),
                                      jnp.where(incl, res1[c:2 * c], 0.0)], axis=0))
        a_sts.append(stack(aa * jnp.exp(gx)))
        v_sws.append(jnp.concatenate([jnp.where(m0, 0.0, vc), jnp.where(m0, vc, 0.0)], axis=0))
        r_abss.append(rr * jnp.exp(gc))
        e_end = jnp.exp(g_last - gc)
        bes.append(bb * e_end)
        nvs.append((vc, kc * e_end))
        decays.append(jnp.exp(g_last))
    tinvs = _neumann_inverses(a_bds)

    y0s, qps, gbds, ncs = [], [], [], []
    for ci in range(len(items)):
        rhs_v = _mm(ak_xs[ci], v_sws[ci])
        uw = _mm(tinvs[ci], jnp.concatenate([rhs_v, a_sts[ci]], axis=1))
        u_v = uw[:, 0:LANES]
        wh = uw[:, LANES:2 * LANES]
        yw = _mm(wmats[ci], jnp.concatenate([u_v + v_sws[ci], wh], axis=1))
        y0s.append(unstack(yw[:, 0:LANES]))
        qps.append((unstack(yw[:, LANES:2 * LANES]) + r_abss[ci]).astype(BF16))
        gbds.append(jnp.where(block_diag, _mm(unstack(wh), bes[ci], TN), 0.0).astype(BF16))
        vc, ke = nvs[ci]
        ncs.append(jnp.where(block_diag, _mm(jnp.concatenate([unstack(u_v), vc], axis=0),
                                             jnp.concatenate([bes[ci], ke], axis=0), TN), 0.0))

    hts = [st_ref[pi] for pi in range(npair)]
    ys = [[] for _ in range(npair)]
    for ci in range(nchunk):
        for pi in range(npair):
            i = pi * nchunk + ci
            htb = hts[pi].astype(BF16)
            ys[pi].append(y0s[i] + _mm(qps[i], htb, NT))
            hts[pi] = hts[pi] * decays[i] + _mm(htb, gbds[i]) + ncs[i]
    for pi in range(npair):
        st_ref[pi] = hts[pi]

    ycols = [jnp.concatenate(ys[pi], axis=0) for pi in range(npair)]
    y = ycols[0] if npair == 1 else jnp.concatenate(ycols, axis=1)
    mean = seg_sum(y) * (1.0 / half)
    yc = y - mean
    var = seg_sum(yc * yc) * (1.0 / half)
    yn = yc * lax.rsqrt(var + RWKV_GN_EPS) * gw_ref[...] + gb_ref[...]
    bonus = seg_sum(r * k2 * rk_ref[...]) * v
    o_ref[0] = ((yn + bonus) * gate).astype(o_ref.dtype)


def _rwkv(proj, small, mu_main, mu_small, w0, w2p, a0, a2p, g2, kk, ka, rk, gw, gb, lora, tb=256, npair=8):
    b, t, _ = proj.shape
    w = LANES * npair
    groups = GROUP_W // w
    ws = small.shape[2]
    glora = g2.shape[0]

    def col(section):
        return lambda bi, hi, ti: (bi, ti, section * groups + hi)

    def vec(section=0):
        return pl.BlockSpec((1, w), lambda bi, hi, ti: (0, section * groups + hi))

    row_blk = lambda s: pl.BlockSpec((1, tb, w), col(s))
    return pl.pallas_call(
        functools.partial(_rwkv_kernel, nchunk=tb // CHUNK, lora=lora),
        out_shape=jax.ShapeDtypeStruct((b, t, GROUP_W), BF16),
        grid=(b, groups, t // tb),
        in_specs=[row_blk(0), row_blk(1), row_blk(2),
                  pl.BlockSpec((1, tb, ws), lambda bi, hi, ti: (bi, ti, 0)),
                  vec(0), vec(1), vec(2),
                  pl.BlockSpec((1, ws), lambda bi, hi, ti: (0, 0)),
                  vec(), pl.BlockSpec((lora, w), lambda bi, hi, ti: (0, hi)),
                  vec(), pl.BlockSpec((lora, w), lambda bi, hi, ti: (0, hi)),
                  pl.BlockSpec((glora, w), lambda bi, hi, ti: (0, hi)),
                  vec(), vec(), vec(), vec(), vec()],
        out_specs=pl.BlockSpec((1, tb, w), lambda bi, hi, ti: (bi, ti, hi)),
        scratch_shapes=[pltpu.VMEM((npair, LANES, LANES), F32)]
        + [pltpu.VMEM((SUBLANES, w), F32)] * 3
        + [pltpu.VMEM((SUBLANES, ws), F32)],
        compiler_params=_cparams(("parallel", "parallel", "arbitrary")),
        name="rwkv7_chunk",
    )(proj, proj, proj, small, mu_main, mu_main, mu_main, mu_small, w0, w2p, a0, a2p, g2,
      kk, ka, rk, gw, gb)


def _pad_cols(a, width):
    return jnp.pad(a, ((0, 0), (0, width - a.shape[1])))


def _pad_rows(a, height):
    return jnp.pad(a, ((0, height - a.shape[0]), (0, 0)))


def _row(v):
    return v.reshape(1, -1).astype(F32)


def kernel(x, p, hgrn_lb_logits, e_w_in, e_gla_w2, e_gla_b, e_hgrn_norm, e_gla_norm, e_w_out, o_w_in, o_rwkv_mu, o_rwkv_w0, o_rwkv_w2, o_rwkv_a0, o_rwkv_a2, o_rwkv_g2, o_rwkv_kk, o_rwkv_ka, o_rwkv_rk, o_rwkv_gn_w, o_rwkv_gn_b, o_gdn_conv, o_gdn_a_log, o_gdn_dt_bias, o_gdn_norm, o_w_out, ln_mix_w, ln_mix_b, ln_ffn_w, ln_ffn_b, ffn_w_up, ffn_conv, ffn_w_down, ple_w_proj, ple_w_gate):
    bsz, seq, d = x.shape
    m = bsz * seq
    depth = ln_mix_w.shape[0]
    alpha = (2 * depth) ** 0.25
    gw = GROUP_W
    x32 = x.reshape(m, d)
    xb = x32.astype(BF16)
    for layer in range(depth):
        j = layer // 2
        if layer % 2 == 0:
            rank = e_gla_w2.shape[1]
            gd0 = 4 * gw + 2 * GLA_HEADS * GLA_DK + gw
            w_tail = jnp.concatenate([e_w_in[j][:, gd0 + rank:], _pad_cols(e_w_in[j][:, gd0:gd0 + rank], LANES)],
                                     axis=1)
            proj = _proj(xb, e_w_in, gd0, layer=j).reshape(bsz, seq, -1)
            tail = _proj(xb, w_tail, w_tail.shape[1], tn=w_tail.shape[1]).reshape(bsz, seq, -1)
            o_a = _hgrn2(proj, hgrn_lb_logits.astype(F32), _row(e_hgrn_norm[j]), layer)
            o_b = _gla(proj, tail, _pad_rows(e_gla_w2[j], LANES), _row(e_gla_b[j]), _row(e_gla_norm[j]))
            w_out = e_w_out[j]
        else:
            w_in = o_w_in[j]
            lora_w = o_rwkv_w2.shape[1]
            lora_a = o_rwkv_a2.shape[1]
            lora_g = o_rwkv_g2.shape[1]
            lora = LANES * (-(-max(lora_w, lora_a) // LANES))
            c_wd = 3 * gw
            c_ad = c_wd + lora_w
            c_gd = c_ad + lora_a
            c_qkv = c_gd + lora_g
            c_z = c_qkv + 3 * gw
            c_ab = c_z + gw
            w_small = jnp.concatenate([_pad_cols(w_in[:, c_wd:c_ad], lora), _pad_cols(w_in[:, c_ad:c_gd], lora),
                                       w_in[:, c_gd:c_qkv], _pad_cols(w_in[:, c_ab:], LANES)], axis=1)
            mu = o_rwkv_mu[j].reshape(1, -1)
            mu_small = jnp.concatenate([_pad_cols(mu[:, c_wd:c_ad], lora), _pad_cols(mu[:, c_ad:c_gd], lora),
                                        mu[:, c_gd:c_qkv], jnp.zeros((1, LANES), F32)], axis=1)
            proj_rkv = _proj(xb, o_w_in, c_wd, layer=j).reshape(bsz, seq, -1)
            proj_gdn = _proj(xb, w_in[:, c_qkv:c_ab], c_ab - c_qkv).reshape(bsz, seq, -1)
            small = _proj(xb, w_small, w_small.shape[1], tn=w_small.shape[1]).reshape(bsz, seq, -1)
            o_a = _rwkv(proj_rkv, small, mu[:, :c_wd], mu_small, _row(o_rwkv_w0[j]),
                        _pad_rows(o_rwkv_w2[j], lora), _row(o_rwkv_a0[j]), _pad_rows(o_rwkv_a2[j], lora),
                        o_rwkv_g2[j], _row(o_rwkv_kk[j]), _row(o_rwkv_ka[j]), _row(o_rwkv_rk[j]),
                        _row(o_rwkv_gn_w[j]), _row(o_rwkv_gn_b[j]), lora)
            o_b = _gdn(proj_gdn, small, o_gdn_conv[j].astype(F32),
                       _pad_cols(_row(o_gdn_a_log[j]), LANES), _pad_cols(_row(o_gdn_dt_bias[j]), LANES),
                       _row(o_gdn_norm[j]))
            w_out = o_w_out[j]
        x32, xb = _mix_out(o_a.reshape(m, gw), o_b.reshape(m, gw), w_out.astype(BF16), x32,
                           _row(ln_mix_w[layer]), _row(ln_mix_b[layer]), alpha)
        h = _ffn_up(xb, ffn_w_up, ffn_conv, layer, seq)
        x32, xb = _ffn_down(h, ffn_w_down[layer].astype(BF16), x32, _row(ln_ffn_w[layer]),
                            _row(ln_ffn_b[layer]), alpha)
        x32, xb = _ple(xb, x32, p.reshape(depth, m, -1), ple_w_gate, ple_w_proj, layer)
    return x32.reshape(bsz, seq, d)
```

```python
import functools

import jax
import jax.numpy as jnp
from jax import lax
from jax.experimental import pallas as pl
from jax.experimental.pallas import tpu as pltpu

F32 = jnp.float32
BF16 = jnp.bfloat16

NN = (((1,), (0,)), ((), ()))
NT = (((1,), (1,)), ((), ()))
TN = (((0,), (0,)), ((), ()))

D_MODEL = 2048
GROUP_W = 1024
CHUNK = 64
LANES = 128
SUBLANES = 8
HGRN_HEADS = 8
GLA_HEADS = 4
GLA_DK = 128
GLA_DV = 256
GLA_TAU = 16.0
RWKV_HEAD = 64
RWKV_GN_EPS = 64e-5
GDN_HEADS = 8
GDN_HEAD = 128
FFN_DIM = 5632
LN_EPS = 1e-5
RMS_EPS = 1e-6
L2_EPS = 1e-6

VMEM_LIMIT = 56 * 1024 * 1024


def _mm(a, b, dims=NN):
    return lax.dot_general(a.astype(BF16), b.astype(BF16), dims, preferred_element_type=F32)


def _split_bf16(x, parts):
    out = []
    for _ in range(parts - 1):
        h = x.astype(BF16)
        out.append(h)
        x = x - h.astype(F32)
    out.append(x.astype(BF16))
    return out


def _bdot(a, b, dims):
    return lax.dot_general(a, b, dims, preferred_element_type=F32)


def _mm_sel(sel, b, dims=NN):
    s = sel.astype(BF16)
    b1, b2, b3 = _split_bf16(b, 3)
    return (_bdot(s, b3, dims) + _bdot(s, b2, dims)) + _bdot(s, b1, dims)


def _mm_hi(a, b, dims=NN):
    ah, al = _split_bf16(a, 2)
    bh, bl = _split_bf16(b, 2)
    return (_bdot(al, bh, dims) + _bdot(ah, bl, dims)) + _bdot(ah, bh, dims)


def _sigmoid(x):
    return 1.0 / (1.0 + jnp.exp(-x))


def _silu(x):
    return x * _sigmoid(x)


def _softplus(x):
    return jnp.maximum(x, 0.0) + jnp.log(1.0 + jnp.exp(-jnp.abs(x)))


def _iota2(shape, dim):
    return lax.broadcasted_iota(jnp.int32, shape, dim)


def _cparams(sem):
    return pltpu.CompilerParams(dimension_semantics=sem, vmem_limit_bytes=VMEM_LIMIT)


def _proj_kernel(x_ref, w_ref, o_ref, wb_ref):
    @pl.when(pl.program_id(1) == 0)
    def _():
        wb_ref[...] = w_ref[...].T.astype(BF16)

    o_ref[...] = lax.dot_general(x_ref[...], wb_ref[...], NN, preferred_element_type=F32).astype(o_ref.dtype)


def _proj(x, wt, n_cols, out_dtype=F32, layer=None, tm=1024, tn=1024):
    m, k = x.shape
    tn = min(tn, n_cols)
    if layer is None:
        w_spec = pl.BlockSpec((tn, k), lambda j, i: (j, 0))
    else:
        w_spec = pl.BlockSpec((None, tn, k), lambda j, i: (layer, j, 0))
    return pl.pallas_call(
        _proj_kernel,
        out_shape=jax.ShapeDtypeStruct((m, n_cols), out_dtype),
        grid=(n_cols // tn, m // tm),
        in_specs=[pl.BlockSpec((tm, k), lambda j, i: (i, 0)), w_spec],
        out_specs=pl.BlockSpec((tm, tn), lambda j, i: (i, j)),
        scratch_shapes=[pltpu.VMEM((k, tn), BF16)],
        compiler_params=_cparams(("parallel", "arbitrary")),
        name="proj_matmul",
    )(x, wt)


def _layer_norm_rows(y, w, b):
    mu = jnp.mean(y, axis=-1, keepdims=True)
    yc = y - mu
    var = jnp.mean(yc * yc, axis=-1, keepdims=True)
    return yc * lax.rsqrt(var + LN_EPS) * w + b


def _mix_out_kernel(oa_ref, ob_ref, w_ref, x_ref, lw_ref, lb_ref, o32_ref, o16_ref, *, alpha):
    half = oa_ref.shape[1]
    acc = _mm(oa_ref[...], w_ref[0:half, :]) + _mm(ob_ref[...], w_ref[half:2 * half, :])
    y = _layer_norm_rows(alpha * x_ref[...] + acc, lw_ref[...], lb_ref[...])
    o32_ref[...] = y
    o16_ref[...] = y.astype(BF16)


def _mix_out(oa, ob, w, x, lw, lb, alpha, tm=512):
    m, half = oa.shape
    n = w.shape[1]
    row = lambda i: (i, 0)
    fixed = lambda i: (0, 0)
    return pl.pallas_call(
        functools.partial(_mix_out_kernel, alpha=alpha),
        out_shape=(jax.ShapeDtypeStruct((m, n), F32), jax.ShapeDtypeStruct((m, n), BF16)),
        grid=(m // tm,),
        in_specs=[pl.BlockSpec((tm, half), row), pl.BlockSpec((tm, half), row),
                  pl.BlockSpec((2 * half, n), fixed), pl.BlockSpec((tm, n), row),
                  pl.BlockSpec((1, n), fixed), pl.BlockSpec((1, n), fixed)],
        out_specs=(pl.BlockSpec((tm, n), row), pl.BlockSpec((tm, n), row)),
        compiler_params=_cparams(("parallel",)),
        name="mix_out_ln",
    )(oa, ob, w, x, lw, lb)


def _ffn_up_kernel(x_ref, wg_ref, wv_ref, cg_ref, cv_ref, o_ref, wgb_ref, wvb_ref, hg_ref, hv_ref,
                   *, tiles_per_seq):
    i = pl.program_id(1)
    tm = x_ref.shape[0]

    @pl.when(i == 0)
    def _():
        wgb_ref[...] = wg_ref[...].astype(BF16)
        wvb_ref[...] = wv_ref[...].astype(BF16)
        hg_ref[...] = jnp.zeros_like(hg_ref)
        hv_ref[...] = jnp.zeros_like(hv_ref)

    first = (i % tiles_per_seq) == 0

    def conv(w_ref, h_ref, c_ref):
        u = lax.dot_general(x_ref[...], w_ref[...], NN, preferred_element_type=F32)
        halo = jnp.where(first, 0.0, h_ref[...])
        h_ref[...] = u[tm - SUBLANES:tm]
        ext = jnp.concatenate([halo, u], axis=0)
        p1 = pltpu.roll(ext, 1, 0)[SUBLANES:SUBLANES + tm]
        p2 = pltpu.roll(ext, 2, 0)[SUBLANES:SUBLANES + tm]
        c = c_ref[...]
        return u * c[2:3] + p1 * c[1:2] + p2 * c[0:1]

    g = conv(wgb_ref, hg_ref, cg_ref)
    v = conv(wvb_ref, hv_ref, cv_ref)
    o_ref[...] = (_silu(g) * v).astype(o_ref.dtype)


def _ffn_up(xb, w_up, conv_w, layer, seq, tm=1024, tf=512):
    m, k = xb.shape
    f = w_up.shape[2] // 2
    nf = f // tf
    return pl.pallas_call(
        functools.partial(_ffn_up_kernel, tiles_per_seq=seq // tm),
        out_shape=jax.ShapeDtypeStruct((m, f), BF16),
        grid=(nf, m // tm),
        in_specs=[pl.BlockSpec((tm, k), lambda j, i: (i, 0)),
                  pl.BlockSpec((None, k, tf), lambda j, i: (layer, 0, j)),
                  pl.BlockSpec((None, k, tf), lambda j, i: (layer, 0, nf + j)),
                  pl.BlockSpec((None, 3, tf), lambda j, i: (layer, 0, j)),
                  pl.BlockSpec((None, 3, tf), lambda j, i: (layer, 0, nf + j))],
        out_specs=pl.BlockSpec((tm, tf), lambda j, i: (i, j)),
        scratch_shapes=[pltpu.VMEM((k, tf), BF16)] * 2 + [pltpu.VMEM((SUBLANES, tf), F32)] * 2,
        compiler_params=_cparams(("parallel", "arbitrary")),
        name="ffn_up_conv_gate",
    )(xb, w_up, w_up, conv_w, conv_w)


def _ffn_down_kernel(h_ref, w_ref, x_ref, lw_ref, lb_ref, o32_ref, o16_ref, acc_ref, *, alpha, nk):
    k = pl.program_id(1)

    @pl.when(k == 0)
    def _():
        acc_ref[...] = _mm(h_ref[...], w_ref[...])

    @pl.when(k > 0)
    def _():
        acc_ref[...] += _mm(h_ref[...], w_ref[...])

    @pl.when(k == nk - 1)
    def _():
        y = _layer_norm_rows(alpha * x_ref[...] + acc_ref[...], lw_ref[...], lb_ref[...])
        o32_ref[...] = y
        o16_ref[...] = y.astype(BF16)


def _ffn_down(h, w, x, lw, lb, alpha, layer, tm=512, tk=2816):
    m, kdim = h.shape
    n = w.shape[2]
    nk = kdim // tk
    assert nk >= 2
    return pl.pallas_call(
        functools.partial(_ffn_down_kernel, alpha=alpha, nk=nk),
        out_shape=(jax.ShapeDtypeStruct((m, n), F32), jax.ShapeDtypeStruct((m, n), BF16)),
        grid=(m // tm, nk),
        in_specs=[pl.BlockSpec((tm, tk), lambda i, k: (i, k)),
                  pl.BlockSpec((None, tk, n), lambda i, k: (layer, k, 0)),
                  pl.BlockSpec((tm, n), lambda i, k: (i, 0)),
                  pl.BlockSpec((1, n), lambda i, k: (0, 0)),
                  pl.BlockSpec((1, n), lambda i, k: (0, 0))],
        out_specs=(pl.BlockSpec((tm, n), lambda i, k: (i, 0)),
                   pl.BlockSpec((tm, n), lambda i, k: (i, 0))),
        scratch_shapes=[pltpu.VMEM((tm, n), F32)],
        compiler_params=_cparams(("parallel", "arbitrary")),
        name="ffn_down_ln",
    )(h, w, x, lw, lb)


def _ple_kernel(xb_ref, wg_ref, p_ref, wp_ref, x_ref, o32_ref, o16_ref, wgb_ref):
    @pl.when(pl.program_id(1) == 0)
    def _():
        wgb_ref[...] = wg_ref[...].astype(BF16)

    gate = _sigmoid(_mm(xb_ref[...], wgb_ref[...]))
    y = x_ref[...] + gate * _mm(p_ref[...], wp_ref[...])
    o32_ref[...] = y
    o16_ref[...] = y.astype(BF16)


def _ple(xb, x, p, wg, wp, layer, tm=512, tn=1024):
    m, k = xb.shape
    n = wg.shape[2]
    kp = p.shape[2]
    return pl.pallas_call(
        _ple_kernel,
        out_shape=(jax.ShapeDtypeStruct((m, n), F32), jax.ShapeDtypeStruct((m, n), BF16)),
        grid=(n // tn, m // tm),
        in_specs=[pl.BlockSpec((tm, k), lambda j, i: (i, 0)),
                  pl.BlockSpec((None, k, tn), lambda j, i: (layer, 0, j)),
                  pl.BlockSpec((None, tm, kp), lambda j, i: (layer, i, 0)),
                  pl.BlockSpec((None, kp, tn), lambda j, i: (layer, 0, j)),
                  pl.BlockSpec((tm, tn), lambda j, i: (i, j))],
        out_specs=(pl.BlockSpec((tm, tn), lambda j, i: (i, j)),
                   pl.BlockSpec((tm, tn), lambda j, i: (i, j))),
        scratch_shapes=[pltpu.VMEM((k, tn), BF16)],
        compiler_params=_cparams(("parallel", "arbitrary")),
        name="ple_gate",
    )(xb, wg, p, wp, x)


def _chunk_tri(tb):
    rb = _iota2((tb, tb), 0)
    cb = _iota2((tb, tb), 1)
    return ((cb <= rb) & ((cb // CHUNK) == (rb // CHUNK))).astype(F32)


def _gla_block(q, k, g, v, st_ref, heads, dk, dv):
    tb = q.shape[0]
    c = CHUNK
    causal = _iota2((c, c), 1) <= _iota2((c, c), 0)
    gc_all = _mm_sel(_chunk_tri(tb), g)
    outs = []
    for h in range(heads):
        kc = slice(h * dk, (h + 1) * dk)
        vc = slice(h * dv, (h + 1) * dv)
        st = st_ref[h]
        o_h = []
        for ci in range(tb // c):
            rows = slice(ci * c, (ci + 1) * c)
            gc = gc_all[rows, kc]
            qq, kk, vv = q[rows, kc], k[rows, kc], v[rows, vc]
            g_mid = gc[c // 2:c // 2 + 1]
            g_last = gc[c - 1:c]
            a = _mm(qq * jnp.exp(gc - g_mid), kk * jnp.exp(g_mid - gc), NT)
            a = jnp.where(causal, a, 0.0)
            o_h.append(_mm(a, vv) + _mm(qq * jnp.exp(gc), st, NT))
            st = st * jnp.exp(g_last) + _mm(vv, kk * jnp.exp(g_last - gc), TN)
        st_ref[h] = st
        outs.append(jnp.concatenate(o_h, axis=0))
    return outs


def _head_rms(o, gain, gate):
    o = o * lax.rsqrt(jnp.mean(o * o, axis=-1, keepdims=True) + RMS_EPS)
    return o * gain * _silu(gate)


def _hgrn2_kernel(q_ref, f_ref, i_ref, g_ref, lg_ref, nw_ref, o_ref, st_ref, *, layer, heads, nchunk):
    @pl.when(pl.program_id(2) == 0)
    def _():
        st_ref[...] = jnp.zeros_like(st_ref)

    lg = lg_ref[...]
    e = jnp.exp(lg - jnp.max(lg, axis=0, keepdims=True))
    lb = jnp.sum(e[0:layer + 1], axis=0, keepdims=True) / jnp.sum(e, axis=0, keepdims=True)

    f = lb + (1.0 - lb) * _sigmoid(f_ref[0])
    outs = _gla_block(_silu(q_ref[0]), 1.0 - f, jnp.log(f), i_ref[0], st_ref, heads, LANES, LANES)
    for h in range(heads):
        cols = slice(h * LANES, (h + 1) * LANES)
        o_ref[0, :, cols] = _head_rms(outs[h], nw_ref[:, cols], g_ref[0, :, cols]).astype(o_ref.dtype)


def _hgrn2(proj, logits, norm_w, layer, tb=256, heads=4):
    b, t, _ = proj.shape
    w = LANES * heads
    per = GROUP_W // w

    def col(section):
        return lambda bi, hi, ti: (bi, ti, section * per + hi)

    return pl.pallas_call(
        functools.partial(_hgrn2_kernel, layer=layer, heads=heads, nchunk=tb // CHUNK),
        out_shape=jax.ShapeDtypeStruct((b, t, GROUP_W), BF16),
        grid=(b, per, t // tb),
        in_specs=[pl.BlockSpec((1, tb, w), col(0)), pl.BlockSpec((1, tb, w), col(1)),
                  pl.BlockSpec((1, tb, w), col(2)), pl.BlockSpec((1, tb, w), col(3)),
                  pl.BlockSpec((logits.shape[0], w), lambda bi, hi, ti: (0, hi)),
                  pl.BlockSpec((1, w), lambda bi, hi, ti: (0, hi))],
        out_specs=pl.BlockSpec((1, tb, w), lambda bi, hi, ti: (bi, ti, hi)),
        scratch_shapes=[pltpu.VMEM((heads, LANES, LANES), F32)],
        compiler_params=_cparams(("parallel", "parallel", "arbitrary")),
        name="hgrn2_chunk",
    )(proj, proj, proj, proj, logits, norm_w)


def _gla_kernel(q_ref, k_ref, v_ref, r_ref, gd_ref, w2_ref, b_ref, nw_ref, o_ref, st_ref, *, heads):
    @pl.when(pl.program_id(2) == 0)
    def _():
        st_ref[...] = jnp.zeros_like(st_ref)

    z = _mm_hi(gd_ref[0], w2_ref[...]) + b_ref[...]
    g = -_softplus(-z) * (1.0 / GLA_TAU)
    outs = _gla_block(q_ref[0] * GLA_DK ** -0.5, k_ref[0], g, v_ref[0], st_ref, heads, GLA_DK, GLA_DV)
    for h in range(heads):
        cols = slice(h * GLA_DV, (h + 1) * GLA_DV)
        o_ref[0, :, cols] = _head_rms(outs[h], nw_ref[:, cols], r_ref[0, :, cols]).astype(o_ref.dtype)


def _gla(proj, tail, w2, bias, norm_w, tb=256, heads=2):
    b, t, _ = proj.shape
    wk = GLA_DK * heads
    wv = GLA_DV * heads
    groups = GLA_HEADS // heads
    qk0 = 4 * GROUP_W // wk
    v0 = (4 * GROUP_W + 2 * GLA_HEADS * GLA_DK) // wv
    gd_blk = GROUP_W // LANES
    return pl.pallas_call(
        functools.partial(_gla_kernel, heads=heads),
        out_shape=jax.ShapeDtypeStruct((b, t, GROUP_W), BF16),
        grid=(b, groups, t // tb),
        in_specs=[pl.BlockSpec((1, tb, wk), lambda bi, hi, ti: (bi, ti, qk0 + hi)),
                  pl.BlockSpec((1, tb, wk), lambda bi, hi, ti: (bi, ti, qk0 + groups + hi)),
                  pl.BlockSpec((1, tb, wv), lambda bi, hi, ti: (bi, ti, v0 + hi)),
                  pl.BlockSpec((1, tb, wv), lambda bi, hi, ti: (bi, ti, hi)),
                  pl.BlockSpec((1, tb, LANES), lambda bi, hi, ti: (bi, ti, gd_blk)),
                  pl.BlockSpec((LANES, wk), lambda bi, hi, ti: (0, hi)),
                  pl.BlockSpec((1, wk), lambda bi, hi, ti: (0, hi)),
                  pl.BlockSpec((1, wv), lambda bi, hi, ti: (0, hi))],
        out_specs=pl.BlockSpec((1, tb, wv), lambda bi, hi, ti: (bi, ti, hi)),
        scratch_shapes=[pltpu.VMEM((heads, GLA_DV, GLA_DK), F32)],
        compiler_params=_cparams(("parallel", "parallel", "arbitrary")),
        name="gla_chunk",
    )(proj, proj, proj, tail, tail, w2, bias, norm_w)


def _neumann_inverses(mats):
    n = mats[0].shape[0]
    eye = (_iota2((n, n), 0) == _iota2((n, n), 1)).astype(F32)
    ts = [eye + a for a in mats]
    ps = [_mm(a, a) for a in mats]
    steps = max(1, (CHUNK - 1).bit_length() - 1)
    for _ in range(steps - 1):
        both = [_mm(jnp.concatenate([t, p], axis=0), p) for t, p in zip(ts, ps)]
        ts = [t + b[0:n] for t, b in zip(ts, both)]
        ps = [b[n:2 * n] for b in both]
    return [t + _mm(t, p) for t, p in zip(ts, ps)]


def _neumann_inverse(a):
    return _neumann_inverses([a])[0]


def _shifted(x, carry_ref, shift):
    tb = x.shape[0]
    ext = jnp.concatenate([carry_ref[...], x], axis=0)
    return pltpu.roll(ext, shift, 0)[SUBLANES:SUBLANES + tb]


def _gdn_kernel(q_ref, k_ref, v_ref, z_ref, ab_ref, cq_ref, ck_ref, cv_ref, al_ref, dt_ref, nw_ref,
                o_ref, st_ref, cq_s, ck_s, cv_s, *, nchunk, heads):
    head0 = pl.program_id(1) * heads

    @pl.when(pl.program_id(2) == 0)
    def _():
        st_ref[...] = jnp.zeros_like(st_ref)
        cq_s[...] = jnp.zeros_like(cq_s)
        ck_s[...] = jnp.zeros_like(ck_s)
        cv_s[...] = jnp.zeros_like(cv_s)

    tb = q_ref.shape[1]

    def conv_silu(x_ref, carry, w_ref):
        x = x_ref[0]
        w = w_ref[...]
        y = x * w[3:4]
        for j in (1, 2, 3):
            y = y + _shifted(x, carry, j) * w[3 - j:4 - j]
        carry[...] = x[tb - SUBLANES:tb]
        return _silu(y)

    def l2n(x):
        return x * lax.rsqrt(jnp.sum(x * x, axis=-1, keepdims=True) + L2_EPS)

    q_all = conv_silu(q_ref, cq_s, cq_ref)
    k_all = conv_silu(k_ref, ck_s, ck_ref)
    v_all = conv_silu(v_ref, cv_s, cv_ref)

    c = CHUNK
    incl = _iota2((c, c), 1) <= _iota2((c, c), 0)
    strict = _iota2((c, c), 1) < _iota2((c, c), 0)
    rb = _iota2((tb, tb), 0)
    cb = _iota2((tb, tb), 1)
    chunk_tri = ((cb <= rb) & ((cb // c) == (rb // c))).astype(F32)
    lane = _iota2((tb, LANES), 1)
    ab = ab_ref[0]
    gmat = -jnp.exp(al_ref[...]) * _softplus(ab + dt_ref[...])
    gcum = _mm_sel(chunk_tri, gmat)
    items = [(h, ci) for h in range(heads) for ci in range(nchunk)]
    per_head = []
    for h in range(heads):
        cols = slice(h * GDN_HEAD, (h + 1) * GDN_HEAD)
        q = l2n(q_all[:, cols]) * GDN_HEAD ** -0.5
        k = l2n(k_all[:, cols])
        g_col = jnp.sum(jnp.where(lane == head0 + h, gcum, 0.0), axis=1, keepdims=True)
        beta = _sigmoid(jnp.sum(jnp.where(lane == head0 + h + GDN_HEADS, ab, 0.0), axis=1, keepdims=True))
        eg = jnp.exp(g_col)
        kb = k * beta
        per_head.append(dict(q=q, k=k, g_col=g_col, kb=kb, qg=q * eg,
                             rhs=jnp.concatenate([v_all[:, cols] * beta, kb * eg], axis=1),
                             pick=(_iota2((c, LANES), 1) == head0 + h).astype(F32)))

    lows, aqks, kgts, decays = [], [], [], []
    for h, ci in items:
        ph = per_head[h]
        sl = slice(ci * c, (ci + 1) * c)
        g_c = ph["g_col"][sl]
        g_rows = _mm_sel(ph["pick"], gcum[sl], NT)
        gam = jnp.where(incl, jnp.exp(jnp.where(incl, g_c - g_rows, 0.0)), 0.0)
        lows.append(jnp.where(strict, _mm(ph["kb"][sl], ph["k"][sl], NT) * gam, 0.0))
        aqks.append(jnp.where(incl, _mm(ph["q"][sl], ph["k"][sl], NT) * gam, 0.0).astype(BF16))
        g_last = g_c[c - 1:c]
        kgts.append((ph["k"][sl] * jnp.exp(g_last - g_c)).T.astype(BF16))
        decays.append(jnp.exp(g_last))
    tinvs = _neumann_inverses([-low for low in lows])
    o0s, qps, mts, nts = [], [], [], []
    for i, (h, ci) in enumerate(items):
        sl = slice(ci * c, (ci + 1) * c)
        uw = _mm(tinvs[i], per_head[h]["rhs"][sl])
        ow = _mm(aqks[i], uw)
        kw = _mm(kgts[i], uw)
        o0s.append(ow[:, 0:GDN_HEAD])
        qps.append((per_head[h]["qg"][sl] - ow[:, GDN_HEAD:2 * GDN_HEAD]).astype(BF16))
        nts.append(kw[:, 0:GDN_HEAD])
        mts.append((-kw[:, GDN_HEAD:2 * GDN_HEAD]).astype(BF16))

    states = [st_ref[h] for h in range(heads)]
    outs = [[] for _ in range(heads)]
    for ci in range(nchunk):
        for h in range(heads):
            i = h * nchunk + ci
            sb = states[h].astype(BF16)
            outs[h].append(o0s[i] + _mm(qps[i], sb))
            states[h] = states[h] * decays[i] + _mm(mts[i], sb) + nts[i]
    for h in range(heads):
        cols = slice(h * GDN_HEAD, (h + 1) * GDN_HEAD)
        st_ref[h] = states[h]
        o = jnp.concatenate(outs[h], axis=0)
        o_ref[0, :, cols] = _head_rms(o, nw_ref[:, cols], z_ref[0, :, cols]).astype(o_ref.dtype)


def _gdn(proj, small, conv_w, alog_pad, dt_pad, norm_w, tb=256, heads=8):
    b, t, _ = proj.shape
    w = GDN_HEAD * heads
    groups = GDN_HEADS // heads
    base = 0
    ab_blk = (small.shape[2] - LANES) // LANES

    def col(section):
        return lambda bi, hi, ti: (bi, ti, base + section * groups + hi)

    def cw(section):
        return pl.BlockSpec((4, w), lambda bi, hi, ti: (0, section * groups + hi))

    return pl.pallas_call(
        functools.partial(_gdn_kernel, nchunk=tb // CHUNK, heads=heads),
        out_shape=jax.ShapeDtypeStruct((b, t, GROUP_W), BF16),
        grid=(b, groups, t // tb),
        in_specs=[pl.BlockSpec((1, tb, w), col(0)), pl.BlockSpec((1, tb, w), col(1)),
                  pl.BlockSpec((1, tb, w), col(2)), pl.BlockSpec((1, tb, w), col(3)),
                  pl.BlockSpec((1, tb, LANES), lambda bi, hi, ti: (bi, ti, ab_blk)),
                  cw(0), cw(1), cw(2),
                  pl.BlockSpec((1, LANES), lambda bi, hi, ti: (0, 0)),
                  pl.BlockSpec((1, LANES), lambda bi, hi, ti: (0, 0)),
                  pl.BlockSpec((1, w), lambda bi, hi, ti: (0, hi))],
        out_specs=pl.BlockSpec((1, tb, w), lambda bi, hi, ti: (bi, ti, hi)),
        scratch_shapes=[pltpu.VMEM((heads, GDN_HEAD, GDN_HEAD), F32)]
        + [pltpu.VMEM((SUBLANES, w), F32)] * 3,
        compiler_params=_cparams(("parallel", "parallel", "arbitrary")),
        name="gdn_chunk",
    )(proj, proj, proj, proj, small, conv_w, conv_w, conv_w, alog_pad, dt_pad, norm_w)


def _rwkv_kernel(r_ref, k_ref, v_ref, sm_ref, mur_ref, muk_ref, muv_ref, mus_ref, w0_ref, w2_ref, a0_ref,
                 a2_ref, g2_ref, kk_ref, ka_ref, rk_ref, gw_ref, gb_ref, o_ref,
                 st_ref, cr_s, ck_s, cv_s, cs_s, *, nchunk, lora):
    @pl.when(pl.program_id(2) == 0)
    def _():
        for ref in (st_ref, cr_s, ck_s, cv_s, cs_s):
            ref[...] = jnp.zeros_like(ref)

    tb = r_ref.shape[1]
    half = RWKV_HEAD
    block_diag = (_iota2((LANES, LANES), 0) // half) == (_iota2((LANES, LANES), 1) // half)
    ones_bd = block_diag.astype(BF16)

    def seg_sum(x):
        hi = x.astype(BF16)
        lo = (x - hi.astype(F32)).astype(BF16)
        parts = []
        for j in range(x.shape[1] // LANES):
            cols = slice(j * LANES, (j + 1) * LANES)
            parts.append(lax.dot_general(hi[:, cols], ones_bd, NN, preferred_element_type=F32)
                         + lax.dot_general(lo[:, cols], ones_bd, NN, preferred_element_type=F32))
        return parts[0] if len(parts) == 1 else jnp.concatenate(parts, axis=1)

    def token_shift(x, carry, mu):
        prev = _shifted(x, carry, 1)
        carry[...] = x[tb - SUBLANES:tb]
        return x + (prev - x) * mu

    r = token_shift(r_ref[0], cr_s, mur_ref[...])
    k = token_shift(k_ref[0], ck_s, muk_ref[...])
    v = token_shift(v_ref[0], cv_s, muv_ref[...])
    sm = token_shift(sm_ref[0], cs_s, mus_ref[...])
    wd = sm[:, 0:lora]
    ad = sm[:, lora:2 * lora]
    gd = sm[:, 2 * lora:2 * lora + g2_ref.shape[0]]
    z = w0_ref[...] + _mm_hi(jnp.tanh(wd), w2_ref[...])
    lw = -jnp.exp(-_softplus(-z) - 0.5)
    ag = _sigmoid(a0_ref[...] + _mm_hi(ad, a2_ref[...]))
    gate = _mm(_sigmoid(gd), g2_ref[...])
    kk = k * kk_ref[...]
    kk = kk * lax.rsqrt(seg_sum(kk * kk) + L2_EPS)
    k2 = k * (1.0 + (ag - 1.0) * ka_ref[...])
    aa_all = -kk
    bb_all = kk * ag
    npair = r.shape[1] // LANES

    c = CHUNK
    m0 = _iota2((c, LANES), 1) < half
    row = _iota2((c, LANES), 0)
    sub = _iota2((c, LANES), 1) % c
    strict = sub < row
    incl = sub <= row
    rb = _iota2((tb, tb), 0)
    cb = _iota2((tb, tb), 1)
    chunk_tri = ((cb <= rb) & ((cb // c) == (rb // c))).astype(F32)
    gc_all = _mm_sel(chunk_tri, lw)
    gx_all = gc_all - lw

    def stack(x):
        return jnp.concatenate([jnp.where(m0, x, 0.0), jnp.where(m0, 0.0, x)], axis=0)

    def unstack(x):
        return jnp.where(m0, x[0:c], x[c:2 * c])

    items = [(pi, ci) for pi in range(npair) for ci in range(nchunk)]
    a_bds, ak_xs, wmats, a_sts, v_sws, r_abss, nvs, bes, decays = [], [], [], [], [], [], [], [], []
    for pi, ci in items:
        sl = (slice(ci * c, (ci + 1) * c), slice(pi * LANES, (pi + 1) * LANES))
        rr, kc, vc, aa, bb = r[sl], k2[sl], v[sl], aa_all[sl], bb_all[sl]
        gc, gx = gc_all[sl], gx_all[sl]
        g_mid = gc[c // 2 - 1:c // 2]
        g_last = gc[c - 1:c]
        e_out = jnp.exp(g_mid - gc)
        at = aa * jnp.exp(gx - g_mid)
        rt = rr * jnp.exp(gc - g_mid)
        bt = bb * e_out
        kt = kc * e_out
        lhs0 = jnp.concatenate([jnp.where(m0, at, 0.0), jnp.where(m0, rt, 0.0)], axis=0)
        lhs1 = jnp.concatenate([jnp.where(m0, 0.0, at), jnp.where(m0, 0.0, rt)], axis=0)
        res0 = _mm(lhs0, jnp.concatenate([bt, kt], axis=0), NT)
        res1 = _mm(lhs1, jnp.concatenate([kt, bt], axis=0), NT)
        top0 = jnp.where(strict, res0[0:c], 0.0)
        top1 = jnp.where(strict, res1[0:c], 0.0)
        a_bds.append(jnp.concatenate([jnp.where(m0, top0, 0.0), jnp.where(m0, 0.0, top1)], axis=0))
        ak_xs.append(jnp.concatenate([jnp.where(m0, 0.0, top0), jnp.where(m0, top1, 0.0)], axis=0))
        wmats.append(jnp.concatenate([jnp.where(incl, res0[c:2 * c], 0.0),
                                      jnp.where(incl, res1[c:2 * c], 0.0)], axis=0))
        a_sts.append(stack(aa * jnp.exp(gx)))
        v_sws.append(jnp.concatenate([jnp.where(m0, 0.0, vc), jnp.where(m0, vc, 0.0)], axis=0))
        r_abss.append(rr * jnp.exp(gc))
        e_end = jnp.exp(g_last - gc)
        bes.append(bb * e_end)
        nvs.append((vc, kc * e_end))
        decays.append(jnp.exp(g_last))
    tinvs = _neumann_inverses(a_bds)

    y0s, qps, gbds, ncs = [], [], [], []
    for ci in range(len(items)):
        rhs_v = _mm(ak_xs[ci], v_sws[ci])
        uw = _mm(tinvs[ci], jnp.concatenate([rhs_v, a_sts[ci]], axis=1))
        u_v = uw[:, 0:LANES]
        wh = uw[:, LANES:2 * LANES]
        yw = _mm(wmats[ci], jnp.concatenate([u_v + v_sws[ci], wh], axis=1))
        y0s.append(unstack(yw[:, 0:LANES]))
        qps.append((unstack(yw[:, LANES:2 * LANES]) + r_abss[ci]).astype(BF16))
        gbds.append(jnp.where(block_diag, _mm(unstack(wh), bes[ci], TN), 0.0).astype(BF16))
        vc, ke = nvs[ci]
        ncs.append(jnp.where(block_diag, _mm(jnp.concatenate([unstack(u_v), vc], axis=0),
                                             jnp.concatenate([bes[ci], ke], axis=0), TN), 0.0))

    hts = [st_ref[pi] for pi in range(npair)]
    ys = [[] for _ in range(npair)]
    for ci in range(nchunk):
        for pi in range(npair):
            i = pi * nchunk + ci
            htb = hts[pi].astype(BF16)
            ys[pi].append(y0s[i] + _mm(qps[i], htb, NT))
            hts[pi] = hts[pi] * decays[i] + _mm(htb, gbds[i]) + ncs[i]
    for pi in range(npair):
        st_ref[pi] = hts[pi]

    ycols = [jnp.concatenate(ys[pi], axis=0) for pi in range(npair)]
    y = ycols[0] if npair == 1 else jnp.concatenate(ycols, axis=1)
    mean = seg_sum(y) * (1.0 / half)
    yc = y - mean
    var = seg_sum(yc * yc) * (1.0 / half)
    yn = yc * lax.rsqrt(var + RWKV_GN_EPS) * gw_ref[...] + gb_ref[...]
    bonus = seg_sum(r * k2 * rk_ref[...]) * v
    o_ref[0] = ((yn + bonus) * gate).astype(o_ref.dtype)


def _rwkv(proj, small, mu_main, mu_small, w0, w2p, a0, a2p, g2, kk, ka, rk, gw, gb, lora, tb=256, npair=8):
    b, t, _ = proj.shape
    w = LANES * npair
    groups = GROUP_W // w
    ws = small.shape[2]
    glora = g2.shape[0]

    def col(section):
        return lambda bi, hi, ti: (bi, ti, section * groups + hi)

    def vec(section=0):
        return pl.BlockSpec((1, w), lambda bi, hi, ti: (0, section * groups + hi))

    row_blk = lambda s: pl.BlockSpec((1, tb, w), col(s))
    return pl.pallas_call(
        functools.partial(_rwkv_kernel, nchunk=tb // CHUNK, lora=lora),
        out_shape=jax.ShapeDtypeStruct((b, t, GROUP_W), BF16),
        grid=(b, groups, t // tb),
        in_specs=[row_blk(0), row_blk(1), row_blk(2),
                  pl.BlockSpec((1, tb, ws), lambda bi, hi, ti: (bi, ti, 0)),
                  vec(0), vec(1), vec(2),
                  pl.BlockSpec((1, ws), lambda bi, hi, ti: (0, 0)),
                  vec(), pl.BlockSpec((lora, w), lambda bi, hi, ti: (0, hi)),
                  vec(), pl.BlockSpec((lora, w), lambda bi, hi, ti: (0, hi)),
                  pl.BlockSpec((glora, w), lambda bi, hi, ti: (0, hi)),
                  vec(), vec(), vec(), vec(), vec()],
        out_specs=pl.BlockSpec((1, tb, w), lambda bi, hi, ti: (bi, ti, hi)),
        scratch_shapes=[pltpu.VMEM((npair, LANES, LANES), F32)]
        + [pltpu.VMEM((SUBLANES, w), F32)] * 3
        + [pltpu.VMEM((SUBLANES, ws), F32)],
        compiler_params=_cparams(("parallel", "parallel", "arbitrary")),
        name="rwkv7_chunk",
    )(proj, proj, proj, small, mu_main, mu_main, mu_main, mu_small, w0, w2p, a0, a2p, g2,
      kk, ka, rk, gw, gb)


def _pad_cols(a, width):
    return jnp.pad(a, ((0, 0), (0, width - a.shape[1])))


def _pad_rows(a, height):
    return jnp.pad(a, ((0, height - a.shape[0]), (0, 0)))


def _row(v):
    return v.reshape(1, -1).astype(F32)


def kernel(x, p, hgrn_lb_logits, e_w_in, e_gla_w2, e_gla_b, e_hgrn_norm, e_gla_norm, e_w_out, o_w_in, o_rwkv_mu, o_rwkv_w0, o_rwkv_w2, o_rwkv_a0, o_rwkv_a2, o_rwkv_g2, o_rwkv_kk, o_rwkv_ka, o_rwkv_rk, o_rwkv_gn_w, o_rwkv_gn_b, o_gdn_conv, o_gdn_a_log, o_gdn_dt_bias, o_gdn_norm, o_w_out, ln_mix_w, ln_mix_b, ln_ffn_w, ln_ffn_b, ffn_w_up, ffn_conv, ffn_w_down, ple_w_proj, ple_w_gate):
    bsz, seq, d = x.shape
    m = bsz * seq
    depth = ln_mix_w.shape[0]
    alpha = (2 * depth) ** 0.25
    gw = GROUP_W
    x32 = x.reshape(m, d)
    xb = x32.astype(BF16)
    w_down = ffn_w_down.astype(BF16)
    for layer in range(depth):
        j = layer // 2
        if layer % 2 == 0:
            rank = e_gla_w2.shape[1]
            gd0 = 4 * gw + 2 * GLA_HEADS * GLA_DK + gw
            wt = jnp.swapaxes(e_w_in, 1, 2)
            w_tail = jnp.concatenate([wt[j, gd0 + rank:], _pad_rows(wt[j, gd0:gd0 + rank], LANES)],
                                     axis=0)
            proj = _proj(xb, wt, gd0, layer=j).reshape(bsz, seq, -1)
            tail = _proj(xb, w_tail, w_tail.shape[0], tn=w_tail.shape[0]).reshape(bsz, seq, -1)
            o_a = _hgrn2(proj, hgrn_lb_logits.astype(F32), _row(e_hgrn_norm[j]), layer)
            o_b = _gla(proj, tail, _pad_rows(e_gla_w2[j], LANES), _row(e_gla_b[j]), _row(e_gla_norm[j]))
            w_out = e_w_out[j]
        else:
            wt = jnp.swapaxes(o_w_in, 1, 2)
            lora_w = o_rwkv_w2.shape[1]
            lora_a = o_rwkv_a2.shape[1]
            lora_g = o_rwkv_g2.shape[1]
            lora = LANES * (-(-max(lora_w, lora_a) // LANES))
            c_wd = 3 * gw
            c_ad = c_wd + lora_w
            c_gd = c_ad + lora_a
            c_qkv = c_gd + lora_g
            c_z = c_qkv + 3 * gw
            c_ab = c_z + gw
            w_small = jnp.concatenate([_pad_rows(wt[j, c_wd:c_ad], lora), _pad_rows(wt[j, c_ad:c_gd], lora),
                                       wt[j, c_gd:c_qkv], _pad_rows(wt[j, c_ab:], LANES)], axis=0)
            mu = o_rwkv_mu[j].reshape(1, -1)
            mu_small = jnp.concatenate([_pad_cols(mu[:, c_wd:c_ad], lora), _pad_cols(mu[:, c_ad:c_gd], lora),
                                        mu[:, c_gd:c_qkv], jnp.zeros((1, LANES), F32)], axis=1)
            proj_rkv = _proj(xb, wt, c_wd, layer=j).reshape(bsz, seq, -1)
            proj_gdn = _proj(xb, wt[j, c_qkv:c_ab], c_ab - c_qkv).reshape(bsz, seq, -1)
            small = _proj(xb, w_small, w_small.shape[0], tn=w_small.shape[0]).reshape(bsz, seq, -1)
            o_a = _rwkv(proj_rkv, small, mu[:, :c_wd], mu_small, _row(o_rwkv_w0[j]),
                        _pad_rows(o_rwkv_w2[j], lora), _row(o_rwkv_a0[j]), _pad_rows(o_rwkv_a2[j], lora),
                        o_rwkv_g2[j], _row(o_rwkv_kk[j]), _row(o_rwkv_ka[j]), _row(o_rwkv_rk[j]),
                        _row(o_rwkv_gn_w[j]), _row(o_rwkv_gn_b[j]), lora)
            o_b = _gdn(proj_gdn, small, o_gdn_conv[j].astype(F32),
                       _pad_cols(_row(o_gdn_a_log[j]), LANES), _pad_cols(_row(o_gdn_dt_bias[j]), LANES),
                       _row(o_gdn_norm[j]))
            w_out = o_w_out[j]
        x32, xb = _mix_out(o_a.reshape(m, gw), o_b.reshape(m, gw), w_out.astype(BF16), x32,
                           _row(ln_mix_w[layer]), _row(ln_mix_b[layer]), alpha)
        h = _ffn_up(xb, ffn_w_up, ffn_conv, layer, seq)
        x32, xb = _ffn_down(h, w_down, x32, _row(ln_ffn_w[layer]), _row(ln_ffn_b[layer]), alpha, layer)
        x32, xb = _ple(xb, x32, p.reshape(depth, m, -1), ple_w_gate, ple_w_proj, layer)
    return x32.reshape(bsz, seq, d)
```

```python
import functools

import jax
import jax.numpy as jnp
from jax import lax
from jax.experimental import pallas as pl
from jax.experimental.pallas import tpu as pltpu

F32 = jnp.float32
BF16 = jnp.bfloat16

NN = (((1,), (0,)), ((), ()))
NT = (((1,), (1,)), ((), ()))
TN = (((0,), (0,)), ((), ()))

D_MODEL = 2048
GROUP_W = 1024
CHUNK = 64
LANES = 128
SUBLANES = 8
HGRN_HEADS = 8
GLA_HEADS = 4
GLA_DK = 128
GLA_DV = 256
GLA_TAU = 16.0
RWKV_HEAD = 64
RWKV_GN_EPS = 64e-5
GDN_HEADS = 8
GDN_HEAD = 128
FFN_DIM = 5632
LN_EPS = 1e-5
LN_SUBROWS = 128
RMS_EPS = 1e-6
L2_EPS = 1e-6

VMEM_LIMIT = 56 * 1024 * 1024


def _mm(a, b, dims=NN):
    return lax.dot_general(a.astype(BF16), b.astype(BF16), dims, preferred_element_type=F32)


def _split_bf16(x, parts):
    out = []
    for _ in range(parts - 1):
        h = x.astype(BF16)
        out.append(h)
        x = x - h.astype(F32)
    out.append(x.astype(BF16))
    return out


def _bdot(a, b, dims):
    return lax.dot_general(a, b, dims, preferred_element_type=F32)


def _mm_sel(sel, b, dims=NN):
    s = sel.astype(BF16)
    b1, b2, b3 = _split_bf16(b, 3)
    return (_bdot(s, b3, dims) + _bdot(s, b2, dims)) + _bdot(s, b1, dims)


def _mm_hi(a, b, dims=NN):
    ah, al = _split_bf16(a, 2)
    bh, bl = _split_bf16(b, 2)
    return (_bdot(al, bh, dims) + _bdot(ah, bl, dims)) + _bdot(ah, bh, dims)


def _sigmoid(x):
    return 1.0 / (1.0 + jnp.exp(-x))


def _silu(x):
    return x * _sigmoid(x)


def _softplus(x):
    return jnp.maximum(x, 0.0) + jnp.log(1.0 + jnp.exp(-jnp.abs(x)))


def _iota2(shape, dim):
    return lax.broadcasted_iota(jnp.int32, shape, dim)


def _cparams(sem):
    return pltpu.CompilerParams(dimension_semantics=sem, vmem_limit_bytes=VMEM_LIMIT)


def _proj_kernel(x_ref, w_ref, o_ref, wb_ref):
    @pl.when(pl.program_id(1) == 0)
    def _():
        wb_ref[...] = w_ref[...].T.astype(BF16)

    o_ref[...] = lax.dot_general(x_ref[...], wb_ref[...], NN, preferred_element_type=F32).astype(o_ref.dtype)


def _proj(x, wt, n_cols, out_dtype=F32, layer=None, tm=1024, tn=1024):
    m, k = x.shape
    tn = min(tn, n_cols)
    if layer is None:
        w_spec = pl.BlockSpec((tn, k), lambda j, i: (j, 0))
    else:
        w_spec = pl.BlockSpec((None, tn, k), lambda j, i: (layer, j, 0))
    return pl.pallas_call(
        _proj_kernel,
        out_shape=jax.ShapeDtypeStruct((m, n_cols), out_dtype),
        grid=(n_cols // tn, m // tm),
        in_specs=[pl.BlockSpec((tm, k), lambda j, i: (i, 0)), w_spec],
        out_specs=pl.BlockSpec((tm, tn), lambda j, i: (i, j)),
        scratch_shapes=[pltpu.VMEM((k, tn), BF16)],
        compiler_params=_cparams(("parallel", "arbitrary")),
        name="proj_matmul",
    )(x, wt)


def _layer_norm_rows(y, w, b):
    mu = jnp.mean(y, axis=-1, keepdims=True)
    yc = y - mu
    var = jnp.mean(yc * yc, axis=-1, keepdims=True)
    return yc * lax.rsqrt(var + LN_EPS) * w + b


def _mix_out_kernel(oa_ref, ob_ref, w_ref, x_ref, lw_ref, lb_ref, o32_ref, o16_ref, *, alpha):
    half = oa_ref.shape[1]
    for s in range(oa_ref.shape[0] // LN_SUBROWS):
        rows = slice(s * LN_SUBROWS, (s + 1) * LN_SUBROWS)
        acc = _mm(oa_ref[rows, :], w_ref[0:half, :]) + _mm(ob_ref[rows, :], w_ref[half:2 * half, :])
        y = _layer_norm_rows(alpha * x_ref[rows, :] + acc, lw_ref[...], lb_ref[...])
        o32_ref[rows, :] = y
        o16_ref[rows, :] = y.astype(BF16)


def _mix_out(oa, ob, w, x, lw, lb, alpha, tm=512):
    m, half = oa.shape
    n = w.shape[1]
    row = lambda i: (i, 0)
    fixed = lambda i: (0, 0)
    return pl.pallas_call(
        functools.partial(_mix_out_kernel, alpha=alpha),
        out_shape=(jax.ShapeDtypeStruct((m, n), F32), jax.ShapeDtypeStruct((m, n), BF16)),
        grid=(m // tm,),
        in_specs=[pl.BlockSpec((tm, half), row), pl.BlockSpec((tm, half), row),
                  pl.BlockSpec((2 * half, n), fixed), pl.BlockSpec((tm, n), row),
                  pl.BlockSpec((1, n), fixed), pl.BlockSpec((1, n), fixed)],
        out_specs=(pl.BlockSpec((tm, n), row), pl.BlockSpec((tm, n), row)),
        compiler_params=_cparams(("parallel",)),
        name="mix_out_ln",
    )(oa, ob, w, x, lw, lb)


def _ffn_up_kernel(x_ref, wg_ref, wv_ref, cg_ref, cv_ref, o_ref, wgb_ref, wvb_ref, hg_ref, hv_ref,
                   *, tiles_per_seq):
    i = pl.program_id(1)
    tm = x_ref.shape[0]

    @pl.when(i == 0)
    def _():
        wgb_ref[...] = wg_ref[...].astype(BF16)
        wvb_ref[...] = wv_ref[...].astype(BF16)
        hg_ref[...] = jnp.zeros_like(hg_ref)
        hv_ref[...] = jnp.zeros_like(hv_ref)

    first = (i % tiles_per_seq) == 0

    def conv(w_ref, h_ref, c_ref):
        u = lax.dot_general(x_ref[...], w_ref[...], NN, preferred_element_type=F32)
        halo = jnp.where(first, 0.0, h_ref[...])
        h_ref[...] = u[tm - SUBLANES:tm]
        ext = jnp.concatenate([halo, u], axis=0)
        p1 = pltpu.roll(ext, 1, 0)[SUBLANES:SUBLANES + tm]
        p2 = pltpu.roll(ext, 2, 0)[SUBLANES:SUBLANES + tm]
        c = c_ref[...]
        return u * c[2:3] + p1 * c[1:2] + p2 * c[0:1]

    g = conv(wgb_ref, hg_ref, cg_ref)
    v = conv(wvb_ref, hv_ref, cv_ref)
    o_ref[...] = (_silu(g) * v).astype(o_ref.dtype)


def _ffn_up(xb, w_up, conv_w, layer, seq, tm=1024, tf=512):
    m, k = xb.shape
    f = w_up.shape[2] // 2
    nf = f // tf
    return pl.pallas_call(
        functools.partial(_ffn_up_kernel, tiles_per_seq=seq // tm),
        out_shape=jax.ShapeDtypeStruct((m, f), BF16),
        grid=(nf, m // tm),
        in_specs=[pl.BlockSpec((tm, k), lambda j, i: (i, 0)),
                  pl.BlockSpec((None, k, tf), lambda j, i: (layer, 0, j)),
                  pl.BlockSpec((None, k, tf), lambda j, i: (layer, 0, nf + j)),
                  pl.BlockSpec((None, 3, tf), lambda j, i: (layer, 0, j)),
                  pl.BlockSpec((None, 3, tf), lambda j, i: (layer, 0, nf + j))],
        out_specs=pl.BlockSpec((tm, tf), lambda j, i: (i, j)),
        scratch_shapes=[pltpu.VMEM((k, tf), BF16)] * 2 + [pltpu.VMEM((SUBLANES, tf), F32)] * 2,
        compiler_params=_cparams(("parallel", "arbitrary")),
        name="ffn_up_conv_gate",
    )(xb, w_up, w_up, conv_w, conv_w)


def _ffn_down_kernel(h_ref, w_ref, x_ref, lw_ref, lb_ref, o32_ref, o16_ref, acc_ref, *, alpha, nk):
    k = pl.program_id(1)

    @pl.when(k == 0)
    def _():
        acc_ref[...] = _mm(h_ref[...], w_ref[...])

    @pl.when(k > 0)
    def _():
        acc_ref[...] += _mm(h_ref[...], w_ref[...])

    @pl.when(k == nk - 1)
    def _():
        y = _layer_norm_rows(alpha * x_ref[...] + acc_ref[...], lw_ref[...], lb_ref[...])
        o32_ref[...] = y
        o16_ref[...] = y.astype(BF16)


def _ffn_down(h, w, x, lw, lb, alpha, layer, tm=512, tk=2816):
    m, kdim = h.shape
    n = w.shape[2]
    nk = kdim // tk
    assert nk >= 2
    return pl.pallas_call(
        functools.partial(_ffn_down_kernel, alpha=alpha, nk=nk),
        out_shape=(jax.ShapeDtypeStruct((m, n), F32), jax.ShapeDtypeStruct((m, n), BF16)),
        grid=(m // tm, nk),
        in_specs=[pl.BlockSpec((tm, tk), lambda i, k: (i, k)),
                  pl.BlockSpec((None, tk, n), lambda i, k: (layer, k, 0)),
                  pl.BlockSpec((tm, n), lambda i, k: (i, 0)),
                  pl.BlockSpec((1, n), lambda i, k: (0, 0)),
                  pl.BlockSpec((1, n), lambda i, k: (0, 0))],
        out_specs=(pl.BlockSpec((tm, n), lambda i, k: (i, 0)),
                   pl.BlockSpec((tm, n), lambda i, k: (i, 0))),
        scratch_shapes=[pltpu.VMEM((tm, n), F32)],
        compiler_params=_cparams(("parallel", "arbitrary")),
        name="ffn_down_ln",
    )(h, w, x, lw, lb)


def _ple_kernel(xb_ref, wg_ref, p_ref, wp_ref, x_ref, o32_ref, o16_ref, wgb_ref):
    @pl.when(pl.program_id(1) == 0)
    def _():
        wgb_ref[...] = wg_ref[...].astype(BF16)

    gate = _sigmoid(_mm(xb_ref[...], wgb_ref[...]))
    y = x_ref[...] + gate * _mm(p_ref[...], wp_ref[...])
    o32_ref[...] = y
    o16_ref[...] = y.astype(BF16)


def _ple(xb, x, p, wg, wp, layer, tm=512, tn=1024):
    m, k = xb.shape
    n = wg.shape[2]
    kp = p.shape[2]
    return pl.pallas_call(
        _ple_kernel,
        out_shape=(jax.ShapeDtypeStruct((m, n), F32), jax.ShapeDtypeStruct((m, n), BF16)),
        grid=(n // tn, m // tm),
        in_specs=[pl.BlockSpec((tm, k), lambda j, i: (i, 0)),
                  pl.BlockSpec((None, k, tn), lambda j, i: (layer, 0, j)),
                  pl.BlockSpec((None, tm, kp), lambda j, i: (layer, i, 0)),
                  pl.BlockSpec((None, kp, tn), lambda j, i: (layer, 0, j)),
                  pl.BlockSpec((tm, tn), lambda j, i: (i, j))],
        out_specs=(pl.BlockSpec((tm, tn), lambda j, i: (i, j)),
                   pl.BlockSpec((tm, tn), lambda j, i: (i, j))),
        scratch_shapes=[pltpu.VMEM((k, tn), BF16)],
        compiler_params=_cparams(("parallel", "arbitrary")),
        name="ple_gate",
    )(xb, wg, p, wp, x)


def _chunk_tri(tb):
    rb = _iota2((tb, tb), 0)
    cb = _iota2((tb, tb), 1)
    return ((cb <= rb) & ((cb // CHUNK) == (rb // CHUNK))).astype(F32)


def _gla_block(q, k, g, v, st_ref, heads, dk, dv):
    tb = q.shape[0]
    c = CHUNK
    causal = _iota2((c, c), 1) <= _iota2((c, c), 0)
    gc_all = _mm_sel(_chunk_tri(tb), g)
    nchunk = tb // c
    items = [(h, ci) for h in range(heads) for ci in range(nchunk)]

    def part(x, h, ci, d):
        return x[ci * c:(ci + 1) * c, h * d:(h + 1) * d]

    gcs = [part(gc_all, h, ci, dk) for h, ci in items]
    scores = [_mm(part(q, h, ci, dk) * jnp.exp(gc - gc[c // 2:c // 2 + 1]),
                  part(k, h, ci, dk) * jnp.exp(gc[c // 2:c // 2 + 1] - gc), NT)
              for (h, ci), gc in zip(items, gcs)]
    intra = [_mm(jnp.where(causal, a, 0.0), part(v, h, ci, dv)) for (h, ci), a in zip(items, scores)]
    incs = [_mm(part(v, h, ci, dv), part(k, h, ci, dk) * jnp.exp(gc[c - 1:c] - gc), TN)
            for (h, ci), gc in zip(items, gcs)]
    qgs = [(part(q, h, ci, dk) * jnp.exp(gc)).astype(BF16) for (h, ci), gc in zip(items, gcs)]
    sts = [st_ref[h] for h in range(heads)]
    o_h = [[] for _ in range(heads)]
    for ci in range(nchunk):
        for h in range(heads):
            i = h * nchunk + ci
            o_h[h].append(intra[i] + _mm(qgs[i], sts[h], NT))
            sts[h] = sts[h] * jnp.exp(gcs[i][c - 1:c]) + incs[i]
    for h in range(heads):
        st_ref[h] = sts[h]
    return [jnp.concatenate(o, axis=0) for o in o_h]


def _head_rms(o, gain, gate):
    o = o * lax.rsqrt(jnp.mean(o * o, axis=-1, keepdims=True) + RMS_EPS)
    return o * gain * _silu(gate)


def _hgrn2_kernel(q_ref, f_ref, i_ref, g_ref, lg_ref, nw_ref, o_ref, st_ref, *, layer, heads, nchunk):
    @pl.when(pl.program_id(2) == 0)
    def _():
        st_ref[...] = jnp.zeros_like(st_ref)

    lg = lg_ref[...]
    e = jnp.exp(lg - jnp.max(lg, axis=0, keepdims=True))
    lb = jnp.sum(e[0:layer + 1], axis=0, keepdims=True) / jnp.sum(e, axis=0, keepdims=True)

    f = lb + (1.0 - lb) * _sigmoid(f_ref[0])
    outs = _gla_block(_silu(q_ref[0]), 1.0 - f, jnp.log(f), i_ref[0], st_ref, heads, LANES, LANES)
    for h in range(heads):
        cols = slice(h * LANES, (h + 1) * LANES)
        o_ref[0, :, cols] = _head_rms(outs[h], nw_ref[:, cols], g_ref[0, :, cols]).astype(o_ref.dtype)


def _hgrn2(proj, logits, norm_w, layer, tb=256, heads=4):
    b, t, _ = proj.shape
    w = LANES * heads
    per = GROUP_W // w

    def col(section):
        return lambda bi, hi, ti: (bi, ti, section * per + hi)

    return pl.pallas_call(
        functools.partial(_hgrn2_kernel, layer=layer, heads=heads, nchunk=tb // CHUNK),
        out_shape=jax.ShapeDtypeStruct((b, t, GROUP_W), BF16),
        grid=(b, per, t // tb),
        in_specs=[pl.BlockSpec((1, tb, w), col(0)), pl.BlockSpec((1, tb, w), col(1)),
                  pl.BlockSpec((1, tb, w), col(2)), pl.BlockSpec((1, tb, w), col(3)),
                  pl.BlockSpec((logits.shape[0], w), lambda bi, hi, ti: (0, hi)),
                  pl.BlockSpec((1, w), lambda bi, hi, ti: (0, hi))],
        out_specs=pl.BlockSpec((1, tb, w), lambda bi, hi, ti: (bi, ti, hi)),
        scratch_shapes=[pltpu.VMEM((heads, LANES, LANES), F32)],
        compiler_params=_cparams(("parallel", "parallel", "arbitrary")),
        name="hgrn2_chunk",
    )(proj, proj, proj, proj, logits, norm_w)


def _gla_kernel(q_ref, k_ref, v_ref, r_ref, gd_ref, w2_ref, b_ref, nw_ref, o_ref, st_ref, *, heads):
    @pl.when(pl.program_id(2) == 0)
    def _():
        st_ref[...] = jnp.zeros_like(st_ref)

    z = _mm_hi(gd_ref[0], w2_ref[...]) + b_ref[...]
    g = -_softplus(-z) * (1.0 / GLA_TAU)
    outs = _gla_block(q_ref[0] * GLA_DK ** -0.5, k_ref[0], g, v_ref[0], st_ref, heads, GLA_DK, GLA_DV)
    for h in range(heads):
        cols = slice(h * GLA_DV, (h + 1) * GLA_DV)
        o_ref[0, :, cols] = _head_rms(outs[h], nw_ref[:, cols], r_ref[0, :, cols]).astype(o_ref.dtype)


def _gla(proj, tail, w2, bias, norm_w, tb=256, heads=2):
    b, t, _ = proj.shape
    wk = GLA_DK * heads
    wv = GLA_DV * heads
    groups = GLA_HEADS // heads
    qk0 = 4 * GROUP_W // wk
    v0 = (4 * GROUP_W + 2 * GLA_HEADS * GLA_DK) // wv
    gd_blk = GROUP_W // LANES
    return pl.pallas_call(
        functools.partial(_gla_kernel, heads=heads),
        out_shape=jax.ShapeDtypeStruct((b, t, GROUP_W), BF16),
        grid=(b, groups, t // tb),
        in_specs=[pl.BlockSpec((1, tb, wk), lambda bi, hi, ti: (bi, ti, qk0 + hi)),
                  pl.BlockSpec((1, tb, wk), lambda bi, hi, ti: (bi, ti, qk0 + groups + hi)),
                  pl.BlockSpec((1, tb, wv), lambda bi, hi, ti: (bi, ti, v0 + hi)),
                  pl.BlockSpec((1, tb, wv), lambda bi, hi, ti: (bi, ti, hi)),
                  pl.BlockSpec((1, tb, LANES), lambda bi, hi, ti: (bi, ti, gd_blk)),
                  pl.BlockSpec((LANES, wk), lambda bi, hi, ti: (0, hi)),
                  pl.BlockSpec((1, wk), lambda bi, hi, ti: (0, hi)),
                  pl.BlockSpec((1, wv), lambda bi, hi, ti: (0, hi))],
        out_specs=pl.BlockSpec((1, tb, wv), lambda bi, hi, ti: (bi, ti, hi)),
        scratch_shapes=[pltpu.VMEM((heads, GLA_DV, GLA_DK), F32)],
        compiler_params=_cparams(("parallel", "parallel", "arbitrary")),
        name="gla_chunk",
    )(proj, proj, proj, tail, tail, w2, bias, norm_w)


def _neumann_inverses(mats):
    n = mats[0].shape[0]
    eye = (_iota2((n, n), 0) == _iota2((n, n), 1)).astype(F32)
    ts = [eye + a for a in mats]
    ps = [_mm(a, a) for a in mats]
    steps = max(1, (CHUNK - 1).bit_length() - 1)
    for _ in range(steps - 1):
        both = [_mm(jnp.concatenate([t, p], axis=0), p) for t, p in zip(ts, ps)]
        ts = [t + b[0:n] for t, b in zip(ts, both)]
        ps = [b[n:2 * n] for b in both]
    return [t + _mm(t, p) for t, p in zip(ts, ps)]


def _neumann_inverse(a):
    return _neumann_inverses([a])[0]


def _shifted(x, carry_ref, shift):
    tb = x.shape[0]
    ext = jnp.concatenate([carry_ref[...], x], axis=0)
    return pltpu.roll(ext, shift, 0)[SUBLANES:SUBLANES + tb]


def _gdn_kernel(q_ref, k_ref, v_ref, z_ref, ab_ref, cq_ref, ck_ref, cv_ref, al_ref, dt_ref, nw_ref,
                o_ref, st_ref, cq_s, ck_s, cv_s, *, nchunk, heads):
    head0 = pl.program_id(1) * heads

    @pl.when(pl.program_id(2) == 0)
    def _():
        st_ref[...] = jnp.zeros_like(st_ref)
        cq_s[...] = jnp.zeros_like(cq_s)
        ck_s[...] = jnp.zeros_like(ck_s)
        cv_s[...] = jnp.zeros_like(cv_s)

    tb = q_ref.shape[1]

    def conv_silu(x_ref, carry, w_ref):
        x = x_ref[0]
        w = w_ref[...]
        y = x * w[3:4]
        for j in (1, 2, 3):
            y = y + _shifted(x, carry, j) * w[3 - j:4 - j]
        carry[...] = x[tb - SUBLANES:tb]
        return _silu(y)

    def l2n(x):
        return x * lax.rsqrt(jnp.sum(x * x, axis=-1, keepdims=True) + L2_EPS)

    q_all = conv_silu(q_ref, cq_s, cq_ref)
    k_all = conv_silu(k_ref, ck_s, ck_ref)
    v_all = conv_silu(v_ref, cv_s, cv_ref)

    c = CHUNK
    incl = _iota2((c, c), 1) <= _iota2((c, c), 0)
    strict = _iota2((c, c), 1) < _iota2((c, c), 0)
    rb = _iota2((tb, tb), 0)
    cb = _iota2((tb, tb), 1)
    chunk_tri = ((cb <= rb) & ((cb // c) == (rb // c))).astype(F32)
    lane = _iota2((tb, LANES), 1)
    ab = ab_ref[0]
    gmat = -jnp.exp(al_ref[...]) * _softplus(ab + dt_ref[...])
    gcum = _mm_sel(chunk_tri, gmat)
    items = [(h, ci) for h in range(heads) for ci in range(nchunk)]
    per_head = []
    for h in range(heads):
        cols = slice(h * GDN_HEAD, (h + 1) * GDN_HEAD)
        q = l2n(q_all[:, cols]) * GDN_HEAD ** -0.5
        k = l2n(k_all[:, cols])
        g_col = jnp.sum(jnp.where(lane == head0 + h, gcum, 0.0), axis=1, keepdims=True)
        beta = _sigmoid(jnp.sum(jnp.where(lane == head0 + h + GDN_HEADS, ab, 0.0), axis=1, keepdims=True))
        eg = jnp.exp(g_col)
        kb = k * beta
        per_head.append(dict(q=q, k=k, g_col=g_col, kb=kb, qg=q * eg,
                             rhs=jnp.concatenate([v_all[:, cols] * beta, kb * eg], axis=1),
                             pick=(_iota2((c, LANES), 1) == head0 + h).astype(F32)))

    def rows(name, h, ci):
        return per_head[h][name][ci * c:(ci + 1) * c]

    g_rows = [_mm_sel(per_head[h]["pick"], gcum[ci * c:(ci + 1) * c], NT) for h, ci in items]
    kks = [_mm(rows("kb", h, ci), rows("k", h, ci), NT) for h, ci in items]
    qks = [_mm(rows("q", h, ci), rows("k", h, ci), NT) for h, ci in items]
    gams = [jnp.where(incl, jnp.exp(jnp.where(incl, rows("g_col", h, ci) - gr, 0.0)), 0.0)
            for (h, ci), gr in zip(items, g_rows)]
    lows = [jnp.where(strict, kk * gam, 0.0) for kk, gam in zip(kks, gams)]
    aqks = [jnp.where(incl, qk * gam, 0.0).astype(BF16) for qk, gam in zip(qks, gams)]
    g_lasts = [rows("g_col", h, ci)[c - 1:c] for h, ci in items]
    kgts = [(rows("k", h, ci) * jnp.exp(gl - rows("g_col", h, ci))).T.astype(BF16)
            for (h, ci), gl in zip(items, g_lasts)]
    decays = [jnp.exp(gl) for gl in g_lasts]
    tinvs = _neumann_inverses([-low for low in lows])
    uws = [_mm(tinvs[i], per_head[h]["rhs"][ci * c:(ci + 1) * c]) for i, (h, ci) in enumerate(items)]
    ows = [_mm(aqks[i], uws[i]) for i in range(len(items))]
    kws = [_mm(kgts[i], uws[i]) for i in range(len(items))]
    o0s = [ow[:, 0:GDN_HEAD] for ow in ows]
    qps = [(per_head[h]["qg"][ci * c:(ci + 1) * c] - ows[i][:, GDN_HEAD:2 * GDN_HEAD]).astype(BF16)
           for i, (h, ci) in enumerate(items)]
    nts = [kw[:, 0:GDN_HEAD] for kw in kws]
    mts = [(-kw[:, GDN_HEAD:2 * GDN_HEAD]).astype(BF16) for kw in kws]

    states = [st_ref[h] for h in range(heads)]
    outs = [[] for _ in range(heads)]
    for ci in range(nchunk):
        for h in range(heads):
            i = h * nchunk + ci
            sb = states[h].astype(BF16)
            outs[h].append(o0s[i] + _mm(qps[i], sb))
            states[h] = states[h] * decays[i] + _mm(mts[i], sb) + nts[i]
    for h in range(heads):
        cols = slice(h * GDN_HEAD, (h + 1) * GDN_HEAD)
        st_ref[h] = states[h]
        o = jnp.concatenate(outs[h], axis=0)
        o_ref[0, :, cols] = _head_rms(o, nw_ref[:, cols], z_ref[0, :, cols]).astype(o_ref.dtype)


def _gdn(proj, small, conv_w, alog_pad, dt_pad, norm_w, tb=256, heads=8):
    b, t, _ = proj.shape
    w = GDN_HEAD * heads
    groups = GDN_HEADS // heads
    base = 0
    ab_blk = (small.shape[2] - LANES) // LANES

    def col(section):
        return lambda bi, hi, ti: (bi, ti, base + section * groups + hi)

    def cw(section):
        return pl.BlockSpec((4, w), lambda bi, hi, ti: (0, section * groups + hi))

    return pl.pallas_call(
        functools.partial(_gdn_kernel, nchunk=tb // CHUNK, heads=heads),
        out_shape=jax.ShapeDtypeStruct((b, t, GROUP_W), BF16),
        grid=(b, groups, t // tb),
        in_specs=[pl.BlockSpec((1, tb, w), col(0)), pl.BlockSpec((1, tb, w), col(1)),
                  pl.BlockSpec((1, tb, w), col(2)), pl.BlockSpec((1, tb, w), col(3)),
                  pl.BlockSpec((1, tb, LANES), lambda bi, hi, ti: (bi, ti, ab_blk)),
                  cw(0), cw(1), cw(2),
                  pl.BlockSpec((1, LANES), lambda bi, hi, ti: (0, 0)),
                  pl.BlockSpec((1, LANES), lambda bi, hi, ti: (0, 0)),
                  pl.BlockSpec((1, w), lambda bi, hi, ti: (0, hi))],
        out_specs=pl.BlockSpec((1, tb, w), lambda bi, hi, ti: (bi, ti, hi)),
        scratch_shapes=[pltpu.VMEM((heads, GDN_HEAD, GDN_HEAD), F32)]
        + [pltpu.VMEM((SUBLANES, w), F32)] * 3,
        compiler_params=_cparams(("parallel", "parallel", "arbitrary")),
        name="gdn_chunk",
    )(proj, proj, proj, proj, small, conv_w, conv_w, conv_w, alog_pad, dt_pad, norm_w)


def _rwkv_kernel(r_ref, k_ref, v_ref, sm_ref, mur_ref, muk_ref, muv_ref, mus_ref, w0_ref, w2_ref, a0_ref,
                 a2_ref, g2_ref, kk_ref, ka_ref, rk_ref, gw_ref, gb_ref, o_ref,
                 st_ref, cr_s, ck_s, cv_s, cs_s, *, nchunk, lora):
    @pl.when(pl.program_id(2) == 0)
    def _():
        for ref in (st_ref, cr_s, ck_s, cv_s, cs_s):
            ref[...] = jnp.zeros_like(ref)

    tb = r_ref.shape[1]
    half = RWKV_HEAD
    block_diag = (_iota2((LANES, LANES), 0) // half) == (_iota2((LANES, LANES), 1) // half)
    ones_bd = block_diag.astype(BF16)

    def seg_sum(x):
        hi = x.astype(BF16)
        lo = (x - hi.astype(F32)).astype(BF16)
        parts = []
        for j in range(x.shape[1] // LANES):
            cols = slice(j * LANES, (j + 1) * LANES)
            parts.append(lax.dot_general(hi[:, cols], ones_bd, NN, preferred_element_type=F32)
                         + lax.dot_general(lo[:, cols], ones_bd, NN, preferred_element_type=F32))
        return parts[0] if len(parts) == 1 else jnp.concatenate(parts, axis=1)

    def token_shift(x, carry, mu):
        prev = _shifted(x, carry, 1)
        carry[...] = x[tb - SUBLANES:tb]
        return x + (prev - x) * mu

    r = token_shift(r_ref[0], cr_s, mur_ref[...])
    k = token_shift(k_ref[0], ck_s, muk_ref[...])
    v = token_shift(v_ref[0], cv_s, muv_ref[...])
    sm = token_shift(sm_ref[0], cs_s, mus_ref[...])
    wd = sm[:, 0:lora]
    ad = sm[:, lora:2 * lora]
    gd = sm[:, 2 * lora:2 * lora + g2_ref.shape[0]]
    z = w0_ref[...] + _mm_hi(jnp.tanh(wd), w2_ref[...])
    lw = -jnp.exp(-_softplus(-z) - 0.5)
    ag = _sigmoid(a0_ref[...] + _mm_hi(ad, a2_ref[...]))
    gate = _mm(_sigmoid(gd), g2_ref[...])
    kk = k * kk_ref[...]
    kk = kk * lax.rsqrt(seg_sum(kk * kk) + L2_EPS)
    k2 = k * (1.0 + (ag - 1.0) * ka_ref[...])
    aa_all = -kk
    bb_all = kk * ag
    npair = r.shape[1] // LANES

    c = CHUNK
    m0 = _iota2((c, LANES), 1) < half
    row = _iota2((c, LANES), 0)
    sub = _iota2((c, LANES), 1) % c
    strict = sub < row
    incl = sub <= row
    rb = _iota2((tb, tb), 0)
    cb = _iota2((tb, tb), 1)
    chunk_tri = ((cb <= rb) & ((cb // c) == (rb // c))).astype(F32)
    gc_all = _mm_sel(chunk_tri, lw)
    gx_all = gc_all - lw

    def stack(x):
        return jnp.concatenate([jnp.where(m0, x, 0.0), jnp.where(m0, 0.0, x)], axis=0)

    def unstack(x):
        return jnp.where(m0, x[0:c], x[c:2 * c])

    items = [(pi, ci) for pi in range(npair) for ci in range(nchunk)]
    a_bds, ak_xs, wmats, a_sts, v_sws, r_abss, nvs, bes, decays = [], [], [], [], [], [], [], [], []
    for pi, ci in items:
        sl = (slice(ci * c, (ci + 1) * c), slice(pi * LANES, (pi + 1) * LANES))
        rr, kc, vc, aa, bb = r[sl], k2[sl], v[sl], aa_all[sl], bb_all[sl]
        gc, gx = gc_all[sl], gx_all[sl]
        g_mid = gc[c // 2 - 1:c // 2]
        g_last = gc[c - 1:c]
        e_out = jnp.exp(g_mid - gc)
        at = aa * jnp.exp(gx - g_mid)
        rt = rr * jnp.exp(gc - g_mid)
        bt = bb * e_out
        kt = kc * e_out
        lhs0 = jnp.concatenate([jnp.where(m0, at, 0.0), jnp.where(m0, rt, 0.0)], axis=0)
        lhs1 = jnp.concatenate([jnp.where(m0, 0.0, at), jnp.where(m0, 0.0, rt)], axis=0)
        res0 = _mm(lhs0, jnp.concatenate([bt, kt], axis=0), NT)
        res1 = _mm(lhs1, jnp.concatenate([kt, bt], axis=0), NT)
        top0 = jnp.where(strict, res0[0:c], 0.0)
        top1 = jnp.where(strict, res1[0:c], 0.0)
        a_bds.append(jnp.concatenate([jnp.where(m0, top0, 0.0), jnp.where(m0, 0.0, top1)], axis=0))
        ak_xs.append(jnp.concatenate([jnp.where(m0, 0.0, top0), jnp.where(m0, top1, 0.0)], axis=0))
        wmats.append(jnp.concatenate([jnp.where(incl, res0[c:2 * c], 0.0),
                                      jnp.where(incl, res1[c:2 * c], 0.0)], axis=0))
        a_sts.append(stack(aa * jnp.exp(gx)))
        v_sws.append(jnp.concatenate([jnp.where(m0, 0.0, vc), jnp.where(m0, vc, 0.0)], axis=0))
        r_abss.append(rr * jnp.exp(gc))
        e_end = jnp.exp(g_last - gc)
        bes.append(bb * e_end)
        nvs.append((vc, kc * e_end))
        decays.append(jnp.exp(g_last))
    tinvs = _neumann_inverses(a_bds)

    n_items = len(items)
    rhs_vs = [_mm(ak_xs[i], v_sws[i]) for i in range(n_items)]
    uws = [_mm(tinvs[i], jnp.concatenate([rhs_vs[i], a_sts[i]], axis=1)) for i in range(n_items)]
    yws = [_mm(wmats[i], jnp.concatenate([uws[i][:, 0:LANES] + v_sws[i], uws[i][:, LANES:2 * LANES]], axis=1))
           for i in range(n_items)]
    y0s = [unstack(yws[i][:, 0:LANES]) for i in range(n_items)]
    qps = [(unstack(yws[i][:, LANES:2 * LANES]) + r_abss[i]).astype(BF16) for i in range(n_items)]
    gbds = [jnp.where(block_diag, _mm(unstack(uws[i][:, LANES:2 * LANES]), bes[i], TN), 0.0).astype(BF16)
            for i in range(n_items)]
    ncs = [jnp.where(block_diag, _mm(jnp.concatenate([unstack(uws[i][:, 0:LANES]), nvs[i][0]], axis=0),
                                     jnp.concatenate([bes[i], nvs[i][1]], axis=0), TN), 0.0)
           for i in range(n_items)]

    hts = [st_ref[pi] for pi in range(npair)]
    ys = [[] for _ in range(npair)]
    for ci in range(nchunk):
        for pi in range(npair):
            i = pi * nchunk + ci
            htb = hts[pi].astype(BF16)
            ys[pi].append(y0s[i] + _mm(qps[i], htb, NT))
            hts[pi] = hts[pi] * decays[i] + _mm(htb, gbds[i]) + ncs[i]
    for pi in range(npair):
        st_ref[pi] = hts[pi]

    ycols = [jnp.concatenate(ys[pi], axis=0) for pi in range(npair)]
    y = ycols[0] if npair == 1 else jnp.concatenate(ycols, axis=1)
    mean = seg_sum(y) * (1.0 / half)
    yc = y - mean
    var = seg_sum(yc * yc) * (1.0 / half)
    yn = yc * lax.rsqrt(var + RWKV_GN_EPS) * gw_ref[...] + gb_ref[...]
    bonus = seg_sum(r * k2 * rk_ref[...]) * v
    o_ref[0] = ((yn + bonus) * gate).astype(o_ref.dtype)


def _rwkv(proj, small, mu_main, mu_small, w0, w2p, a0, a2p, g2, kk, ka, rk, gw, gb, lora, tb=256, npair=8):
    b, t, _ = proj.shape
    w = LANES * npair
    groups = GROUP_W // w
    ws = small.shape[2]
    glora = g2.shape[0]

    def col(section):
        return lambda bi, hi, ti: (bi, ti, section * groups + hi)

    def vec(section=0):
        return pl.BlockSpec((1, w), lambda bi, hi, ti: (0, section * groups + hi))

    row_blk = lambda s: pl.BlockSpec((1, tb, w), col(s))
    return pl.pallas_call(
        functools.partial(_rwkv_kernel, nchunk=tb // CHUNK, lora=lora),
        out_shape=jax.ShapeDtypeStruct((b, t, GROUP_W), BF16),
        grid=(b, groups, t // tb),
        in_specs=[row_blk(0), row_blk(1), row_blk(2),
                  pl.BlockSpec((1, tb, ws), lambda bi, hi, ti: (bi, ti, 0)),
                  vec(0), vec(1), vec(2),
                  pl.BlockSpec((1, ws), lambda bi, hi, ti: (0, 0)),
                  vec(), pl.BlockSpec((lora, w), lambda bi, hi, ti: (0, hi)),
                  vec(), pl.BlockSpec((lora, w), lambda bi, hi, ti: (0, hi)),
                  pl.BlockSpec((glora, w), lambda bi, hi, ti: (0, hi)),
                  vec(), vec(), vec(), vec(), vec()],
        out_specs=pl.BlockSpec((1, tb, w), lambda bi, hi, ti: (bi, ti, hi)),
        scratch_shapes=[pltpu.VMEM((npair, LANES, LANES), F32)]
        + [pltpu.VMEM((SUBLANES, w), F32)] * 3
        + [pltpu.VMEM((SUBLANES, ws), F32)],
        compiler_params=_cparams(("parallel", "parallel", "arbitrary")),
        name="rwkv7_chunk",
    )(proj, proj, proj, small, mu_main, mu_main, mu_main, mu_small, w0, w2p, a0, a2p, g2,
      kk, ka, rk, gw, gb)


def _pad_cols(a, width):
    return jnp.pad(a, ((0, 0), (0, width - a.shape[1])))


def _pad_rows(a, height):
    return jnp.pad(a, ((0, height - a.shape[0]), (0, 0)))


def _row(v):
    return v.reshape(1, -1).astype(F32)


def kernel(x, p, hgrn_lb_logits, e_w_in, e_gla_w2, e_gla_b, e_hgrn_norm, e_gla_norm, e_w_out, o_w_in, o_rwkv_mu, o_rwkv_w0, o_rwkv_w2, o_rwkv_a0, o_rwkv_a2, o_rwkv_g2, o_rwkv_kk, o_rwkv_ka, o_rwkv_rk, o_rwkv_gn_w, o_rwkv_gn_b, o_gdn_conv, o_gdn_a_log, o_gdn_dt_bias, o_gdn_norm, o_w_out, ln_mix_w, ln_mix_b, ln_ffn_w, ln_ffn_b, ffn_w_up, ffn_conv, ffn_w_down, ple_w_proj, ple_w_gate):
    bsz, seq, d = x.shape
    m = bsz * seq
    depth = ln_mix_w.shape[0]
    alpha = (2 * depth) ** 0.25
    gw = GROUP_W
    x32 = x.reshape(m, d)
    xb = x32.astype(BF16)
    w_down = ffn_w_down.astype(BF16)
    for layer in range(depth):
        j = layer // 2
        if layer % 2 == 0:
            rank = e_gla_w2.shape[1]
            gd0 = 4 * gw + 2 * GLA_HEADS * GLA_DK + gw
            wt = jnp.swapaxes(e_w_in, 1, 2)
            w_tail = jnp.concatenate([wt[j, gd0 + rank:], _pad_rows(wt[j, gd0:gd0 + rank], LANES)],
                                     axis=0)
            proj = _proj(xb, wt, gd0, layer=j).reshape(bsz, seq, -1)
            tail = _proj(xb, w_tail, w_tail.shape[0], tn=w_tail.shape[0]).reshape(bsz, seq, -1)
            o_a = _hgrn2(proj, hgrn_lb_logits.astype(F32), _row(e_hgrn_norm[j]), layer)
            o_b = _gla(proj, tail, _pad_rows(e_gla_w2[j], LANES), _row(e_gla_b[j]), _row(e_gla_norm[j]))
            w_out = e_w_out[j]
        else:
            wt = jnp.swapaxes(o_w_in, 1, 2)
            lora_w = o_rwkv_w2.shape[1]
            lora_a = o_rwkv_a2.shape[1]
            lora_g = o_rwkv_g2.shape[1]
            lora = LANES * (-(-max(lora_w, lora_a) // LANES))
            c_wd = 3 * gw
            c_ad = c_wd + lora_w
            c_gd = c_ad + lora_a
            c_qkv = c_gd + lora_g
            c_z = c_qkv + 3 * gw
            c_ab = c_z + gw
            w_small = jnp.concatenate([_pad_rows(wt[j, c_wd:c_ad], lora), _pad_rows(wt[j, c_ad:c_gd], lora),
                                       wt[j, c_gd:c_qkv], _pad_rows(wt[j, c_ab:], LANES)], axis=0)
            mu = o_rwkv_mu[j].reshape(1, -1)
            mu_small = jnp.concatenate([_pad_cols(mu[:, c_wd:c_ad], lora), _pad_cols(mu[:, c_ad:c_gd], lora),
                                        mu[:, c_gd:c_qkv], jnp.zeros((1, LANES), F32)], axis=1)
            proj_rkv = _proj(xb, wt, c_wd, layer=j).reshape(bsz, seq, -1)
            proj_gdn = _proj(xb, wt[j, c_qkv:c_ab], c_ab - c_qkv).reshape(bsz, seq, -1)
            small = _proj(xb, w_small, w_small.shape[0], tn=w_small.shape[0]).reshape(bsz, seq, -1)
            o_a = _rwkv(proj_rkv, small, mu[:, :c_wd], mu_small, _row(o_rwkv_w0[j]),
                        _pad_rows(o_rwkv_w2[j], lora), _row(o_rwkv_a0[j]), _pad_rows(o_rwkv_a2[j], lora),
                        o_rwkv_g2[j], _row(o_rwkv_kk[j]), _row(o_rwkv_ka[j]), _row(o_rwkv_rk[j]),
                        _row(o_rwkv_gn_w[j]), _row(o_rwkv_gn_b[j]), lora)
            o_b = _gdn(proj_gdn, small, o_gdn_conv[j].astype(F32),
                       _pad_cols(_row(o_gdn_a_log[j]), LANES), _pad_cols(_row(o_gdn_dt_bias[j]), LANES),
                       _row(o_gdn_norm[j]))
            w_out = o_w_out[j]
        x32, xb = _mix_out(o_a.reshape(m, gw), o_b.reshape(m, gw), w_out.astype(BF16), x32,
                           _row(ln_mix_w[layer]), _row(ln_mix_b[layer]), alpha)
        h = _ffn_up(xb, ffn_w_up, ffn_conv, layer, seq)
        x32, xb = _ffn_down(h, w_down, x32, _row(ln_ffn_w[layer]), _row(ln_ffn_b[layer]), alpha, layer)
        x32, xb = _ple(xb, x32, p.reshape(depth, m, -1), ple_w_gate, ple_w_proj, layer)
    return x32.reshape(bsz, seq, d)
```

```python
import functools

import jax
import jax.numpy as jnp
from jax import lax
from jax.experimental import pallas as pl
from jax.experimental.pallas import tpu as pltpu

F32 = jnp.float32
BF16 = jnp.bfloat16

NN = (((1,), (0,)), ((), ()))
NT = (((1,), (1,)), ((), ()))
TN = (((0,), (0,)), ((), ()))

D_MODEL = 2048
GROUP_W = 1024
CHUNK = 64
LANES = 128
SUBLANES = 8
HGRN_HEADS = 8
GLA_HEADS = 4
GLA_DK = 128
GLA_DV = 256
GLA_TAU = 16.0
RWKV_HEAD = 64
RWKV_GN_EPS = 64e-5
GDN_HEADS = 8
GDN_HEAD = 128
FFN_DIM = 5632
LN_EPS = 1e-5
LN_SUBROWS = 128
RMS_EPS = 1e-6
L2_EPS = 1e-6

VMEM_LIMIT = 56 * 1024 * 1024


def _mm(a, b, dims=NN):
    return lax.dot_general(a.astype(BF16), b.astype(BF16), dims, preferred_element_type=F32)


def _split_bf16(x, parts):
    out = []
    for _ in range(parts - 1):
        h = x.astype(BF16)
        out.append(h)
        x = x - h.astype(F32)
    out.append(x.astype(BF16))
    return out


def _bdot(a, b, dims):
    return lax.dot_general(a, b, dims, preferred_element_type=F32)


def _mm_sel(sel, b, dims=NN):
    s = sel.astype(BF16)
    b1, b2, b3 = _split_bf16(b, 3)
    return (_bdot(s, b3, dims) + _bdot(s, b2, dims)) + _bdot(s, b1, dims)


def _mm_hi(a, b, dims=NN):
    ah, al = _split_bf16(a, 2)
    bh, bl = _split_bf16(b, 2)
    return (_bdot(al, bh, dims) + _bdot(ah, bl, dims)) + _bdot(ah, bh, dims)


def _sigmoid(x):
    return 1.0 / (1.0 + jnp.exp(-x))


def _silu(x):
    return x * _sigmoid(x)


def _softplus(x):
    return jnp.maximum(x, 0.0) + jnp.log(1.0 + jnp.exp(-jnp.abs(x)))


def _iota2(shape, dim):
    return lax.broadcasted_iota(jnp.int32, shape, dim)


def _cparams(sem):
    return pltpu.CompilerParams(dimension_semantics=sem, vmem_limit_bytes=VMEM_LIMIT)


def _proj_kernel(x_ref, w_ref, o_ref, wb_ref):
    @pl.when(pl.program_id(1) == 0)
    def _():
        wb_ref[...] = w_ref[...].T.astype(BF16)

    o_ref[...] = lax.dot_general(x_ref[...], wb_ref[...], NN, preferred_element_type=F32).astype(o_ref.dtype)


def _proj(x, wt, n_cols, out_dtype=F32, layer=None, tm=1024, tn=1024):
    m, k = x.shape
    tn = min(tn, n_cols)
    if layer is None:
        w_spec = pl.BlockSpec((tn, k), lambda j, i: (j, 0))
    else:
        w_spec = pl.BlockSpec((None, tn, k), lambda j, i: (layer, j, 0))
    return pl.pallas_call(
        _proj_kernel,
        out_shape=jax.ShapeDtypeStruct((m, n_cols), out_dtype),
        grid=(n_cols // tn, m // tm),
        in_specs=[pl.BlockSpec((tm, k), lambda j, i: (i, 0)), w_spec],
        out_specs=pl.BlockSpec((tm, tn), lambda j, i: (i, j)),
        scratch_shapes=[pltpu.VMEM((k, tn), BF16)],
        compiler_params=_cparams(("parallel", "arbitrary")),
        name="proj_matmul",
    )(x, wt)


def _layer_norm_rows(y, w, b):
    mu = jnp.mean(y, axis=-1, keepdims=True)
    yc = y - mu
    var = jnp.mean(yc * yc, axis=-1, keepdims=True)
    return yc * lax.rsqrt(var + LN_EPS) * w + b


def _mix_out_kernel(oa_ref, ob_ref, w_ref, x_ref, lw_ref, lb_ref, o32_ref, o16_ref, *, alpha):
    half = oa_ref.shape[1]
    for s in range(oa_ref.shape[0] // LN_SUBROWS):
        rows = slice(s * LN_SUBROWS, (s + 1) * LN_SUBROWS)
        acc = _mm(oa_ref[rows, :], w_ref[0:half, :]) + _mm(ob_ref[rows, :], w_ref[half:2 * half, :])
        y = _layer_norm_rows(alpha * x_ref[rows, :] + acc, lw_ref[...], lb_ref[...])
        o32_ref[rows, :] = y
        o16_ref[rows, :] = y.astype(BF16)


def _mix_out(oa, ob, w, x, lw, lb, alpha, tm=512):
    m, half = oa.shape
    n = w.shape[1]
    row = lambda i: (i, 0)
    fixed = lambda i: (0, 0)
    return pl.pallas_call(
        functools.partial(_mix_out_kernel, alpha=alpha),
        out_shape=(jax.ShapeDtypeStruct((m, n), F32), jax.ShapeDtypeStruct((m, n), BF16)),
        grid=(m // tm,),
        in_specs=[pl.BlockSpec((tm, half), row), pl.BlockSpec((tm, half), row),
                  pl.BlockSpec((2 * half, n), fixed), pl.BlockSpec((tm, n), row),
                  pl.BlockSpec((1, n), fixed), pl.BlockSpec((1, n), fixed)],
        out_specs=(pl.BlockSpec((tm, n), row), pl.BlockSpec((tm, n), row)),
        compiler_params=_cparams(("parallel",)),
        name="mix_out_ln",
    )(oa, ob, w, x, lw, lb)


def _ffn_up_kernel(x_ref, wg_ref, wv_ref, cg_ref, cv_ref, o_ref, wgb_ref, wvb_ref, hg_ref, hv_ref,
                   *, tiles_per_seq):
    i = pl.program_id(1)
    tm = x_ref.shape[0]

    @pl.when(i == 0)
    def _():
        wgb_ref[...] = wg_ref[...].astype(BF16)
        wvb_ref[...] = wv_ref[...].astype(BF16)
        hg_ref[...] = jnp.zeros_like(hg_ref)
        hv_ref[...] = jnp.zeros_like(hv_ref)

    first = (i % tiles_per_seq) == 0

    def conv(w_ref, h_ref, c_ref):
        u = lax.dot_general(x_ref[...], w_ref[...], NN, preferred_element_type=F32)
        halo = jnp.where(first, 0.0, h_ref[...])
        h_ref[...] = u[tm - SUBLANES:tm]
        ext = jnp.concatenate([halo, u], axis=0)
        p1 = pltpu.roll(ext, 1, 0)[SUBLANES:SUBLANES + tm]
        p2 = pltpu.roll(ext, 2, 0)[SUBLANES:SUBLANES + tm]
        c = c_ref[...]
        return u * c[2:3] + p1 * c[1:2] + p2 * c[0:1]

    g = conv(wgb_ref, hg_ref, cg_ref)
    v = conv(wvb_ref, hv_ref, cv_ref)
    o_ref[...] = (_silu(g) * v).astype(o_ref.dtype)


def _ffn_up(xb, w_up, conv_w, layer, seq, tm=1024, tf=512):
    m, k = xb.shape
    f = w_up.shape[2] // 2
    nf = f // tf
    return pl.pallas_call(
        functools.partial(_ffn_up_kernel, tiles_per_seq=seq // tm),
        out_shape=jax.ShapeDtypeStruct((m, f), BF16),
        grid=(nf, m // tm),
        in_specs=[pl.BlockSpec((tm, k), lambda j, i: (i, 0)),
                  pl.BlockSpec((None, k, tf), lambda j, i: (layer, 0, j)),
                  pl.BlockSpec((None, k, tf), lambda j, i: (layer, 0, nf + j)),
                  pl.BlockSpec((None, 3, tf), lambda j, i: (layer, 0, j)),
                  pl.BlockSpec((None, 3, tf), lambda j, i: (layer, 0, nf + j))],
        out_specs=pl.BlockSpec((tm, tf), lambda j, i: (i, j)),
        scratch_shapes=[pltpu.VMEM((k, tf), BF16)] * 2 + [pltpu.VMEM((SUBLANES, tf), F32)] * 2,
        compiler_params=_cparams(("parallel", "arbitrary")),
        name="ffn_up_conv_gate",
    )(xb, w_up, w_up, conv_w, conv_w)


def _ffn_down_kernel(h_ref, w_ref, x_ref, lw_ref, lb_ref, o32_ref, o16_ref, acc_ref, *, alpha, nk):
    k = pl.program_id(1)

    @pl.when(k == 0)
    def _():
        acc_ref[...] = _mm(h_ref[...], w_ref[...])

    @pl.when((k > 0) & (k < nk - 1))
    def _():
        acc_ref[...] += _mm(h_ref[...], w_ref[...])

    @pl.when(k == nk - 1)
    def _():
        for s in range(h_ref.shape[0] // LN_SUBROWS):
            rows = slice(s * LN_SUBROWS, (s + 1) * LN_SUBROWS)
            acc = acc_ref[rows, :] + _mm(h_ref[rows, :], w_ref[...])
            y = _layer_norm_rows(alpha * x_ref[rows, :] + acc, lw_ref[...], lb_ref[...])
            o32_ref[rows, :] = y
            o16_ref[rows, :] = y.astype(BF16)


def _ffn_down(h, w, x, lw, lb, alpha, layer, tm=512, tk=2816):
    m, kdim = h.shape
    n = w.shape[2]
    nk = kdim // tk
    assert nk >= 2
    return pl.pallas_call(
        functools.partial(_ffn_down_kernel, alpha=alpha, nk=nk),
        out_shape=(jax.ShapeDtypeStruct((m, n), F32), jax.ShapeDtypeStruct((m, n), BF16)),
        grid=(m // tm, nk),
        in_specs=[pl.BlockSpec((tm, tk), lambda i, k: (i, k)),
                  pl.BlockSpec((None, tk, n), lambda i, k: (layer, k, 0)),
                  pl.BlockSpec((tm, n), lambda i, k: (i, 0)),
                  pl.BlockSpec((1, n), lambda i, k: (0, 0)),
                  pl.BlockSpec((1, n), lambda i, k: (0, 0))],
        out_specs=(pl.BlockSpec((tm, n), lambda i, k: (i, 0)),
                   pl.BlockSpec((tm, n), lambda i, k: (i, 0))),
        scratch_shapes=[pltpu.VMEM((tm, n), F32)],
        compiler_params=_cparams(("parallel", "arbitrary")),
        name="ffn_down_ln",
    )(h, w, x, lw, lb)


def _ple_kernel(xb_ref, wg_ref, p_ref, wp_ref, x_ref, o32_ref, o16_ref, wgb_ref):
    @pl.when(pl.program_id(1) == 0)
    def _():
        wgb_ref[...] = wg_ref[...].astype(BF16)

    wp = wp_ref[...].astype(BF16)
    for s in range(xb_ref.shape[0] // LN_SUBROWS):
        rows = slice(s * LN_SUBROWS, (s + 1) * LN_SUBROWS)
        gate = _sigmoid(_mm(xb_ref[rows, :], wgb_ref[...]))
        y = x_ref[rows, :] + gate * _mm(p_ref[rows, :], wp)
        o32_ref[rows, :] = y
        o16_ref[rows, :] = y.astype(BF16)


def _ple(xb, x, p, wg, wp, layer, tm=512, tn=1024):
    m, k = xb.shape
    n = wg.shape[2]
    kp = p.shape[2]
    return pl.pallas_call(
        _ple_kernel,
        out_shape=(jax.ShapeDtypeStruct((m, n), F32), jax.ShapeDtypeStruct((m, n), BF16)),
        grid=(n // tn, m // tm),
        in_specs=[pl.BlockSpec((tm, k), lambda j, i: (i, 0)),
                  pl.BlockSpec((None, k, tn), lambda j, i: (layer, 0, j)),
                  pl.BlockSpec((None, tm, kp), lambda j, i: (layer, i, 0)),
                  pl.BlockSpec((None, kp, tn), lambda j, i: (layer, 0, j)),
                  pl.BlockSpec((tm, tn), lambda j, i: (i, j))],
        out_specs=(pl.BlockSpec((tm, tn), lambda j, i: (i, j)),
                   pl.BlockSpec((tm, tn), lambda j, i: (i, j))),
        scratch_shapes=[pltpu.VMEM((k, tn), BF16)],
        compiler_params=_cparams(("parallel", "arbitrary")),
        name="ple_gate",
    )(xb, wg, p, wp, x)


def _chunk_tri(tb):
    rb = _iota2((tb, tb), 0)
    cb = _iota2((tb, tb), 1)
    return ((cb <= rb) & ((cb // CHUNK) == (rb // CHUNK))).astype(F32)


def _gla_block(q, k, g, v, st_ref, heads, dk, dv):
    tb = q.shape[0]
    c = CHUNK
    causal = _iota2((c, c), 1) <= _iota2((c, c), 0)
    gc_all = _mm_sel(_chunk_tri(tb), g)
    nchunk = tb // c
    items = [(h, ci) for h in range(heads) for ci in range(nchunk)]

    def part(x, h, ci, d):
        return x[ci * c:(ci + 1) * c, h * d:(h + 1) * d]

    gcs = [part(gc_all, h, ci, dk) for h, ci in items]
    scores = [_mm(part(q, h, ci, dk) * jnp.exp(gc - gc[c // 2:c // 2 + 1]),
                  part(k, h, ci, dk) * jnp.exp(gc[c // 2:c // 2 + 1] - gc), NT)
              for (h, ci), gc in zip(items, gcs)]
    intra = [_mm(jnp.where(causal, a, 0.0), part(v, h, ci, dv)) for (h, ci), a in zip(items, scores)]
    incs = [_mm(part(v, h, ci, dv), part(k, h, ci, dk) * jnp.exp(gc[c - 1:c] - gc), TN)
            for (h, ci), gc in zip(items, gcs)]
    qgs = [(part(q, h, ci, dk) * jnp.exp(gc)).astype(BF16) for (h, ci), gc in zip(items, gcs)]
    sts = [st_ref[h] for h in range(heads)]
    o_h = [[] for _ in range(heads)]
    for ci in range(nchunk):
        for h in range(heads):
            i = h * nchunk + ci
            o_h[h].append(intra[i] + _mm(qgs[i], sts[h], NT))
            sts[h] = sts[h] * jnp.exp(gcs[i][c - 1:c]) + incs[i]
    for h in range(heads):
        st_ref[h] = sts[h]
    return [jnp.concatenate(o, axis=0) for o in o_h]


def _head_rms(o, gain, gate):
    o = o * lax.rsqrt(jnp.mean(o * o, axis=-1, keepdims=True) + RMS_EPS)
    return o * gain * _silu(gate)


def _hgrn2_kernel(q_ref, f_ref, i_ref, g_ref, lg_ref, nw_ref, o_ref, st_ref, *, layer, heads, nchunk):
    @pl.when(pl.program_id(2) == 0)
    def _():
        st_ref[...] = jnp.zeros_like(st_ref)

    lg = lg_ref[...]
    e = jnp.exp(lg - jnp.max(lg, axis=0, keepdims=True))
    lb = jnp.sum(e[0:layer + 1], axis=0, keepdims=True) / jnp.sum(e, axis=0, keepdims=True)

    f = lb + (1.0 - lb) * _sigmoid(f_ref[0])
    outs = _gla_block(_silu(q_ref[0]), 1.0 - f, jnp.log(f), i_ref[0], st_ref, heads, LANES, LANES)
    for h in range(heads):
        cols = slice(h * LANES, (h + 1) * LANES)
        o_ref[0, :, cols] = _head_rms(outs[h], nw_ref[:, cols], g_ref[0, :, cols]).astype(o_ref.dtype)


def _hgrn2(proj, logits, norm_w, layer, tb=256, heads=8):
    b, t, _ = proj.shape
    w = LANES * heads
    per = GROUP_W // w

    def col(section):
        return lambda bi, hi, ti: (bi, ti, section * per + hi)

    return pl.pallas_call(
        functools.partial(_hgrn2_kernel, layer=layer, heads=heads, nchunk=tb // CHUNK),
        out_shape=jax.ShapeDtypeStruct((b, t, GROUP_W), BF16),
        grid=(b, per, t // tb),
        in_specs=[pl.BlockSpec((1, tb, w), col(0)), pl.BlockSpec((1, tb, w), col(1)),
                  pl.BlockSpec((1, tb, w), col(2)), pl.BlockSpec((1, tb, w), col(3)),
                  pl.BlockSpec((logits.shape[0], w), lambda bi, hi, ti: (0, hi)),
                  pl.BlockSpec((1, w), lambda bi, hi, ti: (0, hi))],
        out_specs=pl.BlockSpec((1, tb, w), lambda bi, hi, ti: (bi, ti, hi)),
        scratch_shapes=[pltpu.VMEM((heads, LANES, LANES), F32)],
        compiler_params=_cparams(("parallel", "parallel", "arbitrary")),
        name="hgrn2_chunk",
    )(proj, proj, proj, proj, logits, norm_w)


def _gla_kernel(q_ref, k_ref, v_ref, r_ref, gd_ref, w2_ref, b_ref, nw_ref, o_ref, st_ref, *, heads):
    @pl.when(pl.program_id(2) == 0)
    def _():
        st_ref[...] = jnp.zeros_like(st_ref)

    z = _mm_hi(gd_ref[0], w2_ref[...]) + b_ref[...]
    g = -_softplus(-z) * (1.0 / GLA_TAU)
    outs = _gla_block(q_ref[0] * GLA_DK ** -0.5, k_ref[0], g, v_ref[0], st_ref, heads, GLA_DK, GLA_DV)
    for h in range(heads):
        cols = slice(h * GLA_DV, (h + 1) * GLA_DV)
        o_ref[0, :, cols] = _head_rms(outs[h], nw_ref[:, cols], r_ref[0, :, cols]).astype(o_ref.dtype)


def _gla(proj, tail, w2, bias, norm_w, tb=256, heads=4):
    b, t, _ = proj.shape
    wk = GLA_DK * heads
    wv = GLA_DV * heads
    groups = GLA_HEADS // heads
    qk0 = 4 * GROUP_W // wk
    v0 = (4 * GROUP_W + 2 * GLA_HEADS * GLA_DK) // wv
    gd_blk = GROUP_W // LANES
    return pl.pallas_call(
        functools.partial(_gla_kernel, heads=heads),
        out_shape=jax.ShapeDtypeStruct((b, t, GROUP_W), BF16),
        grid=(b, groups, t // tb),
        in_specs=[pl.BlockSpec((1, tb, wk), lambda bi, hi, ti: (bi, ti, qk0 + hi)),
                  pl.BlockSpec((1, tb, wk), lambda bi, hi, ti: (bi, ti, qk0 + groups + hi)),
                  pl.BlockSpec((1, tb, wv), lambda bi, hi, ti: (bi, ti, v0 + hi)),
                  pl.BlockSpec((1, tb, wv), lambda bi, hi, ti: (bi, ti, hi)),
                  pl.BlockSpec((1, tb, LANES), lambda bi, hi, ti: (bi, ti, gd_blk)),
                  pl.BlockSpec((LANES, wk), lambda bi, hi, ti: (0, hi)),
                  pl.BlockSpec((1, wk), lambda bi, hi, ti: (0, hi)),
                  pl.BlockSpec((1, wv), lambda bi, hi, ti: (0, hi))],
        out_specs=pl.BlockSpec((1, tb, wv), lambda bi, hi, ti: (bi, ti, hi)),
        scratch_shapes=[pltpu.VMEM((heads, GLA_DV, GLA_DK), F32)],
        compiler_params=_cparams(("parallel", "parallel", "arbitrary")),
        name="gla_chunk",
    )(proj, proj, proj, tail, tail, w2, bias, norm_w)


def _neumann_inverses(mats):
    n = mats[0].shape[0]
    pack = 2 if (2 * n <= LANES and len(mats) % 2 == 0) else 1
    width = pack * n
    eye = (_iota2((n, width), 0) == _iota2((n, width), 1) % n).astype(F32)
    left = _iota2((n, width), 1) < n

    def rhs(x):
        if pack == 1:
            return x
        return jnp.concatenate([jnp.where(left, x, 0.0), jnp.where(left, 0.0, x)], axis=0)

    if pack == 2:
        mats = [jnp.concatenate([mats[i], mats[i + 1]], axis=1) for i in range(0, len(mats), 2)]
    ts = [eye + a for a in mats]
    ps = [_mm(a, rhs(a)) for a in mats]
    steps = max(1, (CHUNK - 1).bit_length() - 1)
    for _ in range(steps - 1):
        both = [_mm(jnp.concatenate([t, p], axis=0), rhs(p)) for t, p in zip(ts, ps)]
        ts = [t + b[0:n] for t, b in zip(ts, both)]
        ps = [b[n:2 * n] for b in both]
    ts = [t + _mm(t, rhs(p)) for t, p in zip(ts, ps)]
    if pack == 1:
        return ts
    out = []
    for t in ts:
        out += [t[:, 0:n], t[:, n:2 * n]]
    return out


def _shifted(x, carry_ref, shift):
    tb = x.shape[0]
    ext = jnp.concatenate([carry_ref[...], x], axis=0)
    return pltpu.roll(ext, shift, 0)[SUBLANES:SUBLANES + tb]


def _gdn_kernel(q_ref, k_ref, v_ref, z_ref, ab_ref, cq_ref, ck_ref, cv_ref, al_ref, dt_ref, nw_ref,
                o_ref, st_ref, cq_s, ck_s, cv_s, *, nchunk, heads):
    head0 = pl.program_id(1) * heads

    @pl.when(pl.program_id(2) == 0)
    def _():
        st_ref[...] = jnp.zeros_like(st_ref)
        cq_s[...] = jnp.zeros_like(cq_s)
        ck_s[...] = jnp.zeros_like(ck_s)
        cv_s[...] = jnp.zeros_like(cv_s)

    tb = q_ref.shape[1]

    def conv_silu(x_ref, carry, w_ref):
        x = x_ref[0]
        w = w_ref[...]
        y = x * w[3:4]
        for j in (1, 2, 3):
            y = y + _shifted(x, carry, j) * w[3 - j:4 - j]
        carry[...] = x[tb - SUBLANES:tb]
        return _silu(y)

    def l2n(x):
        return x * lax.rsqrt(jnp.sum(x * x, axis=-1, keepdims=True) + L2_EPS)

    q_all = conv_silu(q_ref, cq_s, cq_ref)
    k_all = conv_silu(k_ref, ck_s, ck_ref)
    v_all = conv_silu(v_ref, cv_s, cv_ref)

    c = CHUNK
    incl = _iota2((c, c), 1) <= _iota2((c, c), 0)
    strict = _iota2((c, c), 1) < _iota2((c, c), 0)
    rb = _iota2((tb, tb), 0)
    cb = _iota2((tb, tb), 1)
    chunk_tri = ((cb <= rb) & ((cb // c) == (rb // c))).astype(F32)
    lane = _iota2((tb, LANES), 1)
    ab = ab_ref[0]
    gmat = -jnp.exp(al_ref[...]) * _softplus(ab + dt_ref[...])
    gcum = _mm_sel(chunk_tri, gmat)
    items = [(h, ci) for h in range(heads) for ci in range(nchunk)]
    per_head = []
    for h in range(heads):
        cols = slice(h * GDN_HEAD, (h + 1) * GDN_HEAD)
        q = l2n(q_all[:, cols]) * GDN_HEAD ** -0.5
        k = l2n(k_all[:, cols])
        g_col = jnp.sum(jnp.where(lane == head0 + h, gcum, 0.0), axis=1, keepdims=True)
        beta = _sigmoid(jnp.sum(jnp.where(lane == head0 + h + GDN_HEADS, ab, 0.0), axis=1, keepdims=True))
        eg = jnp.exp(g_col)
        kb = k * beta
        per_head.append(dict(q=q, k=k, g_col=g_col, kb=kb, qg=q * eg,
                             rhs=jnp.concatenate([v_all[:, cols] * beta, kb * eg], axis=1),
                             pick=(_iota2((c, LANES), 1) == head0 + h).astype(F32)))

    def rows(name, h, ci):
        return per_head[h][name][ci * c:(ci + 1) * c]

    g_rows = [_mm_sel(per_head[h]["pick"], gcum[ci * c:(ci + 1) * c], NT) for h, ci in items]
    kks = [_mm(rows("kb", h, ci), rows("k", h, ci), NT) for h, ci in items]
    qks = [_mm(rows("q", h, ci), rows("k", h, ci), NT) for h, ci in items]
    gams = [jnp.where(incl, jnp.exp(jnp.where(incl, rows("g_col", h, ci) - gr, 0.0)), 0.0)
            for (h, ci), gr in zip(items, g_rows)]
    lows = [jnp.where(strict, kk * gam, 0.0) for kk, gam in zip(kks, gams)]
    aqks = [jnp.where(incl, qk * gam, 0.0).astype(BF16) for qk, gam in zip(qks, gams)]
    g_lasts = [rows("g_col", h, ci)[c - 1:c] for h, ci in items]
    kgts = [(rows("k", h, ci) * jnp.exp(gl - rows("g_col", h, ci))).T.astype(BF16)
            for (h, ci), gl in zip(items, g_lasts)]
    decays = [jnp.exp(gl) for gl in g_lasts]
    tinvs = _neumann_inverses([-low for low in lows])
    uws = [_mm(tinvs[i], per_head[h]["rhs"][ci * c:(ci + 1) * c]) for i, (h, ci) in enumerate(items)]
    ows = [_mm(aqks[i], uws[i]) for i in range(len(items))]
    kws = [_mm(kgts[i], uws[i]) for i in range(len(items))]
    o0s = [ow[:, 0:GDN_HEAD] for ow in ows]
    qps = [(per_head[h]["qg"][ci * c:(ci + 1) * c] - ows[i][:, GDN_HEAD:2 * GDN_HEAD]).astype(BF16)
           for i, (h, ci) in enumerate(items)]
    nts = [kw[:, 0:GDN_HEAD] for kw in kws]
    mts = [(-kw[:, GDN_HEAD:2 * GDN_HEAD]).astype(BF16) for kw in kws]

    states = [st_ref[h] for h in range(heads)]
    outs = [[] for _ in range(heads)]
    for ci in range(nchunk):
        for h in range(heads):
            i = h * nchunk + ci
            sb = states[h].astype(BF16)
            outs[h].append(o0s[i] + _mm(qps[i], sb))
            states[h] = states[h] * decays[i] + _mm(mts[i], sb) + nts[i]
    for h in range(heads):
        cols = slice(h * GDN_HEAD, (h + 1) * GDN_HEAD)
        st_ref[h] = states[h]
        o = jnp.concatenate(outs[h], axis=0)
        o_ref[0, :, cols] = _head_rms(o, nw_ref[:, cols], z_ref[0, :, cols]).astype(o_ref.dtype)


def _gdn(proj, small, conv_w, alog_pad, dt_pad, norm_w, tb=256, heads=8):
    b, t, _ = proj.shape
    w = GDN_HEAD * heads
    groups = GDN_HEADS // heads
    base = 0
    ab_blk = (small.shape[2] - LANES) // LANES

    def col(section):
        return lambda bi, hi, ti: (bi, ti, base + section * groups + hi)

    def cw(section):
        return pl.BlockSpec((4, w), lambda bi, hi, ti: (0, section * groups + hi))

    return pl.pallas_call(
        functools.partial(_gdn_kernel, nchunk=tb // CHUNK, heads=heads),
        out_shape=jax.ShapeDtypeStruct((b, t, GROUP_W), BF16),
        grid=(b, groups, t // tb),
        in_specs=[pl.BlockSpec((1, tb, w), col(0)), pl.BlockSpec((1, tb, w), col(1)),
                  pl.BlockSpec((1, tb, w), col(2)), pl.BlockSpec((1, tb, w), col(3)),
                  pl.BlockSpec((1, tb, LANES), lambda bi, hi, ti: (bi, ti, ab_blk)),
                  cw(0), cw(1), cw(2),
                  pl.BlockSpec((1, LANES), lambda bi, hi, ti: (0, 0)),
                  pl.BlockSpec((1, LANES), lambda bi, hi, ti: (0, 0)),
                  pl.BlockSpec((1, w), lambda bi, hi, ti: (0, hi))],
        out_specs=pl.BlockSpec((1, tb, w), lambda bi, hi, ti: (bi, ti, hi)),
        scratch_shapes=[pltpu.VMEM((heads, GDN_HEAD, GDN_HEAD), F32)]
        + [pltpu.VMEM((SUBLANES, w), F32)] * 3,
        compiler_params=_cparams(("parallel", "parallel", "arbitrary")),
        name="gdn_chunk",
    )(proj, proj, proj, proj, small, conv_w, conv_w, conv_w, alog_pad, dt_pad, norm_w)


def _rwkv_kernel(r_ref, k_ref, v_ref, sm_ref, mur_ref, muk_ref, muv_ref, mus_ref, w0_ref, w2_ref, a0_ref,
                 a2_ref, g2_ref, kk_ref, ka_ref, rk_ref, gw_ref, gb_ref, o_ref,
                 st_ref, cr_s, ck_s, cv_s, cs_s, *, nchunk, lora):
    @pl.when(pl.program_id(2) == 0)
    def _():
        for ref in (st_ref, cr_s, ck_s, cv_s, cs_s):
            ref[...] = jnp.zeros_like(ref)

    tb = r_ref.shape[1]
    half = RWKV_HEAD
    block_diag = (_iota2((LANES, LANES), 0) // half) == (_iota2((LANES, LANES), 1) // half)
    ones_bd = block_diag.astype(BF16)

    def seg_sum(x):
        hi = x.astype(BF16)
        lo = (x - hi.astype(F32)).astype(BF16)
        parts = []
        for j in range(x.shape[1] // LANES):
            cols = slice(j * LANES, (j + 1) * LANES)
            parts.append(lax.dot_general(hi[:, cols], ones_bd, NN, preferred_element_type=F32)
                         + lax.dot_general(lo[:, cols], ones_bd, NN, preferred_element_type=F32))
        return parts[0] if len(parts) == 1 else jnp.concatenate(parts, axis=1)

    def token_shift(x, carry, mu):
        prev = _shifted(x, carry, 1)
        carry[...] = x[tb - SUBLANES:tb]
        return x + (prev - x) * mu

    r = token_shift(r_ref[0], cr_s, mur_ref[...])
    k = token_shift(k_ref[0], ck_s, muk_ref[...])
    v = token_shift(v_ref[0], cv_s, muv_ref[...])
    sm = token_shift(sm_ref[0], cs_s, mus_ref[...])
    wd = sm[:, 0:lora]
    ad = sm[:, lora:2 * lora]
    gd = sm[:, 2 * lora:2 * lora + g2_ref.shape[0]]
    z = w0_ref[...] + _mm_hi(jnp.tanh(wd), w2_ref[...])
    lw = -jnp.exp(-_softplus(-z) - 0.5)
    ag = _sigmoid(a0_ref[...] + _mm_hi(ad, a2_ref[...]))
    gate = _mm(_sigmoid(gd), g2_ref[...])
    kk = k * kk_ref[...]
    kk = kk * lax.rsqrt(seg_sum(kk * kk) + L2_EPS)
    k2 = k * (1.0 + (ag - 1.0) * ka_ref[...])
    aa_all = -kk
    bb_all = kk * ag
    npair = r.shape[1] // LANES

    c = CHUNK
    m0 = _iota2((c, LANES), 1) < half
    row = _iota2((c, LANES), 0)
    sub = _iota2((c, LANES), 1) % c
    strict = sub < row
    incl = sub <= row
    rb = _iota2((tb, tb), 0)
    cb = _iota2((tb, tb), 1)
    chunk_tri = ((cb <= rb) & ((cb // c) == (rb // c))).astype(F32)
    gc_all = _mm_sel(chunk_tri, lw)
    gx_all = gc_all - lw

    def stack(x):
        return jnp.concatenate([jnp.where(m0, x, 0.0), jnp.where(m0, 0.0, x)], axis=0)

    def unstack(x):
        return jnp.where(m0, x[0:c], x[c:2 * c])

    items = [(pi, ci) for pi in range(npair) for ci in range(nchunk)]
    a_bds, ak_xs, wmats, a_sts, v_sws, r_abss, nvs, bes, decays = [], [], [], [], [], [], [], [], []
    for pi, ci in items:
        sl = (slice(ci * c, (ci + 1) * c), slice(pi * LANES, (pi + 1) * LANES))
        rr, kc, vc, aa, bb = r[sl], k2[sl], v[sl], aa_all[sl], bb_all[sl]
        gc, gx = gc_all[sl], gx_all[sl]
        g_mid = gc[c // 2 - 1:c // 2]
        g_last = gc[c - 1:c]
        e_out = jnp.exp(g_mid - gc)
        at = aa * jnp.exp(gx - g_mid)
        rt = rr * jnp.exp(gc - g_mid)
        bt = bb * e_out
        kt = kc * e_out
        lhs0 = jnp.concatenate([jnp.where(m0, at, 0.0), jnp.where(m0, rt, 0.0)], axis=0)
        lhs1 = jnp.concatenate([jnp.where(m0, 0.0, at), jnp.where(m0, 0.0, rt)], axis=0)
        res0 = _mm(lhs0, jnp.concatenate([bt, kt], axis=0), NT)
        res1 = _mm(lhs1, jnp.concatenate([kt, bt], axis=0), NT)
        top0 = jnp.where(strict, res0[0:c], 0.0)
        top1 = jnp.where(strict, res1[0:c], 0.0)
        a_bds.append(jnp.concatenate([jnp.where(m0, top0, 0.0), jnp.where(m0, 0.0, top1)], axis=0))
        ak_xs.append(jnp.concatenate([jnp.where(m0, 0.0, top0), jnp.where(m0, top1, 0.0)], axis=0))
        wmats.append(jnp.concatenate([jnp.where(incl, res0[c:2 * c], 0.0),
                                      jnp.where(incl, res1[c:2 * c], 0.0)], axis=0))
        a_sts.append(stack(aa * jnp.exp(gx)))
        v_sws.append(jnp.concatenate([jnp.where(m0, 0.0, vc), jnp.where(m0, vc, 0.0)], axis=0))
        r_abss.append(rr * jnp.exp(gc))
        e_end = jnp.exp(g_last - gc)
        bes.append(bb * e_end)
        nvs.append((vc, kc * e_end))
        decays.append(jnp.exp(g_last))
    tinvs = _neumann_inverses(a_bds)

    n_items = len(items)
    rhs_vs = [_mm(ak_xs[i], v_sws[i]) for i in range(n_items)]
    uws = [_mm(tinvs[i], jnp.concatenate([rhs_vs[i], a_sts[i]], axis=1)) for i in range(n_items)]
    yws = [_mm(wmats[i], jnp.concatenate([uws[i][:, 0:LANES] + v_sws[i], uws[i][:, LANES:2 * LANES]], axis=1))
           for i in range(n_items)]
    y0s = [unstack(yws[i][:, 0:LANES]) for i in range(n_items)]
    qps = [(unstack(yws[i][:, LANES:2 * LANES]) + r_abss[i]).astype(BF16) for i in range(n_items)]
    gbds = [jnp.where(block_diag, _mm(unstack(uws[i][:, LANES:2 * LANES]), bes[i], TN), 0.0).astype(BF16)
            for i in range(n_items)]
    ncs = [jnp.where(block_diag, _mm(jnp.concatenate([unstack(uws[i][:, 0:LANES]), nvs[i][0]], axis=0),
                                     jnp.concatenate([bes[i], nvs[i][1]], axis=0), TN), 0.0)
           for i in range(n_items)]

    hts = [st_ref[pi] for pi in range(npair)]
    ys = [[] for _ in range(npair)]
    for ci in range(nchunk):
        for pi in range(npair):
            i = pi * nchunk + ci
            htb = hts[pi].astype(BF16)
            ys[pi].append(y0s[i] + _mm(qps[i], htb, NT))
            hts[pi] = hts[pi] * decays[i] + _mm(htb, gbds[i]) + ncs[i]
    for pi in range(npair):
        st_ref[pi] = hts[pi]

    ycols = [jnp.concatenate(ys[pi], axis=0) for pi in range(npair)]
    y = ycols[0] if npair == 1 else jnp.concatenate(ycols, axis=1)
    mean = seg_sum(y) * (1.0 / half)
    yc = y - mean
    var = seg_sum(yc * yc) * (1.0 / half)
    yn = yc * lax.rsqrt(var + RWKV_GN_EPS) * gw_ref[...] + gb_ref[...]
    bonus = seg_sum(r * k2 * rk_ref[...]) * v
    o_ref[0] = ((yn + bonus) * gate).astype(o_ref.dtype)


def _rwkv(proj, small, mu_main, mu_small, w0, w2p, a0, a2p, g2, kk, ka, rk, gw, gb, lora, tb=256, npair=8):
    b, t, _ = proj.shape
    w = LANES * npair
    groups = GROUP_W // w
    ws = small.shape[2]
    glora = g2.shape[0]

    def col(section):
        return lambda bi, hi, ti: (bi, ti, section * groups + hi)

    def vec(section=0):
        return pl.BlockSpec((1, w), lambda bi, hi, ti: (0, section * groups + hi))

    row_blk = lambda s: pl.BlockSpec((1, tb, w), col(s))
    return pl.pallas_call(
        functools.partial(_rwkv_kernel, nchunk=tb // CHUNK, lora=lora),
        out_shape=jax.ShapeDtypeStruct((b, t, GROUP_W), BF16),
        grid=(b, groups, t // tb),
        in_specs=[row_blk(0), row_blk(1), row_blk(2),
                  pl.BlockSpec((1, tb, ws), lambda bi, hi, ti: (bi, ti, 0)),
                  vec(0), vec(1), vec(2),
                  pl.BlockSpec((1, ws), lambda bi, hi, ti: (0, 0)),
                  vec(), pl.BlockSpec((lora, w), lambda bi, hi, ti: (0, hi)),
                  vec(), pl.BlockSpec((lora, w), lambda bi, hi, ti: (0, hi)),
                  pl.BlockSpec((glora, w), lambda bi, hi, ti: (0, hi)),
                  vec(), vec(), vec(), vec(), vec()],
        out_specs=pl.BlockSpec((1, tb, w), lambda bi, hi, ti: (bi, ti, hi)),
        scratch_shapes=[pltpu.VMEM((npair, LANES, LANES), F32)]
        + [pltpu.VMEM((SUBLANES, w), F32)] * 3
        + [pltpu.VMEM((SUBLANES, ws), F32)],
        compiler_params=_cparams(("parallel", "parallel", "arbitrary")),
        name="rwkv7_chunk",
    )(proj, proj, proj, small, mu_main, mu_main, mu_main, mu_small, w0, w2p, a0, a2p, g2,
      kk, ka, rk, gw, gb)


def _pad_cols(a, width):
    return jnp.pad(a, ((0, 0), (0, width - a.shape[1])))


def _pad_rows(a, height):
    return jnp.pad(a, ((0, height - a.shape[0]), (0, 0)))


def _row(v):
    return v.reshape(1, -1).astype(F32)


def kernel(x, p, hgrn_lb_logits, e_w_in, e_gla_w2, e_gla_b, e_hgrn_norm, e_gla_norm, e_w_out, o_w_in, o_rwkv_mu, o_rwkv_w0, o_rwkv_w2, o_rwkv_a0, o_rwkv_a2, o_rwkv_g2, o_rwkv_kk, o_rwkv_ka, o_rwkv_rk, o_rwkv_gn_w, o_rwkv_gn_b, o_gdn_conv, o_gdn_a_log, o_gdn_dt_bias, o_gdn_norm, o_w_out, ln_mix_w, ln_mix_b, ln_ffn_w, ln_ffn_b, ffn_w_up, ffn_conv, ffn_w_down, ple_w_proj, ple_w_gate):
    bsz, seq, d = x.shape
    m = bsz * seq
    depth = ln_mix_w.shape[0]
    alpha = (2 * depth) ** 0.25
    gw = GROUP_W
    x32 = x.reshape(m, d)
    xb = x32.astype(BF16)
    w_down = ffn_w_down.astype(BF16)
    for layer in range(depth):
        j = layer // 2
        if layer % 2 == 0:
            rank = e_gla_w2.shape[1]
            gd0 = 4 * gw + 2 * GLA_HEADS * GLA_DK + gw
            wt = jnp.swapaxes(e_w_in, 1, 2)
            w_tail = jnp.concatenate([wt[j, gd0 + rank:], _pad_rows(wt[j, gd0:gd0 + rank], LANES)],
                                     axis=0)
            proj = _proj(xb, wt, gd0, layer=j).reshape(bsz, seq, -1)
            tail = _proj(xb, w_tail, w_tail.shape[0], tn=w_tail.shape[0]).reshape(bsz, seq, -1)
            o_a = _hgrn2(proj, hgrn_lb_logits.astype(F32), _row(e_hgrn_norm[j]), layer)
            o_b = _gla(proj, tail, _pad_rows(e_gla_w2[j], LANES), _row(e_gla_b[j]), _row(e_gla_norm[j]))
            w_out = e_w_out[j]
        else:
            wt = jnp.swapaxes(o_w_in, 1, 2)
            lora_w = o_rwkv_w2.shape[1]
            lora_a = o_rwkv_a2.shape[1]
            lora_g = o_rwkv_g2.shape[1]
            lora = LANES * (-(-max(lora_w, lora_a) // LANES))
            c_wd = 3 * gw
            c_ad = c_wd + lora_w
            c_gd = c_ad + lora_a
            c_qkv = c_gd + lora_g
            c_z = c_qkv + 3 * gw
            c_ab = c_z + gw
            w_small = jnp.concatenate([_pad_rows(wt[j, c_wd:c_ad], lora), _pad_rows(wt[j, c_ad:c_gd], lora),
                                       wt[j, c_gd:c_qkv], _pad_rows(wt[j, c_ab:], LANES)], axis=0)
            mu = o_rwkv_mu[j].reshape(1, -1)
            mu_small = jnp.concatenate([_pad_cols(mu[:, c_wd:c_ad], lora), _pad_cols(mu[:, c_ad:c_gd], lora),
                                        mu[:, c_gd:c_qkv], jnp.zeros((1, LANES), F32)], axis=1)
            proj_rkv = _proj(xb, wt, c_wd, layer=j).reshape(bsz, seq, -1)
            proj_gdn = _proj(xb, wt[j, c_qkv:c_ab], c_ab - c_qkv).reshape(bsz, seq, -1)
            small = _proj(xb, w_small, w_small.shape[0], tn=w_small.shape[0]).reshape(bsz, seq, -1)
            o_a = _rwkv(proj_rkv, small, mu[:, :c_wd], mu_small, _row(o_rwkv_w0[j]),
                        _pad_rows(o_rwkv_w2[j], lora), _row(o_rwkv_a0[j]), _pad_rows(o_rwkv_a2[j], lora),
                        o_rwkv_g2[j], _row(o_rwkv_kk[j]), _row(o_rwkv_ka[j]), _row(o_rwkv_rk[j]),
                        _row(o_rwkv_gn_w[j]), _row(o_rwkv_gn_b[j]), lora)
            o_b = _gdn(proj_gdn, small, o_gdn_conv[j].astype(F32),
                       _pad_cols(_row(o_gdn_a_log[j]), LANES), _pad_cols(_row(o_gdn_dt_bias[j]), LANES),
                       _row(o_gdn_norm[j]))
            w_out = o_w_out[j]
        x32, xb = _mix_out(o_a.reshape(m, gw), o_b.reshape(m, gw), w_out.astype(BF16), x32,
                           _row(ln_mix_w[layer]), _row(ln_mix_b[layer]), alpha)
        h = _ffn_up(xb, ffn_w_up, ffn_conv, layer, seq)
        x32, xb = _ffn_down(h, w_down, x32, _row(ln_ffn_w[layer]), _row(ln_ffn_b[layer]), alpha, layer)
        x32, xb = _ple(xb, x32, p.reshape(depth, m, -1), ple_w_gate, ple_w_proj, layer)
    return x32.reshape(bsz, seq, d)
```

```python
import functools

import jax
import jax.numpy as jnp
from jax import lax
from jax.experimental import pallas as pl
from jax.experimental.pallas import tpu as pltpu

F32 = jnp.float32
BF16 = jnp.bfloat16

NN = (((1,), (0,)), ((), ()))
NT = (((1,), (1,)), ((), ()))
TN = (((0,), (0,)), ((), ()))

D_MODEL = 2048
GROUP_W = 1024
CHUNK = 64
LANES = 128
SUBLANES = 8
HGRN_HEADS = 8
GLA_HEADS = 4
GLA_DK = 128
GLA_DV = 256
GLA_TAU = 16.0
RWKV_HEAD = 64
RWKV_GN_EPS = 64e-5
GDN_HEADS = 8
GDN_HEAD = 128
FFN_DIM = 5632
LN_EPS = 1e-5
LN_SUBROWS = 128
RMS_EPS = 1e-6
L2_EPS = 1e-6

VMEM_LIMIT = 56 * 1024 * 1024


def _mm(a, b, dims=NN):
    return lax.dot_general(a.astype(BF16), b.astype(BF16), dims, preferred_element_type=F32)


def _split_bf16(x, parts):
    out = []
    for _ in range(parts - 1):
        h = x.astype(BF16)
        out.append(h)
        x = x - h.astype(F32)
    out.append(x.astype(BF16))
    return out


def _bdot(a, b, dims):
    return lax.dot_general(a, b, dims, preferred_element_type=F32)


def _mm_sel(sel, b, dims=NN):
    s = sel.astype(BF16)
    b1, b2, b3 = _split_bf16(b, 3)
    return (_bdot(s, b3, dims) + _bdot(s, b2, dims)) + _bdot(s, b1, dims)


def _mm_hi(a, b, dims=NN):
    ah, al = _split_bf16(a, 2)
    bh, bl = _split_bf16(b, 2)
    return (_bdot(al, bh, dims) + _bdot(ah, bl, dims)) + _bdot(ah, bh, dims)


def _sigmoid(x):
    return 1.0 / (1.0 + jnp.exp(-x))


def _silu(x):
    return x * _sigmoid(x)


def _softplus(x):
    return jnp.maximum(x, 0.0) + jnp.log(1.0 + jnp.exp(-jnp.abs(x)))


def _iota2(shape, dim):
    return lax.broadcasted_iota(jnp.int32, shape, dim)


def _cparams(sem):
    return pltpu.CompilerParams(dimension_semantics=sem, vmem_limit_bytes=VMEM_LIMIT)


def _proj_kernel(x_ref, w_ref, o_ref, wb_ref):
    @pl.when(pl.program_id(1) == 0)
    def _():
        wb_ref[...] = w_ref[...].T.astype(BF16)

    o_ref[...] = lax.dot_general(x_ref[...], wb_ref[...], NN, preferred_element_type=F32).astype(o_ref.dtype)


def _proj(x, wt, n_cols, out_dtype=F32, layer=None, tm=1024, tn=1024):
    m, k = x.shape
    tn = min(tn, n_cols)
    if layer is None:
        w_spec = pl.BlockSpec((tn, k), lambda j, i: (j, 0))
    else:
        w_spec = pl.BlockSpec((None, tn, k), lambda j, i: (layer, j, 0))
    return pl.pallas_call(
        _proj_kernel,
        out_shape=jax.ShapeDtypeStruct((m, n_cols), out_dtype),
        grid=(n_cols // tn, m // tm),
        in_specs=[pl.BlockSpec((tm, k), lambda j, i: (i, 0)), w_spec],
        out_specs=pl.BlockSpec((tm, tn), lambda j, i: (i, j)),
        scratch_shapes=[pltpu.VMEM((k, tn), BF16)],
        compiler_params=_cparams(("parallel", "arbitrary")),
        name="proj_matmul",
    )(x, wt)


def _layer_norm_rows(y, w, b):
    mu = jnp.mean(y, axis=-1, keepdims=True)
    yc = y - mu
    var = jnp.mean(yc * yc, axis=-1, keepdims=True)
    return yc * lax.rsqrt(var + LN_EPS) * w + b


def _mix_out_kernel(oa_ref, ob_ref, w_ref, x_ref, lw_ref, lb_ref, o32_ref, o16_ref, *, alpha):
    half = oa_ref.shape[1]
    for s in range(oa_ref.shape[0] // LN_SUBROWS):
        rows = slice(s * LN_SUBROWS, (s + 1) * LN_SUBROWS)
        acc = _mm(oa_ref[rows, :], w_ref[0:half, :]) + _mm(ob_ref[rows, :], w_ref[half:2 * half, :])
        y = _layer_norm_rows(alpha * x_ref[rows, :] + acc, lw_ref[...], lb_ref[...])
        o32_ref[rows, :] = y
        o16_ref[rows, :] = y.astype(BF16)


def _mix_out(oa, ob, w, x, lw, lb, alpha, tm=512):
    m, half = oa.shape
    n = w.shape[1]
    row = lambda i: (i, 0)
    fixed = lambda i: (0, 0)
    return pl.pallas_call(
        functools.partial(_mix_out_kernel, alpha=alpha),
        out_shape=(jax.ShapeDtypeStruct((m, n), F32), jax.ShapeDtypeStruct((m, n), BF16)),
        grid=(m // tm,),
        in_specs=[pl.BlockSpec((tm, half), row), pl.BlockSpec((tm, half), row),
                  pl.BlockSpec((2 * half, n), fixed), pl.BlockSpec((tm, n), row),
                  pl.BlockSpec((1, n), fixed), pl.BlockSpec((1, n), fixed)],
        out_specs=(pl.BlockSpec((tm, n), row), pl.BlockSpec((tm, n), row)),
        compiler_params=_cparams(("parallel",)),
        name="mix_out_ln",
    )(oa, ob, w, x, lw, lb)


def _ffn_up_kernel(x_ref, wg_ref, wv_ref, cg_ref, cv_ref, o_ref, wgb_ref, wvb_ref, hg_ref, hv_ref,
                   *, tiles_per_seq):
    i = pl.program_id(1)
    tm = x_ref.shape[0]

    @pl.when(i == 0)
    def _():
        wgb_ref[...] = wg_ref[...].astype(BF16)
        wvb_ref[...] = wv_ref[...].astype(BF16)
        hg_ref[...] = jnp.zeros_like(hg_ref)
        hv_ref[...] = jnp.zeros_like(hv_ref)

    first = (i % tiles_per_seq) == 0

    def conv(w_ref, h_ref, c_ref):
        u = lax.dot_general(x_ref[...], w_ref[...], NN, preferred_element_type=F32)
        halo = jnp.where(first, 0.0, h_ref[...])
        h_ref[...] = u[tm - SUBLANES:tm]
        ext = jnp.concatenate([halo, u], axis=0)
        p1 = pltpu.roll(ext, 1, 0)[SUBLANES:SUBLANES + tm]
        p2 = pltpu.roll(ext, 2, 0)[SUBLANES:SUBLANES + tm]
        c = c_ref[...]
        return u * c[2:3] + p1 * c[1:2] + p2 * c[0:1]

    g = conv(wgb_ref, hg_ref, cg_ref)
    v = conv(wvb_ref, hv_ref, cv_ref)
    o_ref[...] = (_silu(g) * v).astype(o_ref.dtype)


def _ffn_up(xb, w_up, conv_w, layer, seq, tm=1024, tf=512):
    m, k = xb.shape
    f = w_up.shape[2] // 2
    nf = f // tf
    return pl.pallas_call(
        functools.partial(_ffn_up_kernel, tiles_per_seq=seq // tm),
        out_shape=jax.ShapeDtypeStruct((m, f), BF16),
        grid=(nf, m // tm),
        in_specs=[pl.BlockSpec((tm, k), lambda j, i: (i, 0)),
                  pl.BlockSpec((None, k, tf), lambda j, i: (layer, 0, j)),
                  pl.BlockSpec((None, k, tf), lambda j, i: (layer, 0, nf + j)),
                  pl.BlockSpec((None, 3, tf), lambda j, i: (layer, 0, j)),
                  pl.BlockSpec((None, 3, tf), lambda j, i: (layer, 0, nf + j))],
        out_specs=pl.BlockSpec((tm, tf), lambda j, i: (i, j)),
        scratch_shapes=[pltpu.VMEM((k, tf), BF16)] * 2 + [pltpu.VMEM((SUBLANES, tf), F32)] * 2,
        compiler_params=_cparams(("parallel", "arbitrary")),
        name="ffn_up_conv_gate",
    )(xb, w_up, w_up, conv_w, conv_w)


def _ffn_down_kernel(h_ref, w_ref, x_ref, lw_ref, lb_ref, o32_ref, o16_ref, acc_ref, *, alpha, nk):
    k = pl.program_id(1)

    @pl.when(k == 0)
    def _():
        acc_ref[...] = _mm(h_ref[...], w_ref[...])

    @pl.when((k > 0) & (k < nk - 1))
    def _():
        acc_ref[...] += _mm(h_ref[...], w_ref[...])

    @pl.when(k == nk - 1)
    def _():
        for s in range(h_ref.shape[0] // LN_SUBROWS):
            rows = slice(s * LN_SUBROWS, (s + 1) * LN_SUBROWS)
            acc = acc_ref[rows, :] + _mm(h_ref[rows, :], w_ref[...])
            y = _layer_norm_rows(alpha * x_ref[rows, :] + acc, lw_ref[...], lb_ref[...])
            o32_ref[rows, :] = y
            o16_ref[rows, :] = y.astype(BF16)


def _ffn_down(h, w, x, lw, lb, alpha, layer, tm=1024, tk=1408):
    m, kdim = h.shape
    n = w.shape[2]
    nk = kdim // tk
    assert nk >= 2
    once = pl.Buffered(1)
    return pl.pallas_call(
        functools.partial(_ffn_down_kernel, alpha=alpha, nk=nk),
        out_shape=(jax.ShapeDtypeStruct((m, n), F32), jax.ShapeDtypeStruct((m, n), BF16)),
        grid=(m // tm, nk),
        in_specs=[pl.BlockSpec((tm, tk), lambda i, k: (i, k)),
                  pl.BlockSpec((None, tk, n), lambda i, k: (layer, k, 0)),
                  pl.BlockSpec((tm, n), lambda i, k: (i, 0), pipeline_mode=once),
                  pl.BlockSpec((1, n), lambda i, k: (0, 0)),
                  pl.BlockSpec((1, n), lambda i, k: (0, 0))],
        out_specs=(pl.BlockSpec((tm, n), lambda i, k: (i, 0), pipeline_mode=once),
                   pl.BlockSpec((tm, n), lambda i, k: (i, 0), pipeline_mode=once)),
        scratch_shapes=[pltpu.VMEM((tm, n), F32)],
        compiler_params=_cparams(("parallel", "arbitrary")),
        name="ffn_down_ln",
    )(h, w, x, lw, lb)


def _ple_kernel(xb_ref, wg_ref, p_ref, wp_ref, x_ref, o32_ref, o16_ref, wgb_ref):
    @pl.when(pl.program_id(1) == 0)
    def _():
        wgb_ref[...] = wg_ref[...].astype(BF16)

    wp = wp_ref[...].astype(BF16)
    for s in range(xb_ref.shape[0] // LN_SUBROWS):
        rows = slice(s * LN_SUBROWS, (s + 1) * LN_SUBROWS)
        gate = _sigmoid(_mm(xb_ref[rows, :], wgb_ref[...]))
        y = x_ref[rows, :] + gate * _mm(p_ref[rows, :], wp)
        o32_ref[rows, :] = y
        o16_ref[rows, :] = y.astype(BF16)


def _ple(xb, x, p, wg, wp, layer, tm=512, tn=1024):
    m, k = xb.shape
    n = wg.shape[2]
    kp = p.shape[2]
    return pl.pallas_call(
        _ple_kernel,
        out_shape=(jax.ShapeDtypeStruct((m, n), F32), jax.ShapeDtypeStruct((m, n), BF16)),
        grid=(n // tn, m // tm),
        in_specs=[pl.BlockSpec((tm, k), lambda j, i: (i, 0)),
                  pl.BlockSpec((None, k, tn), lambda j, i: (layer, 0, j)),
                  pl.BlockSpec((None, tm, kp), lambda j, i: (layer, i, 0)),
                  pl.BlockSpec((None, kp, tn), lambda j, i: (layer, 0, j)),
                  pl.BlockSpec((tm, tn), lambda j, i: (i, j))],
        out_specs=(pl.BlockSpec((tm, tn), lambda j, i: (i, j)),
                   pl.BlockSpec((tm, tn), lambda j, i: (i, j))),
        scratch_shapes=[pltpu.VMEM((k, tn), BF16)],
        compiler_params=_cparams(("parallel", "arbitrary")),
        name="ple_gate",
    )(xb, wg, p, wp, x)


def _chunk_tri(tb):
    rb = _iota2((tb, tb), 0)
    cb = _iota2((tb, tb), 1)
    return ((cb <= rb) & ((cb // CHUNK) == (rb // CHUNK))).astype(F32)


def _gla_block(q, k, g, v, st_ref, heads, dk, dv):
    tb = q.shape[0]
    c = CHUNK
    causal = _iota2((c, c), 1) <= _iota2((c, c), 0)
    gc_all = _mm_sel(_chunk_tri(tb), g)
    nchunk = tb // c
    items = [(h, ci) for h in range(heads) for ci in range(nchunk)]

    def part(x, h, ci, d):
        return x[ci * c:(ci + 1) * c, h * d:(h + 1) * d]

    gcs = [part(gc_all, h, ci, dk) for h, ci in items]
    scores = [_mm(part(q, h, ci, dk) * jnp.exp(gc - gc[c // 2:c // 2 + 1]),
                  part(k, h, ci, dk) * jnp.exp(gc[c // 2:c // 2 + 1] - gc), NT)
              for (h, ci), gc in zip(items, gcs)]
    intra = [_mm(jnp.where(causal, a, 0.0), part(v, h, ci, dv)) for (h, ci), a in zip(items, scores)]
    incs = [_mm(part(v, h, ci, dv), part(k, h, ci, dk) * jnp.exp(gc[c - 1:c] - gc), TN)
            for (h, ci), gc in zip(items, gcs)]
    qgs = [(part(q, h, ci, dk) * jnp.exp(gc)).astype(BF16) for (h, ci), gc in zip(items, gcs)]
    sts = [st_ref[h] for h in range(heads)]
    o_h = [[] for _ in range(heads)]
    for ci in range(nchunk):
        for h in range(heads):
            i = h * nchunk + ci
            o_h[h].append(intra[i] + _mm(qgs[i], sts[h], NT))
            sts[h] = sts[h] * jnp.exp(gcs[i][c - 1:c]) + incs[i]
    for h in range(heads):
        st_ref[h] = sts[h]
    return [jnp.concatenate(o, axis=0) for o in o_h]


def _head_rms(o, gain, gate):
    o = o * lax.rsqrt(jnp.mean(o * o, axis=-1, keepdims=True) + RMS_EPS)
    return o * gain * _silu(gate)


def _hgrn2_kernel(q_ref, f_ref, i_ref, g_ref, lg_ref, nw_ref, o_ref, st_ref, *, layer, heads, nchunk):
    @pl.when(pl.program_id(2) == 0)
    def _():
        st_ref[...] = jnp.zeros_like(st_ref)

    lg = lg_ref[...]
    e = jnp.exp(lg - jnp.max(lg, axis=0, keepdims=True))
    lb = jnp.sum(e[0:layer + 1], axis=0, keepdims=True) / jnp.sum(e, axis=0, keepdims=True)

    f = lb + (1.0 - lb) * _sigmoid(f_ref[0])
    outs = _gla_block(_silu(q_ref[0]), 1.0 - f, jnp.log(f), i_ref[0], st_ref, heads, LANES, LANES)
    for h in range(heads):
        cols = slice(h * LANES, (h + 1) * LANES)
        o_ref[0, :, cols] = _head_rms(outs[h], nw_ref[:, cols], g_ref[0, :, cols]).astype(o_ref.dtype)


def _hgrn2(proj, logits, norm_w, layer, tb=256, heads=8):
    b, t, _ = proj.shape
    w = LANES * heads
    per = GROUP_W // w

    def col(section):
        return lambda bi, hi, ti: (bi, ti, section * per + hi)

    return pl.pallas_call(
        functools.partial(_hgrn2_kernel, layer=layer, heads=heads, nchunk=tb // CHUNK),
        out_shape=jax.ShapeDtypeStruct((b, t, GROUP_W), BF16),
        grid=(b, per, t // tb),
        in_specs=[pl.BlockSpec((1, tb, w), col(0)), pl.BlockSpec((1, tb, w), col(1)),
                  pl.BlockSpec((1, tb, w), col(2)), pl.BlockSpec((1, tb, w), col(3)),
                  pl.BlockSpec((logits.shape[0], w), lambda bi, hi, ti: (0, hi)),
                  pl.BlockSpec((1, w), lambda bi, hi, ti: (0, hi))],
        out_specs=pl.BlockSpec((1, tb, w), lambda bi, hi, ti: (bi, ti, hi)),
        scratch_shapes=[pltpu.VMEM((heads, LANES, LANES), F32)],
        compiler_params=_cparams(("parallel", "parallel", "arbitrary")),
        name="hgrn2_chunk",
    )(proj, proj, proj, proj, logits, norm_w)


def _gla_kernel(q_ref, k_ref, v_ref, r_ref, gd_ref, w2_ref, b_ref, nw_ref, o_ref, st_ref, *, heads):
    @pl.when(pl.program_id(2) == 0)
    def _():
        st_ref[...] = jnp.zeros_like(st_ref)

    z = _mm_hi(gd_ref[0], w2_ref[...]) + b_ref[...]
    g = -_softplus(-z) * (1.0 / GLA_TAU)
    outs = _gla_block(q_ref[0] * GLA_DK ** -0.5, k_ref[0], g, v_ref[0], st_ref, heads, GLA_DK, GLA_DV)
    for h in range(heads):
        cols = slice(h * GLA_DV, (h + 1) * GLA_DV)
        o_ref[0, :, cols] = _head_rms(outs[h], nw_ref[:, cols], r_ref[0, :, cols]).astype(o_ref.dtype)


def _gla(proj, tail, w2, bias, norm_w, tb=256, heads=4):
    b, t, _ = proj.shape
    wk = GLA_DK * heads
    wv = GLA_DV * heads
    groups = GLA_HEADS // heads
    qk0 = 4 * GROUP_W // wk
    v0 = (4 * GROUP_W + 2 * GLA_HEADS * GLA_DK) // wv
    gd_blk = GROUP_W // LANES
    return pl.pallas_call(
        functools.partial(_gla_kernel, heads=heads),
        out_shape=jax.ShapeDtypeStruct((b, t, GROUP_W), BF16),
        grid=(b, groups, t // tb),
        in_specs=[pl.BlockSpec((1, tb, wk), lambda bi, hi, ti: (bi, ti, qk0 + hi)),
                  pl.BlockSpec((1, tb, wk), lambda bi, hi, ti: (bi, ti, qk0 + groups + hi)),
                  pl.BlockSpec((1, tb, wv), lambda bi, hi, ti: (bi, ti, v0 + hi)),
                  pl.BlockSpec((1, tb, wv), lambda bi, hi, ti: (bi, ti, hi)),
                  pl.BlockSpec((1, tb, LANES), lambda bi, hi, ti: (bi, ti, gd_blk)),
                  pl.BlockSpec((LANES, wk), lambda bi, hi, ti: (0, hi)),
                  pl.BlockSpec((1, wk), lambda bi, hi, ti: (0, hi)),
                  pl.BlockSpec((1, wv), lambda bi, hi, ti: (0, hi))],
        out_specs=pl.BlockSpec((1, tb, wv), lambda bi, hi, ti: (bi, ti, hi)),
        scratch_shapes=[pltpu.VMEM((heads, GLA_DV, GLA_DK), F32)],
        compiler_params=_cparams(("parallel", "parallel", "arbitrary")),
        name="gla_chunk",
    )(proj, proj, proj, tail, tail, w2, bias, norm_w)


def _neumann_inverses(mats):
    n = mats[0].shape[0]
    pack = 2 if (2 * n <= LANES and len(mats) % 2 == 0) else 1
    width = pack * n
    eye = (_iota2((n, width), 0) == _iota2((n, width), 1) % n).astype(F32)
    left = _iota2((n, width), 1) < n

    def rhs(x):
        if pack == 1:
            return x
        return jnp.concatenate([jnp.where(left, x, 0.0), jnp.where(left, 0.0, x)], axis=0)

    if pack == 2:
        mats = [jnp.concatenate([mats[i], mats[i + 1]], axis=1) for i in range(0, len(mats), 2)]
    ts = [eye + a for a in mats]
    ps = [_mm(a, rhs(a)) for a in mats]
    steps = max(1, (CHUNK - 1).bit_length() - 1)
    for _ in range(steps - 1):
        both = [_mm(jnp.concatenate([t, p], axis=0), rhs(p)) for t, p in zip(ts, ps)]
        ts = [t + b[0:n] for t, b in zip(ts, both)]
        ps = [b[n:2 * n] for b in both]
    ts = [t + _mm(t, rhs(p)) for t, p in zip(ts, ps)]
    if pack == 1:
        return ts
    out = []
    for t in ts:
        out += [t[:, 0:n], t[:, n:2 * n]]
    return out


def _shifted(x, carry_ref, shift):
    tb = x.shape[0]
    ext = jnp.concatenate([carry_ref[...], x], axis=0)
    return pltpu.roll(ext, shift, 0)[SUBLANES:SUBLANES + tb]


def _gdn_kernel(q_ref, k_ref, v_ref, z_ref, ab_ref, cq_ref, ck_ref, cv_ref, al_ref, dt_ref, nw_ref,
                o_ref, st_ref, cq_s, ck_s, cv_s, *, nchunk, heads):
    head0 = pl.program_id(1) * heads

    @pl.when(pl.program_id(2) == 0)
    def _():
        st_ref[...] = jnp.zeros_like(st_ref)
        cq_s[...] = jnp.zeros_like(cq_s)
        ck_s[...] = jnp.zeros_like(ck_s)
        cv_s[...] = jnp.zeros_like(cv_s)

    tb = q_ref.shape[1]

    def conv_silu(x_ref, carry, w_ref):
        x = x_ref[0]
        w = w_ref[...]
        y = x * w[3:4]
        for j in (1, 2, 3):
            y = y + _shifted(x, carry, j) * w[3 - j:4 - j]
        carry[...] = x[tb - SUBLANES:tb]
        return _silu(y)

    def l2n(x):
        return x * lax.rsqrt(jnp.sum(x * x, axis=-1, keepdims=True) + L2_EPS)

    q_all = conv_silu(q_ref, cq_s, cq_ref)
    k_all = conv_silu(k_ref, ck_s, ck_ref)
    v_all = conv_silu(v_ref, cv_s, cv_ref)

    c = CHUNK
    incl = _iota2((c, c), 1) <= _iota2((c, c), 0)
    strict = _iota2((c, c), 1) < _iota2((c, c), 0)
    rb = _iota2((tb, tb), 0)
    cb = _iota2((tb, tb), 1)
    chunk_tri = ((cb <= rb) & ((cb // c) == (rb // c))).astype(F32)
    lane = _iota2((tb, LANES), 1)
    ab = ab_ref[0]
    gmat = -jnp.exp(al_ref[...]) * _softplus(ab + dt_ref[...])
    gcum = _mm_sel(chunk_tri, gmat)
    items = [(h, ci) for h in range(heads) for ci in range(nchunk)]
    per_head = []
    for h in range(heads):
        cols = slice(h * GDN_HEAD, (h + 1) * GDN_HEAD)
        q = l2n(q_all[:, cols]) * GDN_HEAD ** -0.5
        k = l2n(k_all[:, cols])
        g_col = jnp.sum(jnp.where(lane == head0 + h, gcum, 0.0), axis=1, keepdims=True)
        beta = _sigmoid(jnp.sum(jnp.where(lane == head0 + h + GDN_HEADS, ab, 0.0), axis=1, keepdims=True))
        eg = jnp.exp(g_col)
        kb = k * beta
        per_head.append(dict(q=q, k=k, g_col=g_col, kb=kb, qg=q * eg,
                             rhs=jnp.concatenate([v_all[:, cols] * beta, kb * eg], axis=1),
                             pick=(_iota2((c, LANES), 1) == head0 + h).astype(F32)))

    def rows(name, h, ci):
        return per_head[h][name][ci * c:(ci + 1) * c]

    g_rows = [_mm_sel(per_head[h]["pick"], gcum[ci * c:(ci + 1) * c], NT) for h, ci in items]
    kks = [_mm(rows("kb", h, ci), rows("k", h, ci), NT) for h, ci in items]
    qks = [_mm(rows("q", h, ci), rows("k", h, ci), NT) for h, ci in items]
    gams = [jnp.where(incl, jnp.exp(jnp.where(incl, rows("g_col", h, ci) - gr, 0.0)), 0.0)
            for (h, ci), gr in zip(items, g_rows)]
    lows = [jnp.where(strict, kk * gam, 0.0) for kk, gam in zip(kks, gams)]
    aqks = [jnp.where(incl, qk * gam, 0.0).astype(BF16) for qk, gam in zip(qks, gams)]
    g_lasts = [rows("g_col", h, ci)[c - 1:c] for h, ci in items]
    kgts = [(rows("k", h, ci) * jnp.exp(gl - rows("g_col", h, ci))).T.astype(BF16)
            for (h, ci), gl in zip(items, g_lasts)]
    decays = [jnp.exp(gl) for gl in g_lasts]
    tinvs = _neumann_inverses([-low for low in lows])
    uws = [_mm(tinvs[i], per_head[h]["rhs"][ci * c:(ci + 1) * c]) for i, (h, ci) in enumerate(items)]
    ows = [_mm(aqks[i], uws[i]) for i in range(len(items))]
    kws = [_mm(kgts[i], uws[i]) for i in range(len(items))]
    o0s = [ow[:, 0:GDN_HEAD] for ow in ows]
    qps = [(per_head[h]["qg"][ci * c:(ci + 1) * c] - ows[i][:, GDN_HEAD:2 * GDN_HEAD]).astype(BF16)
           for i, (h, ci) in enumerate(items)]
    nts = [kw[:, 0:GDN_HEAD] for kw in kws]
    mts = [(-kw[:, GDN_HEAD:2 * GDN_HEAD]).astype(BF16) for kw in kws]

    states = [st_ref[h] for h in range(heads)]
    outs = [[] for _ in range(heads)]
    for ci in range(nchunk):
        for h in range(heads):
            i = h * nchunk + ci
            sb = states[h].astype(BF16)
            outs[h].append(o0s[i] + _mm(qps[i], sb))
            states[h] = states[h] * decays[i] + _mm(mts[i], sb) + nts[i]
    for h in range(heads):
        cols = slice(h * GDN_HEAD, (h + 1) * GDN_HEAD)
        st_ref[h] = states[h]
        o = jnp.concatenate(outs[h], axis=0)
        o_ref[0, :, cols] = _head_rms(o, nw_ref[:, cols], z_ref[0, :, cols]).astype(o_ref.dtype)


def _gdn(proj, small, conv_w, alog_pad, dt_pad, norm_w, tb=256, heads=8):
    b, t, _ = proj.shape
    w = GDN_HEAD * heads
    groups = GDN_HEADS // heads
    base = 0
    ab_blk = (small.shape[2] - LANES) // LANES

    def col(section):
        return lambda bi, hi, ti: (bi, ti, base + section * groups + hi)

    def cw(section):
        return pl.BlockSpec((4, w), lambda bi, hi, ti: (0, section * groups + hi))

    return pl.pallas_call(
        functools.partial(_gdn_kernel, nchunk=tb // CHUNK, heads=heads),
        out_shape=jax.ShapeDtypeStruct((b, t, GROUP_W), BF16),
        grid=(b, groups, t // tb),
        in_specs=[pl.BlockSpec((1, tb, w), col(0)), pl.BlockSpec((1, tb, w), col(1)),
                  pl.BlockSpec((1, tb, w), col(2)), pl.BlockSpec((1, tb, w), col(3)),
                  pl.BlockSpec((1, tb, LANES), lambda bi, hi, ti: (bi, ti, ab_blk)),
                  cw(0), cw(1), cw(2),
                  pl.BlockSpec((1, LANES), lambda bi, hi, ti: (0, 0)),
                  pl.BlockSpec((1, LANES), lambda bi, hi, ti: (0, 0)),
                  pl.BlockSpec((1, w), lambda bi, hi, ti: (0, hi))],
        out_specs=pl.BlockSpec((1, tb, w), lambda bi, hi, ti: (bi, ti, hi)),
        scratch_shapes=[pltpu.VMEM((heads, GDN_HEAD, GDN_HEAD), F32)]
        + [pltpu.VMEM((SUBLANES, w), F32)] * 3,
        compiler_params=_cparams(("parallel", "parallel", "arbitrary")),
        name="gdn_chunk",
    )(proj, proj, proj, proj, small, conv_w, conv_w, conv_w, alog_pad, dt_pad, norm_w)


def _rwkv_kernel(r_ref, k_ref, v_ref, sm_ref, mur_ref, muk_ref, muv_ref, mus_ref, w0_ref, w2_ref, a0_ref,
                 a2_ref, g2_ref, kk_ref, ka_ref, rk_ref, gw_ref, gb_ref, o_ref,
                 st_ref, cr_s, ck_s, cv_s, cs_s, *, nchunk, lora):
    @pl.when(pl.program_id(2) == 0)
    def _():
        for ref in (st_ref, cr_s, ck_s, cv_s, cs_s):
            ref[...] = jnp.zeros_like(ref)

    tb = r_ref.shape[1]
    half = RWKV_HEAD
    block_diag = (_iota2((LANES, LANES), 0) // half) == (_iota2((LANES, LANES), 1) // half)
    ones_bd = block_diag.astype(BF16)

    def seg_sum(x):
        hi = x.astype(BF16)
        lo = (x - hi.astype(F32)).astype(BF16)
        parts = []
        for j in range(x.shape[1] // LANES):
            cols = slice(j * LANES, (j + 1) * LANES)
            parts.append(lax.dot_general(hi[:, cols], ones_bd, NN, preferred_element_type=F32)
                         + lax.dot_general(lo[:, cols], ones_bd, NN, preferred_element_type=F32))
        return parts[0] if len(parts) == 1 else jnp.concatenate(parts, axis=1)

    def token_shift(x, carry, mu):
        prev = _shifted(x, carry, 1)
        carry[...] = x[tb - SUBLANES:tb]
        return x + (prev - x) * mu

    r = token_shift(r_ref[0], cr_s, mur_ref[...])
    k = token_shift(k_ref[0], ck_s, muk_ref[...])
    v = token_shift(v_ref[0], cv_s, muv_ref[...])
    sm = token_shift(sm_ref[0], cs_s, mus_ref[...])
    wd = sm[:, 0:lora]
    ad = sm[:, lora:2 * lora]
    gd = sm[:, 2 * lora:2 * lora + g2_ref.shape[0]]
    z = w0_ref[...] + _mm_hi(jnp.tanh(wd), w2_ref[...])
    lw = -jnp.exp(-_softplus(-z) - 0.5)
    ag = _sigmoid(a0_ref[...] + _mm_hi(ad, a2_ref[...]))
    gate = _mm(_sigmoid(gd), g2_ref[...])
    kk = k * kk_ref[...]
    kk = kk * lax.rsqrt(seg_sum(kk * kk) + L2_EPS)
    k2 = k * (1.0 + (ag - 1.0) * ka_ref[...])
    aa_all = -kk
    bb_all = kk * ag
    npair = r.shape[1] // LANES

    c = CHUNK
    m0 = _iota2((c, LANES), 1) < half
    row = _iota2((c, LANES), 0)
    sub = _iota2((c, LANES), 1) % c
    strict = sub < row
    incl = sub <= row
    rb = _iota2((tb, tb), 0)
    cb = _iota2((tb, tb), 1)
    chunk_tri = ((cb <= rb) & ((cb // c) == (rb // c))).astype(F32)
    gc_all = _mm_sel(chunk_tri, lw)
    gx_all = gc_all - lw

    def stack(x):
        return jnp.concatenate([jnp.where(m0, x, 0.0), jnp.where(m0, 0.0, x)], axis=0)

    def unstack(x):
        return jnp.where(m0, x[0:c], x[c:2 * c])

    items = [(pi, ci) for pi in range(npair) for ci in range(nchunk)]
    a_bds, ak_xs, wmats, a_sts, v_sws, r_abss, nvs, bes, decays = [], [], [], [], [], [], [], [], []
    for pi, ci in items:
        sl = (slice(ci * c, (ci + 1) * c), slice(pi * LANES, (pi + 1) * LANES))
        rr, kc, vc, aa, bb = r[sl], k2[sl], v[sl], aa_all[sl], bb_all[sl]
        gc, gx = gc_all[sl], gx_all[sl]
        g_mid = gc[c // 2 - 1:c // 2]
        g_last = gc[c - 1:c]
        e_out = jnp.exp(g_mid - gc)
        at = aa * jnp.exp(gx - g_mid)
        rt = rr * jnp.exp(gc - g_mid)
        bt = bb * e_out
        kt = kc * e_out
        lhs0 = jnp.concatenate([jnp.where(m0, at, 0.0), jnp.where(m0, rt, 0.0)], axis=0)
        lhs1 = jnp.concatenate([jnp.where(m0, 0.0, at), jnp.where(m0, 0.0, rt)], axis=0)
        res0 = _mm(lhs0, jnp.concatenate([bt, kt], axis=0), NT)
        res1 = _mm(lhs1, jnp.concatenate([kt, bt], axis=0), NT)
        top0 = jnp.where(strict, res0[0:c], 0.0)
        top1 = jnp.where(strict, res1[0:c], 0.0)
        a_bds.append(jnp.concatenate([jnp.where(m0, top0, 0.0), jnp.where(m0, 0.0, top1)], axis=0))
        ak_xs.append(jnp.concatenate([jnp.where(m0, 0.0, top0), jnp.where(m0, top1, 0.0)], axis=0))
        wmats.append(jnp.concatenate([jnp.where(incl, res0[c:2 * c], 0.0),
                                      jnp.where(incl, res1[c:2 * c], 0.0)], axis=0))
        a_sts.append(stack(aa * jnp.exp(gx)))
        v_sws.append(jnp.concatenate([jnp.where(m0, 0.0, vc), jnp.where(m0, vc, 0.0)], axis=0))
        r_abss.append(rr * jnp.exp(gc))
        e_end = jnp.exp(g_last - gc)
        bes.append(bb * e_end)
        nvs.append((vc, kc * e_end))
        decays.append(jnp.exp(g_last))
    tinvs = _neumann_inverses(a_bds)

    n_items = len(items)
    rhs_vs = [_mm(ak_xs[i], v_sws[i]) for i in range(n_items)]
    uws = [_mm(tinvs[i], jnp.concatenate([rhs_vs[i], a_sts[i]], axis=1)) for i in range(n_items)]
    yws = [_mm(wmats[i], jnp.concatenate([uws[i][:, 0:LANES] + v_sws[i], uws[i][:, LANES:2 * LANES]], axis=1))
           for i in range(n_items)]
    y0s = [unstack(yws[i][:, 0:LANES]) for i in range(n_items)]
    qps = [(unstack(yws[i][:, LANES:2 * LANES]) + r_abss[i]).astype(BF16) for i in range(n_items)]
    gbds = [jnp.where(block_diag, _mm(unstack(uws[i][:, LANES:2 * LANES]), bes[i], TN), 0.0).astype(BF16)
            for i in range(n_items)]
    ncs = [jnp.where(block_diag, _mm(jnp.concatenate([unstack(uws[i][:, 0:LANES]), nvs[i][0]], axis=0),
                                     jnp.concatenate([bes[i], nvs[i][1]], axis=0), TN), 0.0)
           for i in range(n_items)]

    hts = [st_ref[pi] for pi in range(npair)]
    ys = [[] for _ in range(npair)]
    for ci in range(nchunk):
        for pi in range(npair):
            i = pi * nchunk + ci
            htb = hts[pi].astype(BF16)
            ys[pi].append(y0s[i] + _mm(qps[i], htb, NT))
            hts[pi] = hts[pi] * decays[i] + _mm(htb, gbds[i]) + ncs[i]
    for pi in range(npair):
        st_ref[pi] = hts[pi]

    ycols = [jnp.concatenate(ys[pi], axis=0) for pi in range(npair)]
    y = ycols[0] if npair == 1 else jnp.concatenate(ycols, axis=1)
    mean = seg_sum(y) * (1.0 / half)
    yc = y - mean
    var = seg_sum(yc * yc) * (1.0 / half)
    yn = yc * lax.rsqrt(var + RWKV_GN_EPS) * gw_ref[...] + gb_ref[...]
    bonus = seg_sum(r * k2 * rk_ref[...]) * v
    o_ref[0] = ((yn + bonus) * gate).astype(o_ref.dtype)


def _rwkv(proj, small, mu_main, mu_small, w0, w2p, a0, a2p, g2, kk, ka, rk, gw, gb, lora, tb=256, npair=8):
    b, t, _ = proj.shape
    w = LANES * npair
    groups = GROUP_W // w
    ws = small.shape[2]
    glora = g2.shape[0]

    def col(section):
        return lambda bi, hi, ti: (bi, ti, section * groups + hi)

    def vec(section=0):
        return pl.BlockSpec((1, w), lambda bi, hi, ti: (0, section * groups + hi))

    row_blk = lambda s: pl.BlockSpec((1, tb, w), col(s))
    return pl.pallas_call(
        functools.partial(_rwkv_kernel, nchunk=tb // CHUNK, lora=lora),
        out_shape=jax.ShapeDtypeStruct((b, t, GROUP_W), BF16),
        grid=(b, groups, t // tb),
        in_specs=[row_blk(0), row_blk(1), row_blk(2),
                  pl.BlockSpec((1, tb, ws), lambda bi, hi, ti: (bi, ti, 0)),
                  vec(0), vec(1), vec(2),
                  pl.BlockSpec((1, ws), lambda bi, hi, ti: (0, 0)),
                  vec(), pl.BlockSpec((lora, w), lambda bi, hi, ti: (0, hi)),
                  vec(), pl.BlockSpec((lora, w), lambda bi, hi, ti: (0, hi)),
                  pl.BlockSpec((glora, w), lambda bi, hi, ti: (0, hi)),
                  vec(), vec(), vec(), vec(), vec()],
        out_specs=pl.BlockSpec((1, tb, w), lambda bi, hi, ti: (bi, ti, hi)),
        scratch_shapes=[pltpu.VMEM((npair, LANES, LANES), F32)]
        + [pltpu.VMEM((SUBLANES, w), F32)] * 3
        + [pltpu.VMEM((SUBLANES, ws), F32)],
        compiler_params=_cparams(("parallel", "parallel", "arbitrary")),
        name="rwkv7_chunk",
    )(proj, proj, proj, small, mu_main, mu_main, mu_main, mu_small, w0, w2p, a0, a2p, g2,
      kk, ka, rk, gw, gb)


def _pad_cols(a, width):
    return jnp.pad(a, ((0, 0), (0, width - a.shape[1])))


def _pad_rows(a, height):
    return jnp.pad(a, ((0, height - a.shape[0]), (0, 0)))


def _row(v):
    return v.reshape(1, -1).astype(F32)


def kernel(x, p, hgrn_lb_logits, e_w_in, e_gla_w2, e_gla_b, e_hgrn_norm, e_gla_norm, e_w_out, o_w_in, o_rwkv_mu, o_rwkv_w0, o_rwkv_w2, o_rwkv_a0, o_rwkv_a2, o_rwkv_g2, o_rwkv_kk, o_rwkv_ka, o_rwkv_rk, o_rwkv_gn_w, o_rwkv_gn_b, o_gdn_conv, o_gdn_a_log, o_gdn_dt_bias, o_gdn_norm, o_w_out, ln_mix_w, ln_mix_b, ln_ffn_w, ln_ffn_b, ffn_w_up, ffn_conv, ffn_w_down, ple_w_proj, ple_w_gate):
    bsz, seq, d = x.shape
    m = bsz * seq
    depth = ln_mix_w.shape[0]
    alpha = (2 * depth) ** 0.25
    gw = GROUP_W
    x32 = x.reshape(m, d)
    xb = x32.astype(BF16)
    w_down = ffn_w_down.astype(BF16)
    for layer in range(depth):
        j = layer // 2
        if layer % 2 == 0:
            rank = e_gla_w2.shape[1]
            gd0 = 4 * gw + 2 * GLA_HEADS * GLA_DK + gw
            wt = jnp.swapaxes(e_w_in, 1, 2)
            w_tail = jnp.concatenate([wt[j, gd0 + rank:], _pad_rows(wt[j, gd0:gd0 + rank], LANES)],
                                     axis=0)
            proj = _proj(xb, wt, gd0, layer=j).reshape(bsz, seq, -1)
            tail = _proj(xb, w_tail, w_tail.shape[0], tn=w_tail.shape[0]).reshape(bsz, seq, -1)
            o_a = _hgrn2(proj, hgrn_lb_logits.astype(F32), _row(e_hgrn_norm[j]), layer)
            o_b = _gla(proj, tail, _pad_rows(e_gla_w2[j], LANES), _row(e_gla_b[j]), _row(e_gla_norm[j]))
            w_out = e_w_out[j]
        else:
            wt = jnp.swapaxes(o_w_in, 1, 2)
            lora_w = o_rwkv_w2.shape[1]
            lora_a = o_rwkv_a2.shape[1]
            lora_g = o_rwkv_g2.shape[1]
            lora = LANES * (-(-max(lora_w, lora_a) // LANES))
            c_wd = 3 * gw
            c_ad = c_wd + lora_w
            c_gd = c_ad + lora_a
            c_qkv = c_gd + lora_g
            c_z = c_qkv + 3 * gw
            c_ab = c_z + gw
            w_small = jnp.concatenate([_pad_rows(wt[j, c_wd:c_ad], lora), _pad_rows(wt[j, c_ad:c_gd], lora),
                                       wt[j, c_gd:c_qkv], _pad_rows(wt[j, c_ab:], LANES)], axis=0)
            mu = o_rwkv_mu[j].reshape(1, -1)
            mu_small = jnp.concatenate([_pad_cols(mu[:, c_wd:c_ad], lora), _pad_cols(mu[:, c_ad:c_gd], lora),
                                        mu[:, c_gd:c_qkv], jnp.zeros((1, LANES), F32)], axis=1)
            proj_rkv = _proj(xb, wt, c_wd, layer=j).reshape(bsz, seq, -1)
            proj_gdn = _proj(xb, wt[j, c_qkv:c_ab], c_ab - c_qkv).reshape(bsz, seq, -1)
            small = _proj(xb, w_small, w_small.shape[0], tn=w_small.shape[0]).reshape(bsz, seq, -1)
            o_a = _rwkv(proj_rkv, small, mu[:, :c_wd], mu_small, _row(o_rwkv_w0[j]),
                        _pad_rows(o_rwkv_w2[j], lora), _row(o_rwkv_a0[j]), _pad_rows(o_rwkv_a2[j], lora),
                        o_rwkv_g2[j], _row(o_rwkv_kk[j]), _row(o_rwkv_ka[j]), _row(o_rwkv_rk[j]),
                        _row(o_rwkv_gn_w[j]), _row(o_rwkv_gn_b[j]), lora)
            o_b = _gdn(proj_gdn, small, o_gdn_conv[j].astype(F32),
                       _pad_cols(_row(o_gdn_a_log[j]), LANES), _pad_cols(_row(o_gdn_dt_bias[j]), LANES),
                       _row(o_gdn_norm[j]))
            w_out = o_w_out[j]
        x32, xb = _mix_out(o_a.reshape(m, gw), o_b.reshape(m, gw), w_out.astype(BF16), x32,
                           _row(ln_mix_w[layer]), _row(ln_mix_b[layer]), alpha)
        h = _ffn_up(xb, ffn_w_up, ffn_conv, layer, seq)
        x32, xb = _ffn_down(h, w_down, x32, _row(ln_ffn_w[layer]), _row(ln_ffn_b[layer]), alpha, layer)
        x32, xb = _ple(xb, x32, p.reshape(depth, m, -1), ple_w_gate, ple_w_proj, layer)
    return x32.reshape(bsz, seq, d)
```

```python
import functools

import jax
import jax.numpy as jnp
from jax import lax
from jax.experimental import pallas as pl
from jax.experimental.pallas import tpu as pltpu

F32 = jnp.float32
BF16 = jnp.bfloat16

NN = (((1,), (0,)), ((), ()))
NT = (((1,), (1,)), ((), ()))
TN = (((0,), (0,)), ((), ()))

D_MODEL = 2048
GROUP_W = 1024
CHUNK = 64
LANES = 128
SUBLANES = 8
HGRN_HEADS = 8
GLA_HEADS = 4
GLA_DK = 128
GLA_DV = 256
GLA_TAU = 16.0
RWKV_HEAD = 64
RWKV_GN_EPS = 64e-5
GDN_HEADS = 8
GDN_HEAD = 128
FFN_DIM = 5632
LN_EPS = 1e-5
LN_SUBROWS = 128
RMS_EPS = 1e-6
L2_EPS = 1e-6

VMEM_LIMIT = 60 * 1024 * 1024


def _mm(a, b, dims=NN):
    return lax.dot_general(a.astype(BF16), b.astype(BF16), dims, preferred_element_type=F32)


def _split_bf16(x, parts):
    out = []
    for _ in range(parts - 1):
        h = x.astype(BF16)
        out.append(h)
        x = x - h.astype(F32)
    out.append(x.astype(BF16))
    return out


def _bdot(a, b, dims):
    return lax.dot_general(a, b, dims, preferred_element_type=F32)


def _mm_sel(sel, b, dims=NN):
    s = sel.astype(BF16)
    b1, b2, b3 = _split_bf16(b, 3)
    return (_bdot(s, b3, dims) + _bdot(s, b2, dims)) + _bdot(s, b1, dims)


def _mm_hi(a, b, dims=NN):
    ah, al = _split_bf16(a, 2)
    bh, bl = _split_bf16(b, 2)
    return (_bdot(al, bh, dims) + _bdot(ah, bl, dims)) + _bdot(ah, bh, dims)


def _sigmoid(x):
    return 1.0 / (1.0 + jnp.exp(-x))


def _silu(x):
    return x * _sigmoid(x)


def _softplus(x):
    return jnp.maximum(x, 0.0) + jnp.log(1.0 + jnp.exp(-jnp.abs(x)))


def _iota2(shape, dim):
    return lax.broadcasted_iota(jnp.int32, shape, dim)


def _cparams(sem):
    return pltpu.CompilerParams(dimension_semantics=sem, vmem_limit_bytes=VMEM_LIMIT)


def _proj_kernel(x_ref, w_ref, o_ref, wb_ref):
    @pl.when(pl.program_id(1) == 0)
    def _():
        w = w_ref[...] if len(w_ref.shape) == 2 else w_ref[0]
        wb_ref[...] = w.T.astype(BF16)

    o_ref[...] = lax.dot_general(x_ref[...], wb_ref[...], NN, preferred_element_type=F32).astype(o_ref.dtype)


def _proj(x, wt, n_cols, out_dtype=F32, layer=None, row0=0, tm=1024, tn=1024):
    m, k = x.shape
    tn = min(tn, n_cols)
    if layer is None:
        w_spec = pl.BlockSpec((tn, k), lambda j, i: (j, 0))
    elif row0 % tn == 0:
        w_spec = pl.BlockSpec((None, tn, k), lambda j, i: (layer, row0 // tn + j, 0))
    else:
        w_spec = pl.BlockSpec((pl.Element(1), pl.Element(tn), pl.Element(k)),
                              lambda j, i: (layer, pl.multiple_of(row0 + j * tn, SUBLANES), 0))
    return pl.pallas_call(
        _proj_kernel,
        out_shape=jax.ShapeDtypeStruct((m, n_cols), out_dtype),
        grid=(n_cols // tn, m // tm),
        in_specs=[pl.BlockSpec((tm, k), lambda j, i: (i, 0)), w_spec],
        out_specs=pl.BlockSpec((tm, tn), lambda j, i: (i, j)),
        scratch_shapes=[pltpu.VMEM((k, tn), BF16)],
        compiler_params=_cparams(("parallel", "arbitrary")),
        name="proj_matmul",
    )(x, wt)


def _layer_norm_rows(y, w, b):
    mu = jnp.mean(y, axis=-1, keepdims=True)
    yc = y - mu
    var = jnp.mean(yc * yc, axis=-1, keepdims=True)
    return yc * lax.rsqrt(var + LN_EPS) * w + b


def _mix_out_kernel(oa_ref, ob_ref, w_ref, x_ref, lw_ref, lb_ref, o32_ref, o16_ref, *, alpha):
    half = oa_ref.shape[1]
    for s in range(oa_ref.shape[0] // LN_SUBROWS):
        rows = slice(s * LN_SUBROWS, (s + 1) * LN_SUBROWS)
        acc = _mm(oa_ref[rows, :], w_ref[0:half, :]) + _mm(ob_ref[rows, :], w_ref[half:2 * half, :])
        y = _layer_norm_rows(alpha * x_ref[rows, :] + acc, lw_ref[...], lb_ref[...])
        o32_ref[rows, :] = y
        o16_ref[rows, :] = y.astype(BF16)


def _mix_out(oa, ob, w, x, lw, lb, alpha, tm=512):
    m, half = oa.shape
    n = w.shape[1]
    row = lambda i: (i, 0)
    fixed = lambda i: (0, 0)
    return pl.pallas_call(
        functools.partial(_mix_out_kernel, alpha=alpha),
        out_shape=(jax.ShapeDtypeStruct((m, n), F32), jax.ShapeDtypeStruct((m, n), BF16)),
        grid=(m // tm,),
        in_specs=[pl.BlockSpec((tm, half), row), pl.BlockSpec((tm, half), row),
                  pl.BlockSpec((2 * half, n), fixed), pl.BlockSpec((tm, n), row),
                  pl.BlockSpec((1, n), fixed), pl.BlockSpec((1, n), fixed)],
        out_specs=(pl.BlockSpec((tm, n), row), pl.BlockSpec((tm, n), row)),
        compiler_params=_cparams(("parallel",)),
        name="mix_out_ln",
    )(oa, ob, w, x, lw, lb)


def _ffn_up_kernel(x_ref, wg_ref, wv_ref, cg_ref, cv_ref, o_ref, wgb_ref, wvb_ref, hg_ref, hv_ref,
                   *, tiles_per_seq):
    i = pl.program_id(1)
    tm = x_ref.shape[0]

    @pl.when(i == 0)
    def _():
        wgb_ref[...] = wg_ref[...].astype(BF16)
        wvb_ref[...] = wv_ref[...].astype(BF16)
        hg_ref[...] = jnp.zeros_like(hg_ref)
        hv_ref[...] = jnp.zeros_like(hv_ref)

    first = (i % tiles_per_seq) == 0

    def conv(w_ref, h_ref, c_ref):
        u = lax.dot_general(x_ref[...], w_ref[...], NN, preferred_element_type=F32)
        halo = jnp.where(first, 0.0, h_ref[...])
        h_ref[...] = u[tm - SUBLANES:tm]
        ext = jnp.concatenate([halo, u], axis=0)
        p1 = pltpu.roll(ext, 1, 0)[SUBLANES:SUBLANES + tm]
        p2 = pltpu.roll(ext, 2, 0)[SUBLANES:SUBLANES + tm]
        c = c_ref[...]
        return u * c[2:3] + p1 * c[1:2] + p2 * c[0:1]

    g = conv(wgb_ref, hg_ref, cg_ref)
    v = conv(wvb_ref, hv_ref, cv_ref)
    o_ref[...] = (_silu(g) * v).astype(o_ref.dtype)


def _ffn_up(xb, w_up, conv_w, layer, seq, tm=1024, tf=512):
    m, k = xb.shape
    f = w_up.shape[2] // 2
    nf = f // tf
    return pl.pallas_call(
        functools.partial(_ffn_up_kernel, tiles_per_seq=seq // tm),
        out_shape=jax.ShapeDtypeStruct((m, f), BF16),
        grid=(nf, m // tm),
        in_specs=[pl.BlockSpec((tm, k), lambda j, i: (i, 0)),
                  pl.BlockSpec((None, k, tf), lambda j, i: (layer, 0, j)),
                  pl.BlockSpec((None, k, tf), lambda j, i: (layer, 0, nf + j)),
                  pl.BlockSpec((None, 3, tf), lambda j, i: (layer, 0, j)),
                  pl.BlockSpec((None, 3, tf), lambda j, i: (layer, 0, nf + j))],
        out_specs=pl.BlockSpec((tm, tf), lambda j, i: (i, j)),
        scratch_shapes=[pltpu.VMEM((k, tf), BF16)] * 2 + [pltpu.VMEM((SUBLANES, tf), F32)] * 2,
        compiler_params=_cparams(("parallel", "arbitrary")),
        name="ffn_up_conv_gate",
    )(xb, w_up, w_up, conv_w, conv_w)


def _ffn_down_kernel(h_ref, w_ref, x_ref, lw_ref, lb_ref, o32_ref, o16_ref, acc_ref, *, alpha, nk):
    k = pl.program_id(1)

    @pl.when(k == 0)
    def _():
        acc_ref[...] = _mm(h_ref[...], w_ref[...])

    @pl.when((k > 0) & (k < nk - 1))
    def _():
        acc_ref[...] += _mm(h_ref[...], w_ref[...])

    @pl.when(k == nk - 1)
    def _():
        for s in range(h_ref.shape[0] // LN_SUBROWS):
            rows = slice(s * LN_SUBROWS, (s + 1) * LN_SUBROWS)
            acc = acc_ref[rows, :] + _mm(h_ref[rows, :], w_ref[...])
            y = _layer_norm_rows(alpha * x_ref[rows, :] + acc, lw_ref[...], lb_ref[...])
            o32_ref[rows, :] = y
            o16_ref[rows, :] = y.astype(BF16)


def _ffn_down(h, w, x, lw, lb, alpha, layer, tm=1024, tk=512):
    m, kdim = h.shape
    n = w.shape[2]
    nk = kdim // tk
    assert nk >= 2
    return pl.pallas_call(
        functools.partial(_ffn_down_kernel, alpha=alpha, nk=nk),
        out_shape=(jax.ShapeDtypeStruct((m, n), F32), jax.ShapeDtypeStruct((m, n), BF16)),
        grid=(m // tm, nk),
        in_specs=[pl.BlockSpec((tm, tk), lambda i, k: (i, k)),
                  pl.BlockSpec((None, tk, n), lambda i, k: (layer, k, 0)),
                  pl.BlockSpec((tm, n), lambda i, k: (i, 0)),
                  pl.BlockSpec((1, n), lambda i, k: (0, 0)),
                  pl.BlockSpec((1, n), lambda i, k: (0, 0))],
        out_specs=(pl.BlockSpec((tm, n), lambda i, k: (i, 0)),
                   pl.BlockSpec((tm, n), lambda i, k: (i, 0))),
        scratch_shapes=[pltpu.VMEM((tm, n), F32)],
        compiler_params=_cparams(("parallel", "arbitrary")),
        name="ffn_down_ln",
    )(h, w, x, lw, lb)


def _ple_kernel(xb_ref, wg_ref, p_ref, wp_ref, x_ref, o32_ref, o16_ref, wgb_ref):
    @pl.when(pl.program_id(1) == 0)
    def _():
        wgb_ref[...] = wg_ref[...].astype(BF16)

    wp = wp_ref[...].astype(BF16)
    for s in range(xb_ref.shape[0] // LN_SUBROWS):
        rows = slice(s * LN_SUBROWS, (s + 1) * LN_SUBROWS)
        gate = _sigmoid(_mm(xb_ref[rows, :], wgb_ref[...]))
        y = x_ref[rows, :] + gate * _mm(p_ref[rows, :], wp)
        o32_ref[rows, :] = y
        o16_ref[rows, :] = y.astype(BF16)


def _ple(xb, x, p, wg, wp, layer, tm=512, tn=1024):
    m, k = xb.shape
    n = wg.shape[2]
    kp = p.shape[2]
    return pl.pallas_call(
        _ple_kernel,
        out_shape=(jax.ShapeDtypeStruct((m, n), F32), jax.ShapeDtypeStruct((m, n), BF16)),
        grid=(n // tn, m // tm),
        in_specs=[pl.BlockSpec((tm, k), lambda j, i: (i, 0)),
                  pl.BlockSpec((None, k, tn), lambda j, i: (layer, 0, j)),
                  pl.BlockSpec((None, tm, kp), lambda j, i: (layer, i, 0)),
                  pl.BlockSpec((None, kp, tn), lambda j, i: (layer, 0, j)),
                  pl.BlockSpec((tm, tn), lambda j, i: (i, j))],
        out_specs=(pl.BlockSpec((tm, tn), lambda j, i: (i, j)),
                   pl.BlockSpec((tm, tn), lambda j, i: (i, j))),
        scratch_shapes=[pltpu.VMEM((k, tn), BF16)],
        compiler_params=_cparams(("parallel", "arbitrary")),
        name="ple_gate",
    )(xb, wg, p, wp, x)


def _chunk_tri(tb):
    rb = _iota2((tb, tb), 0)
    cb = _iota2((tb, tb), 1)
    return ((cb <= rb) & ((cb // CHUNK) == (rb // CHUNK))).astype(F32)


def _gla_block(q, k, g, v, st_ref, heads, dk, dv):
    tb = q.shape[0]
    c = CHUNK
    causal = _iota2((c, c), 1) <= _iota2((c, c), 0)
    gc_all = _mm_sel(_chunk_tri(tb), g)
    nchunk = tb // c
    items = [(h, ci) for h in range(heads) for ci in range(nchunk)]

    def part(x, h, ci, d):
        return x[ci * c:(ci + 1) * c, h * d:(h + 1) * d]

    gcs = [part(gc_all, h, ci, dk) for h, ci in items]
    scores = [_mm(part(q, h, ci, dk) * jnp.exp(gc - gc[c // 2:c // 2 + 1]),
                  part(k, h, ci, dk) * jnp.exp(gc[c // 2:c // 2 + 1] - gc), NT)
              for (h, ci), gc in zip(items, gcs)]
    intra = [_mm(jnp.where(causal, a, 0.0), part(v, h, ci, dv)) for (h, ci), a in zip(items, scores)]
    incs = [_mm(part(v, h, ci, dv), part(k, h, ci, dk) * jnp.exp(gc[c - 1:c] - gc), TN)
            for (h, ci), gc in zip(items, gcs)]
    qgs = [(part(q, h, ci, dk) * jnp.exp(gc)).astype(BF16) for (h, ci), gc in zip(items, gcs)]
    sts = [st_ref[h] for h in range(heads)]
    o_h = [[] for _ in range(heads)]
    for ci in range(nchunk):
        for h in range(heads):
            i = h * nchunk + ci
            o_h[h].append(intra[i] + _mm(qgs[i], sts[h], NT))
            sts[h] = sts[h] * jnp.exp(gcs[i][c - 1:c]) + incs[i]
    for h in range(heads):
        st_ref[h] = sts[h]
    return [jnp.concatenate(o, axis=0) for o in o_h]


def _head_rms(o, gain, gate):
    o = o * lax.rsqrt(jnp.mean(o * o, axis=-1, keepdims=True) + RMS_EPS)
    return o * gain * _silu(gate)


def _hgrn2_kernel(q_ref, f_ref, i_ref, g_ref, lg_ref, nw_ref, o_ref, st_ref, *, layer, heads, nchunk):
    @pl.when(pl.program_id(2) == 0)
    def _():
        st_ref[...] = jnp.zeros_like(st_ref)

    lg = lg_ref[...]
    e = jnp.exp(lg - jnp.max(lg, axis=0, keepdims=True))
    lb = jnp.sum(e[0:layer + 1], axis=0, keepdims=True) / jnp.sum(e, axis=0, keepdims=True)

    f = lb + (1.0 - lb) * _sigmoid(f_ref[0])
    outs = _gla_block(_silu(q_ref[0]), 1.0 - f, jnp.log(f), i_ref[0], st_ref, heads, LANES, LANES)
    for h in range(heads):
        cols = slice(h * LANES, (h + 1) * LANES)
        o_ref[0, :, cols] = _head_rms(outs[h], nw_ref[:, cols], g_ref[0, :, cols]).astype(o_ref.dtype)


def _hgrn2(proj, logits, norm_w, layer, tb=256, heads=8):
    b, t, _ = proj.shape
    w = LANES * heads
    per = GROUP_W // w

    def col(section):
        return lambda bi, hi, ti: (bi, ti, section * per + hi)

    return pl.pallas_call(
        functools.partial(_hgrn2_kernel, layer=layer, heads=heads, nchunk=tb // CHUNK),
        out_shape=jax.ShapeDtypeStruct((b, t, GROUP_W), BF16),
        grid=(b, per, t // tb),
        in_specs=[pl.BlockSpec((1, tb, w), col(0)), pl.BlockSpec((1, tb, w), col(1)),
                  pl.BlockSpec((1, tb, w), col(2)), pl.BlockSpec((1, tb, w), col(3)),
                  pl.BlockSpec((logits.shape[0], w), lambda bi, hi, ti: (0, hi)),
                  pl.BlockSpec((1, w), lambda bi, hi, ti: (0, hi))],
        out_specs=pl.BlockSpec((1, tb, w), lambda bi, hi, ti: (bi, ti, hi)),
        scratch_shapes=[pltpu.VMEM((heads, LANES, LANES), F32)],
        compiler_params=_cparams(("parallel", "parallel", "arbitrary")),
        name="hgrn2_chunk",
    )(proj, proj, proj, proj, logits, norm_w)


def _gla_kernel(q_ref, k_ref, v_ref, r_ref, gd_ref, w2_ref, b_ref, nw_ref, o_ref, st_ref, *, heads):
    @pl.when(pl.program_id(2) == 0)
    def _():
        st_ref[...] = jnp.zeros_like(st_ref)

    z = _mm_hi(gd_ref[0], w2_ref[...]) + b_ref[...]
    g = -_softplus(-z) * (1.0 / GLA_TAU)
    outs = _gla_block(q_ref[0] * GLA_DK ** -0.5, k_ref[0], g, v_ref[0], st_ref, heads, GLA_DK, GLA_DV)
    for h in range(heads):
        cols = slice(h * GLA_DV, (h + 1) * GLA_DV)
        o_ref[0, :, cols] = _head_rms(outs[h], nw_ref[:, cols], r_ref[0, :, cols]).astype(o_ref.dtype)


def _gla(proj, tail, w2, bias, norm_w, tb=256, heads=4):
    b, t, _ = proj.shape
    wk = GLA_DK * heads
    wv = GLA_DV * heads
    groups = GLA_HEADS // heads
    qk0 = 4 * GROUP_W // wk
    v0 = (4 * GROUP_W + 2 * GLA_HEADS * GLA_DK) // wv
    gd_blk = GROUP_W // LANES
    return pl.pallas_call(
        functools.partial(_gla_kernel, heads=heads),
        out_shape=jax.ShapeDtypeStruct((b, t, GROUP_W), BF16),
        grid=(b, groups, t // tb),
        in_specs=[pl.BlockSpec((1, tb, wk), lambda bi, hi, ti: (bi, ti, qk0 + hi)),
                  pl.BlockSpec((1, tb, wk), lambda bi, hi, ti: (bi, ti, qk0 + groups + hi)),
                  pl.BlockSpec((1, tb, wv), lambda bi, hi, ti: (bi, ti, v0 + hi)),
                  pl.BlockSpec((1, tb, wv), lambda bi, hi, ti: (bi, ti, hi)),
                  pl.BlockSpec((1, tb, LANES), lambda bi, hi, ti: (bi, ti, gd_blk)),
                  pl.BlockSpec((LANES, wk), lambda bi, hi, ti: (0, hi)),
                  pl.BlockSpec((1, wk), lambda bi, hi, ti: (0, hi)),
                  pl.BlockSpec((1, wv), lambda bi, hi, ti: (0, hi))],
        out_specs=pl.BlockSpec((1, tb, wv), lambda bi, hi, ti: (bi, ti, hi)),
        scratch_shapes=[pltpu.VMEM((heads, GLA_DV, GLA_DK), F32)],
        compiler_params=_cparams(("parallel", "parallel", "arbitrary")),
        name="gla_chunk",
    )(proj, proj, proj, tail, tail, w2, bias, norm_w)


def _neumann_inverses(mats):
    n = mats[0].shape[0]
    pack = 2 if (2 * n <= LANES and len(mats) % 2 == 0) else 1
    width = pack * n
    eye = (_iota2((n, width), 0) == _iota2((n, width), 1) % n).astype(F32)
    left = _iota2((n, width), 1) < n

    def rhs(x):
        if pack == 1:
            return x
        return jnp.concatenate([jnp.where(left, x, 0.0), jnp.where(left, 0.0, x)], axis=0)

    if pack == 2:
        mats = [jnp.concatenate([mats[i], mats[i + 1]], axis=1) for i in range(0, len(mats), 2)]
    ts = [eye + a for a in mats]
    ps = [_mm(a, rhs(a)) for a in mats]
    steps = max(1, (CHUNK - 1).bit_length() - 1)
    for _ in range(steps - 1):
        both = [_mm(jnp.concatenate([t, p], axis=0), rhs(p)) for t, p in zip(ts, ps)]
        ts = [t + b[0:n] for t, b in zip(ts, both)]
        ps = [b[n:2 * n] for b in both]
    ts = [t + _mm(t, rhs(p)) for t, p in zip(ts, ps)]
    if pack == 1:
        return ts
    out = []
    for t in ts:
        out += [t[:, 0:n], t[:, n:2 * n]]
    return out


def _shifted(x, carry_ref, shift):
    tb = x.shape[0]
    ext = jnp.concatenate([carry_ref[...], x], axis=0)
    return pltpu.roll(ext, shift, 0)[SUBLANES:SUBLANES + tb]


def _gdn_kernel(q_ref, k_ref, v_ref, z_ref, ab_ref, cq_ref, ck_ref, cv_ref, al_ref, dt_ref, nw_ref,
                o_ref, st_ref, cq_s, ck_s, cv_s, *, nchunk, heads):
    head0 = pl.program_id(1) * heads

    @pl.when(pl.program_id(2) == 0)
    def _():
        st_ref[...] = jnp.zeros_like(st_ref)
        cq_s[...] = jnp.zeros_like(cq_s)
        ck_s[...] = jnp.zeros_like(ck_s)
        cv_s[...] = jnp.zeros_like(cv_s)

    tb = q_ref.shape[1]

    def conv_silu(x_ref, carry, w_ref):
        x = x_ref[0]
        w = w_ref[...]
        y = x * w[3:4]
        for j in (1, 2, 3):
            y = y + _shifted(x, carry, j) * w[3 - j:4 - j]
        carry[...] = x[tb - SUBLANES:tb]
        return _silu(y)

    def l2n(x):
        return x * lax.rsqrt(jnp.sum(x * x, axis=-1, keepdims=True) + L2_EPS)

    q_all = conv_silu(q_ref, cq_s, cq_ref)
    k_all = conv_silu(k_ref, ck_s, ck_ref)
    v_all = conv_silu(v_ref, cv_s, cv_ref)

    c = CHUNK
    incl = _iota2((c, c), 1) <= _iota2((c, c), 0)
    strict = _iota2((c, c), 1) < _iota2((c, c), 0)
    rb = _iota2((tb, tb), 0)
    cb = _iota2((tb, tb), 1)
    chunk_tri = ((cb <= rb) & ((cb // c) == (rb // c))).astype(F32)
    lane = _iota2((tb, LANES), 1)
    ab = ab_ref[0]
    gmat = -jnp.exp(al_ref[...]) * _softplus(ab + dt_ref[...])
    gcum = _mm_sel(chunk_tri, gmat)
    items = [(h, ci) for h in range(heads) for ci in range(nchunk)]
    per_head = []
    for h in range(heads):
        cols = slice(h * GDN_HEAD, (h + 1) * GDN_HEAD)
        q = l2n(q_all[:, cols]) * GDN_HEAD ** -0.5
        k = l2n(k_all[:, cols])
        g_col = jnp.sum(jnp.where(lane == head0 + h, gcum, 0.0), axis=1, keepdims=True)
        beta = _sigmoid(jnp.sum(jnp.where(lane == head0 + h + GDN_HEADS, ab, 0.0), axis=1, keepdims=True))
        eg = jnp.exp(g_col)
        kb = k * beta
        per_head.append(dict(q=q, k=k, g_col=g_col, kb=kb, qg=q * eg,
                             rhs=jnp.concatenate([v_all[:, cols] * beta, kb * eg], axis=1),
                             pick=(_iota2((c, LANES), 1) == head0 + h).astype(F32)))

    def rows(name, h, ci):
        return per_head[h][name][ci * c:(ci + 1) * c]

    g_rows = [_mm_sel(per_head[h]["pick"], gcum[ci * c:(ci + 1) * c], NT) for h, ci in items]
    kks = [_mm(rows("kb", h, ci), rows("k", h, ci), NT) for h, ci in items]
    qks = [_mm(rows("q", h, ci), rows("k", h, ci), NT) for h, ci in items]
    gams = [jnp.where(incl, jnp.exp(jnp.where(incl, rows("g_col", h, ci) - gr, 0.0)), 0.0)
            for (h, ci), gr in zip(items, g_rows)]
    lows = [jnp.where(strict, kk * gam, 0.0) for kk, gam in zip(kks, gams)]
    aqks = [jnp.where(incl, qk * gam, 0.0).astype(BF16) for qk, gam in zip(qks, gams)]
    g_lasts = [rows("g_col", h, ci)[c - 1:c] for h, ci in items]
    kgts = [(rows("k", h, ci) * jnp.exp(gl - rows("g_col", h, ci))).T.astype(BF16)
            for (h, ci), gl in zip(items, g_lasts)]
    decays = [jnp.exp(gl) for gl in g_lasts]
    tinvs = _neumann_inverses([-low for low in lows])
    uws = [_mm(tinvs[i], per_head[h]["rhs"][ci * c:(ci + 1) * c]) for i, (h, ci) in enumerate(items)]
    ows = [_mm(aqks[i], uws[i]) for i in range(len(items))]
    kws = [_mm(kgts[i], uws[i]) for i in range(len(items))]
    o0s = [ow[:, 0:GDN_HEAD] for ow in ows]
    qps = [(per_head[h]["qg"][ci * c:(ci + 1) * c] - ows[i][:, GDN_HEAD:2 * GDN_HEAD]).astype(BF16)
           for i, (h, ci) in enumerate(items)]
    nts = [kw[:, 0:GDN_HEAD] for kw in kws]
    mts = [(-kw[:, GDN_HEAD:2 * GDN_HEAD]).astype(BF16) for kw in kws]

    states = [st_ref[h] for h in range(heads)]
    outs = [[] for _ in range(heads)]
    for ci in range(nchunk):
        for h in range(heads):
            i = h * nchunk + ci
            sb = states[h].astype(BF16)
            outs[h].append(o0s[i] + _mm(qps[i], sb))
            states[h] = states[h] * decays[i] + _mm(mts[i], sb) + nts[i]
    for h in range(heads):
        cols = slice(h * GDN_HEAD, (h + 1) * GDN_HEAD)
        st_ref[h] = states[h]
        o = jnp.concatenate(outs[h], axis=0)
        o_ref[0, :, cols] = _head_rms(o, nw_ref[:, cols], z_ref[0, :, cols]).astype(o_ref.dtype)


def _gdn(proj, small, conv_w, alog_pad, dt_pad, norm_w, tb=256, heads=8):
    b, t, _ = proj.shape
    w = GDN_HEAD * heads
    groups = GDN_HEADS // heads
    base = 0
    ab_blk = (small.shape[2] - LANES) // LANES

    def col(section):
        return lambda bi, hi, ti: (bi, ti, base + section * groups + hi)

    def cw(section):
        return pl.BlockSpec((4, w), lambda bi, hi, ti: (0, section * groups + hi))

    return pl.pallas_call(
        functools.partial(_gdn_kernel, nchunk=tb // CHUNK, heads=heads),
        out_shape=jax.ShapeDtypeStruct((b, t, GROUP_W), BF16),
        grid=(b, groups, t // tb),
        in_specs=[pl.BlockSpec((1, tb, w), col(0)), pl.BlockSpec((1, tb, w), col(1)),
                  pl.BlockSpec((1, tb, w), col(2)), pl.BlockSpec((1, tb, w), col(3)),
                  pl.BlockSpec((1, tb, LANES), lambda bi, hi, ti: (bi, ti, ab_blk)),
                  cw(0), cw(1), cw(2),
                  pl.BlockSpec((1, LANES), lambda bi, hi, ti: (0, 0)),
                  pl.BlockSpec((1, LANES), lambda bi, hi, ti: (0, 0)),
                  pl.BlockSpec((1, w), lambda bi, hi, ti: (0, hi))],
        out_specs=pl.BlockSpec((1, tb, w), lambda bi, hi, ti: (bi, ti, hi)),
        scratch_shapes=[pltpu.VMEM((heads, GDN_HEAD, GDN_HEAD), F32)]
        + [pltpu.VMEM((SUBLANES, w), F32)] * 3,
        compiler_params=_cparams(("parallel", "parallel", "arbitrary")),
        name="gdn_chunk",
    )(proj, proj, proj, proj, small, conv_w, conv_w, conv_w, alog_pad, dt_pad, norm_w)


def _rwkv_kernel(r_ref, k_ref, v_ref, sm_ref, mur_ref, muk_ref, muv_ref, mus_ref, w0_ref, w2_ref, a0_ref,
                 a2_ref, g2_ref, kk_ref, ka_ref, rk_ref, gw_ref, gb_ref, o_ref,
                 st_ref, cr_s, ck_s, cv_s, cs_s, *, nchunk, lora):
    @pl.when(pl.program_id(2) == 0)
    def _():
        for ref in (st_ref, cr_s, ck_s, cv_s, cs_s):
            ref[...] = jnp.zeros_like(ref)

    tb = r_ref.shape[1]
    half = RWKV_HEAD
    block_diag = (_iota2((LANES, LANES), 0) // half) == (_iota2((LANES, LANES), 1) // half)
    ones_bd = block_diag.astype(BF16)

    def seg_sum(x):
        hi = x.astype(BF16)
        lo = (x - hi.astype(F32)).astype(BF16)
        parts = []
        for j in range(x.shape[1] // LANES):
            cols = slice(j * LANES, (j + 1) * LANES)
            parts.append(lax.dot_general(hi[:, cols], ones_bd, NN, preferred_element_type=F32)
                         + lax.dot_general(lo[:, cols], ones_bd, NN, preferred_element_type=F32))
        return parts[0] if len(parts) == 1 else jnp.concatenate(parts, axis=1)

    def token_shift(x, carry, mu):
        prev = _shifted(x, carry, 1)
        carry[...] = x[tb - SUBLANES:tb]
        return x + (prev - x) * mu

    r = token_shift(r_ref[0], cr_s, mur_ref[...])
    k = token_shift(k_ref[0], ck_s, muk_ref[...])
    v = token_shift(v_ref[0], cv_s, muv_ref[...])
    sm = token_shift(sm_ref[0], cs_s, mus_ref[...])
    wd = sm[:, 0:lora]
    ad = sm[:, lora:2 * lora]
    gd = sm[:, 2 * lora:2 * lora + g2_ref.shape[0]]
    z = w0_ref[...] + _mm_hi(jnp.tanh(wd), w2_ref[...])
    lw = -jnp.exp(-_softplus(-z) - 0.5)
    ag = _sigmoid(a0_ref[...] + _mm_hi(ad, a2_ref[...]))
    gate = _mm(_sigmoid(gd), g2_ref[...])
    kk = k * kk_ref[...]
    kk = kk * lax.rsqrt(seg_sum(kk * kk) + L2_EPS)
    k2 = k * (1.0 + (ag - 1.0) * ka_ref[...])
    aa_all = -kk
    bb_all = kk * ag
    npair = r.shape[1] // LANES

    c = CHUNK
    m0 = _iota2((c, LANES), 1) < half
    row = _iota2((c, LANES), 0)
    sub = _iota2((c, LANES), 1) % c
    strict = sub < row
    incl = sub <= row
    rb = _iota2((tb, tb), 0)
    cb = _iota2((tb, tb), 1)
    chunk_tri = ((cb <= rb) & ((cb // c) == (rb // c))).astype(F32)
    gc_all = _mm_sel(chunk_tri, lw)
    gx_all = gc_all - lw

    def stack(x):
        return jnp.concatenate([jnp.where(m0, x, 0.0), jnp.where(m0, 0.0, x)], axis=0)

    def unstack(x):
        return jnp.where(m0, x[0:c], x[c:2 * c])

    items = [(pi, ci) for pi in range(npair) for ci in range(nchunk)]
    a_bds, ak_xs, wmats, a_sts, v_sws, r_abss, nvs, bes, decays = [], [], [], [], [], [], [], [], []
    for pi, ci in items:
        sl = (slice(ci * c, (ci + 1) * c), slice(pi * LANES, (pi + 1) * LANES))
        rr, kc, vc, aa, bb = r[sl], k2[sl], v[sl], aa_all[sl], bb_all[sl]
        gc, gx = gc_all[sl], gx_all[sl]
        g_mid = gc[c // 2 - 1:c // 2]
        g_last = gc[c - 1:c]
        e_out = jnp.exp(g_mid - gc)
        at = aa * jnp.exp(gx - g_mid)
        rt = rr * jnp.exp(gc - g_mid)
        bt = bb * e_out
        kt = kc * e_out
        lhs0 = jnp.concatenate([jnp.where(m0, at, 0.0), jnp.where(m0, rt, 0.0)], axis=0)
        lhs1 = jnp.concatenate([jnp.where(m0, 0.0, at), jnp.where(m0, 0.0, rt)], axis=0)
        res0 = _mm(lhs0, jnp.concatenate([bt, kt], axis=0), NT)
        res1 = _mm(lhs1, jnp.concatenate([kt, bt], axis=0), NT)
        top0 = jnp.where(strict, res0[0:c], 0.0)
        top1 = jnp.where(strict, res1[0:c], 0.0)
        a_bds.append(jnp.concatenate([jnp.where(m0, top0, 0.0), jnp.where(m0, 0.0, top1)], axis=0))
        ak_xs.append(jnp.concatenate([jnp.where(m0, 0.0, top0), jnp.where(m0, top1, 0.0)], axis=0))
        wmats.append(jnp.concatenate([jnp.where(incl, res0[c:2 * c], 0.0),
                                      jnp.where(incl, res1[c:2 * c], 0.0)], axis=0))
        a_sts.append(stack(aa * jnp.exp(gx)))
        v_sws.append(jnp.concatenate([jnp.where(m0, 0.0, vc), jnp.where(m0, vc, 0.0)], axis=0))
        r_abss.append(rr * jnp.exp(gc))
        e_end = jnp.exp(g_last - gc)
        bes.append(bb * e_end)
        nvs.append((vc, kc * e_end))
        decays.append(jnp.exp(g_last))
    tinvs = _neumann_inverses(a_bds)

    n_items = len(items)
    rhs_vs = [_mm(ak_xs[i], v_sws[i]) for i in range(n_items)]
    uws = [_mm(tinvs[i], jnp.concatenate([rhs_vs[i], a_sts[i]], axis=1)) for i in range(n_items)]
    yws = [_mm(wmats[i], jnp.concatenate([uws[i][:, 0:LANES] + v_sws[i], uws[i][:, LANES:2 * LANES]], axis=1))
           for i in range(n_items)]
    y0s = [unstack(yws[i][:, 0:LANES]) for i in range(n_items)]
    qps = [(unstack(yws[i][:, LANES:2 * LANES]) + r_abss[i]).astype(BF16) for i in range(n_items)]
    gbds = [jnp.where(block_diag, _mm(unstack(uws[i][:, LANES:2 * LANES]), bes[i], TN), 0.0).astype(BF16)
            for i in range(n_items)]
    ncs = [jnp.where(block_diag, _mm(jnp.concatenate([unstack(uws[i][:, 0:LANES]), nvs[i][0]], axis=0),
                                     jnp.concatenate([bes[i], nvs[i][1]], axis=0), TN), 0.0)
           for i in range(n_items)]

    hts = [st_ref[pi] for pi in range(npair)]
    ys = [[] for _ in range(npair)]
    for ci in range(nchunk):
        for pi in range(npair):
            i = pi * nchunk + ci
            htb = hts[pi].astype(BF16)
            ys[pi].append(y0s[i] + _mm(qps[i], htb, NT))
            hts[pi] = hts[pi] * decays[i] + _mm(htb, gbds[i]) + ncs[i]
    for pi in range(npair):
        st_ref[pi] = hts[pi]

    ycols = [jnp.concatenate(ys[pi], axis=0) for pi in range(npair)]
    y = ycols[0] if npair == 1 else jnp.concatenate(ycols, axis=1)
    mean = seg_sum(y) * (1.0 / half)
    yc = y - mean
    var = seg_sum(yc * yc) * (1.0 / half)
    yn = yc * lax.rsqrt(var + RWKV_GN_EPS) * gw_ref[...] + gb_ref[...]
    bonus = seg_sum(r * k2 * rk_ref[...]) * v
    o_ref[0] = ((yn + bonus) * gate).astype(o_ref.dtype)


def _rwkv(proj, small, mu_main, mu_small, w0, w2p, a0, a2p, g2, kk, ka, rk, gw, gb, lora, tb=256, npair=8):
    b, t, _ = proj.shape
    w = LANES * npair
    groups = GROUP_W // w
    ws = small.shape[2]
    glora = g2.shape[0]

    def col(section):
        return lambda bi, hi, ti: (bi, ti, section * groups + hi)

    def vec(section=0):
        return pl.BlockSpec((1, w), lambda bi, hi, ti: (0, section * groups + hi))

    row_blk = lambda s: pl.BlockSpec((1, tb, w), col(s))
    return pl.pallas_call(
        functools.partial(_rwkv_kernel, nchunk=tb // CHUNK, lora=lora),
        out_shape=jax.ShapeDtypeStruct((b, t, GROUP_W), BF16),
        grid=(b, groups, t // tb),
        in_specs=[row_blk(0), row_blk(1), row_blk(2),
                  pl.BlockSpec((1, tb, ws), lambda bi, hi, ti: (bi, ti, 0)),
                  vec(0), vec(1), vec(2),
                  pl.BlockSpec((1, ws), lambda bi, hi, ti: (0, 0)),
                  vec(), pl.BlockSpec((lora, w), lambda bi, hi, ti: (0, hi)),
                  vec(), pl.BlockSpec((lora, w), lambda bi, hi, ti: (0, hi)),
                  pl.BlockSpec((glora, w), lambda bi, hi, ti: (0, hi)),
                  vec(), vec(), vec(), vec(), vec()],
        out_specs=pl.BlockSpec((1, tb, w), lambda bi, hi, ti: (bi, ti, hi)),
        scratch_shapes=[pltpu.VMEM((npair, LANES, LANES), F32)]
        + [pltpu.VMEM((SUBLANES, w), F32)] * 3
        + [pltpu.VMEM((SUBLANES, ws), F32)],
        compiler_params=_cparams(("parallel", "parallel", "arbitrary")),
        name="rwkv7_chunk",
    )(proj, proj, proj, small, mu_main, mu_main, mu_main, mu_small, w0, w2p, a0, a2p, g2,
      kk, ka, rk, gw, gb)


def _pad_cols(a, width):
    return jnp.pad(a, ((0, 0), (0, width - a.shape[1])))


def _pad_rows(a, height):
    return jnp.pad(a, ((0, height - a.shape[0]), (0, 0)))


def _row(v):
    return v.reshape(1, -1).astype(F32)


def kernel(x, p, hgrn_lb_logits, e_w_in, e_gla_w2, e_gla_b, e_hgrn_norm, e_gla_norm, e_w_out, o_w_in, o_rwkv_mu, o_rwkv_w0, o_rwkv_w2, o_rwkv_a0, o_rwkv_a2, o_rwkv_g2, o_rwkv_kk, o_rwkv_ka, o_rwkv_rk, o_rwkv_gn_w, o_rwkv_gn_b, o_gdn_conv, o_gdn_a_log, o_gdn_dt_bias, o_gdn_norm, o_w_out, ln_mix_w, ln_mix_b, ln_ffn_w, ln_ffn_b, ffn_w_up, ffn_conv, ffn_w_down, ple_w_proj, ple_w_gate):
    bsz, seq, d = x.shape
    m = bsz * seq
    depth = ln_mix_w.shape[0]
    alpha = (2 * depth) ** 0.25
    gw = GROUP_W
    x32 = x.reshape(m, d)
    xb = x32.astype(BF16)
    w_down = ffn_w_down.astype(BF16)
    for layer in range(depth):
        j = layer // 2
        if layer % 2 == 0:
            rank = e_gla_w2.shape[1]
            gd0 = 4 * gw + 2 * GLA_HEADS * GLA_DK + gw
            wt = jnp.swapaxes(e_w_in, 1, 2)
            w_tail = jnp.concatenate([wt[j, gd0 + rank:], _pad_rows(wt[j, gd0:gd0 + rank], LANES)],
                                     axis=0)
            proj = _proj(xb, wt, gd0, layer=j).reshape(bsz, seq, -1)
            tail = _proj(xb, w_tail, w_tail.shape[0], tn=w_tail.shape[0]).reshape(bsz, seq, -1)
            o_a = _hgrn2(proj, hgrn_lb_logits.astype(F32), _row(e_hgrn_norm[j]), layer)
            o_b = _gla(proj, tail, _pad_rows(e_gla_w2[j], LANES), _row(e_gla_b[j]), _row(e_gla_norm[j]))
            w_out = e_w_out[j]
        else:
            wt = jnp.swapaxes(o_w_in, 1, 2)
            lora_w = o_rwkv_w2.shape[1]
            lora_a = o_rwkv_a2.shape[1]
            lora_g = o_rwkv_g2.shape[1]
            lora = LANES * (-(-max(lora_w, lora_a) // LANES))
            c_wd = 3 * gw
            c_ad = c_wd + lora_w
            c_gd = c_ad + lora_a
            c_qkv = c_gd + lora_g
            c_z = c_qkv + 3 * gw
            c_ab = c_z + gw
            w_small = jnp.concatenate([_pad_rows(wt[j, c_wd:c_ad], lora), _pad_rows(wt[j, c_ad:c_gd], lora),
                                       wt[j, c_gd:c_qkv], _pad_rows(wt[j, c_ab:], LANES)], axis=0)
            mu = o_rwkv_mu[j].reshape(1, -1)
            mu_small = jnp.concatenate([_pad_cols(mu[:, c_wd:c_ad], lora), _pad_cols(mu[:, c_ad:c_gd], lora),
                                        mu[:, c_gd:c_qkv], jnp.zeros((1, LANES), F32)], axis=1)
            proj_rkv = _proj(xb, wt, c_wd, layer=j).reshape(bsz, seq, -1)
            proj_gdn = _proj(xb, wt, c_ab - c_qkv, layer=j, row0=c_qkv).reshape(bsz, seq, -1)
            small = _proj(xb, w_small, w_small.shape[0], tn=w_small.shape[0]).reshape(bsz, seq, -1)
            o_a = _rwkv(proj_rkv, small, mu[:, :c_wd], mu_small, _row(o_rwkv_w0[j]),
                        _pad_rows(o_rwkv_w2[j], lora), _row(o_rwkv_a0[j]), _pad_rows(o_rwkv_a2[j], lora),
                        o_rwkv_g2[j], _row(o_rwkv_kk[j]), _row(o_rwkv_ka[j]), _row(o_rwkv_rk[j]),
                        _row(o_rwkv_gn_w[j]), _row(o_rwkv_gn_b[j]), lora)
            o_b = _gdn(proj_gdn, small, o_gdn_conv[j].astype(F32),
                       _pad_cols(_row(o_gdn_a_log[j]), LANES), _pad_cols(_row(o_gdn_dt_bias[j]), LANES),
                       _row(o_gdn_norm[j]))
            w_out = o_w_out[j]
        x32, xb = _mix_out(o_a.reshape(m, gw), o_b.reshape(m, gw), w_out.astype(BF16), x32,
                           _row(ln_mix_w[layer]), _row(ln_mix_b[layer]), alpha)
        h = _ffn_up(xb, ffn_w_up, ffn_conv, layer, seq)
        x32, xb = _ffn_down(h, w_down, x32, _row(ln_ffn_w[layer]), _row(ln_ffn_b[layer]), alpha, layer)
        x32, xb = _ple(xb, x32, p.reshape(depth, m, -1), ple_w_gate, ple_w_proj, layer)
    return x32.reshape(bsz, seq, d)
```

```python
import functools

import jax
import jax.numpy as jnp
from jax import lax
from jax.experimental import pallas as pl
from jax.experimental.pallas import tpu as pltpu

F32 = jnp.float32
BF16 = jnp.bfloat16

NN = (((1,), (0,)), ((), ()))
NT = (((1,), (1,)), ((), ()))
TN = (((0,), (0,)), ((), ()))

D_MODEL = 2048
GROUP_W = 1024
CHUNK = 64
LANES = 128
SUBLANES = 8
HGRN_HEADS = 8
GLA_HEADS = 4
GLA_DK = 128
GLA_DV = 256
GLA_TAU = 16.0
RWKV_HEAD = 64
RWKV_GN_EPS = 64e-5
GDN_HEADS = 8
GDN_HEAD = 128
FFN_DIM = 5632
LN_EPS = 1e-5
LN_SUBROWS = 128
RMS_EPS = 1e-6
L2_EPS = 1e-6

VMEM_LIMIT = 56 * 1024 * 1024


def _mm(a, b, dims=NN):
    return lax.dot_general(a.astype(BF16), b.astype(BF16), dims, preferred_element_type=F32)


def _split_bf16(x, parts):
    out = []
    for _ in range(parts - 1):
        h = x.astype(BF16)
        out.append(h)
        x = x - h.astype(F32)
    out.append(x.astype(BF16))
    return out


def _bdot(a, b, dims):
    return lax.dot_general(a, b, dims, preferred_element_type=F32)


def _mm_sel(sel, b, dims=NN):
    s = sel.astype(BF16)
    b1, b2, b3 = _split_bf16(b, 3)
    return (_bdot(s, b3, dims) + _bdot(s, b2, dims)) + _bdot(s, b1, dims)


def _mm_hi(a, b, dims=NN):
    ah, al = _split_bf16(a, 2)
    bh, bl = _split_bf16(b, 2)
    return (_bdot(al, bh, dims) + _bdot(ah, bl, dims)) + _bdot(ah, bh, dims)


def _sigmoid(x):
    return 1.0 / (1.0 + jnp.exp(-x))


def _silu(x):
    return x * _sigmoid(x)


def _softplus(x):
    return jnp.maximum(x, 0.0) + jnp.log(1.0 + jnp.exp(-jnp.abs(x)))


def _iota2(shape, dim):
    return lax.broadcasted_iota(jnp.int32, shape, dim)


def _cparams(sem):
    return pltpu.CompilerParams(dimension_semantics=sem, vmem_limit_bytes=VMEM_LIMIT)


def _proj_kernel(x_ref, w_ref, o_ref, wb_ref):
    @pl.when(pl.program_id(1) == 0)
    def _():
        w = w_ref[...] if len(w_ref.shape) == 2 else w_ref[0]
        wb_ref[...] = w.T.astype(BF16)

    o_ref[...] = lax.dot_general(x_ref[...], wb_ref[...], NN, preferred_element_type=F32).astype(o_ref.dtype)


def _proj(x, wt, n_cols, out_dtype=F32, layer=None, row0=0, tm=1024, tn=1024):
    m, k = x.shape
    tn = min(tn, n_cols)
    if layer is None:
        w_spec = pl.BlockSpec((tn, k), lambda j, i: (j, 0))
    elif row0 % tn == 0:
        w_spec = pl.BlockSpec((None, tn, k), lambda j, i: (layer, row0 // tn + j, 0))
    else:
        w_spec = pl.BlockSpec((pl.Element(1), pl.Element(tn), pl.Element(k)),
                              lambda j, i: (layer, pl.multiple_of(row0 + j * tn, SUBLANES), 0))
    return pl.pallas_call(
        _proj_kernel,
        out_shape=jax.ShapeDtypeStruct((m, n_cols), out_dtype),
        grid=(n_cols // tn, m // tm),
        in_specs=[pl.BlockSpec((tm, k), lambda j, i: (i, 0)), w_spec],
        out_specs=pl.BlockSpec((tm, tn), lambda j, i: (i, j)),
        scratch_shapes=[pltpu.VMEM((k, tn), BF16)],
        compiler_params=_cparams(("parallel", "arbitrary")),
        name="proj_matmul",
    )(x, wt)


def _layer_norm_rows(y, w, b):
    mu = jnp.mean(y, axis=-1, keepdims=True)
    yc = y - mu
    var = jnp.mean(yc * yc, axis=-1, keepdims=True)
    return yc * lax.rsqrt(var + LN_EPS) * w + b


def _mix_out_kernel(oa_ref, ob_ref, w_ref, x_ref, lw_ref, lb_ref, o32_ref, o16_ref, *, alpha):
    half = oa_ref.shape[1]
    for s in range(oa_ref.shape[0] // LN_SUBROWS):
        rows = slice(s * LN_SUBROWS, (s + 1) * LN_SUBROWS)
        acc = _mm(oa_ref[rows, :], w_ref[0:half, :]) + _mm(ob_ref[rows, :], w_ref[half:2 * half, :])
        y = _layer_norm_rows(alpha * x_ref[rows, :] + acc, lw_ref[...], lb_ref[...])
        o32_ref[rows, :] = y
        o16_ref[rows, :] = y.astype(BF16)


def _mix_out(oa, ob, w, x, lw, lb, alpha, tm=512):
    m, half = oa.shape
    n = w.shape[1]
    row = lambda i: (i, 0)
    fixed = lambda i: (0, 0)
    return pl.pallas_call(
        functools.partial(_mix_out_kernel, alpha=alpha),
        out_shape=(jax.ShapeDtypeStruct((m, n), F32), jax.ShapeDtypeStruct((m, n), BF16)),
        grid=(m // tm,),
        in_specs=[pl.BlockSpec((tm, half), row), pl.BlockSpec((tm, half), row),
                  pl.BlockSpec((2 * half, n), fixed), pl.BlockSpec((tm, n), row),
                  pl.BlockSpec((1, n), fixed), pl.BlockSpec((1, n), fixed)],
        out_specs=(pl.BlockSpec((tm, n), row), pl.BlockSpec((tm, n), row)),
        compiler_params=_cparams(("parallel",)),
        name="mix_out_ln",
    )(oa, ob, w, x, lw, lb)


def _ffn_up_kernel(x_ref, wg_ref, wv_ref, cg_ref, cv_ref, o_ref, wgb_ref, wvb_ref, hg_ref, hv_ref,
                   *, tiles_per_seq):
    i = pl.program_id(1)
    tm = x_ref.shape[0]

    @pl.when(i == 0)
    def _():
        wgb_ref[...] = wg_ref[...].astype(BF16)
        wvb_ref[...] = wv_ref[...].astype(BF16)
        hg_ref[...] = jnp.zeros_like(hg_ref)
        hv_ref[...] = jnp.zeros_like(hv_ref)

    first = (i % tiles_per_seq) == 0

    def conv(w_ref, h_ref, c_ref):
        u = lax.dot_general(x_ref[...], w_ref[...], NN, preferred_element_type=F32)
        halo = jnp.where(first, 0.0, h_ref[...])
        h_ref[...] = u[tm - SUBLANES:tm]
        ext = jnp.concatenate([halo, u], axis=0)
        p1 = pltpu.roll(ext, 1, 0)[SUBLANES:SUBLANES + tm]
        p2 = pltpu.roll(ext, 2, 0)[SUBLANES:SUBLANES + tm]
        c = c_ref[...]
        return u * c[2:3] + p1 * c[1:2] + p2 * c[0:1]

    g = conv(wgb_ref, hg_ref, cg_ref)
    v = conv(wvb_ref, hv_ref, cv_ref)
    o_ref[...] = (_silu(g) * v).astype(o_ref.dtype)


def _ffn_up(xb, w_up, conv_w, layer, seq, tm=1024, tf=512):
    m, k = xb.shape
    f = w_up.shape[2] // 2
    nf = f // tf
    return pl.pallas_call(
        functools.partial(_ffn_up_kernel, tiles_per_seq=seq // tm),
        out_shape=jax.ShapeDtypeStruct((m, f), BF16),
        grid=(nf, m // tm),
        in_specs=[pl.BlockSpec((tm, k), lambda j, i: (i, 0)),
                  pl.BlockSpec((None, k, tf), lambda j, i: (layer, 0, j)),
                  pl.BlockSpec((None, k, tf), lambda j, i: (layer, 0, nf + j)),
                  pl.BlockSpec((None, 3, tf), lambda j, i: (layer, 0, j)),
                  pl.BlockSpec((None, 3, tf), lambda j, i: (layer, 0, nf + j))],
        out_specs=pl.BlockSpec((tm, tf), lambda j, i: (i, j)),
        scratch_shapes=[pltpu.VMEM((k, tf), BF16)] * 2 + [pltpu.VMEM((SUBLANES, tf), F32)] * 2,
        compiler_params=_cparams(("parallel", "arbitrary")),
        name="ffn_up_conv_gate",
    )(xb, w_up, w_up, conv_w, conv_w)


def _ffn_down_kernel(h_ref, w_ref, x_ref, lw_ref, lb_ref, o32_ref, o16_ref, acc_ref, *, alpha, nk):
    k = pl.program_id(1)

    @pl.when(k == 0)
    def _():
        acc_ref[...] = _mm(h_ref[...], w_ref[...])

    @pl.when((k > 0) & (k < nk - 1))
    def _():
        acc_ref[...] += _mm(h_ref[...], w_ref[...])

    @pl.when(k == nk - 1)
    def _():
        for s in range(h_ref.shape[0] // LN_SUBROWS):
            rows = slice(s * LN_SUBROWS, (s + 1) * LN_SUBROWS)
            acc = acc_ref[rows, :] + _mm(h_ref[rows, :], w_ref[...])
            y = _layer_norm_rows(alpha * x_ref[rows, :] + acc, lw_ref[...], lb_ref[...])
            o32_ref[rows, :] = y
            o16_ref[rows, :] = y.astype(BF16)


def _ffn_down(h, w, x, lw, lb, alpha, layer, tm=512, tk=2816):
    m, kdim = h.shape
    n = w.shape[2]
    nk = kdim // tk
    assert nk >= 2
    return pl.pallas_call(
        functools.partial(_ffn_down_kernel, alpha=alpha, nk=nk),
        out_shape=(jax.ShapeDtypeStruct((m, n), F32), jax.ShapeDtypeStruct((m, n), BF16)),
        grid=(m // tm, nk),
        in_specs=[pl.BlockSpec((tm, tk), lambda i, k: (i, k)),
                  pl.BlockSpec((None, tk, n), lambda i, k: (layer, k, 0)),
                  pl.BlockSpec((tm, n), lambda i, k: (i, 0)),
                  pl.BlockSpec((1, n), lambda i, k: (0, 0)),
                  pl.BlockSpec((1, n), lambda i, k: (0, 0))],
        out_specs=(pl.BlockSpec((tm, n), lambda i, k: (i, 0)),
                   pl.BlockSpec((tm, n), lambda i, k: (i, 0))),
        scratch_shapes=[pltpu.VMEM((tm, n), F32)],
        compiler_params=_cparams(("parallel", "arbitrary")),
        name="ffn_down_ln",
    )(h, w, x, lw, lb)


def _ple_kernel(xb_ref, wg_ref, p_ref, wp_ref, x_ref, o32_ref, o16_ref, wgb_ref):
    @pl.when(pl.program_id(1) == 0)
    def _():
        wgb_ref[...] = wg_ref[...].astype(BF16)

    wp = wp_ref[...].astype(BF16)
    for s in range(xb_ref.shape[0] // LN_SUBROWS):
        rows = slice(s * LN_SUBROWS, (s + 1) * LN_SUBROWS)
        gate = _sigmoid(_mm(xb_ref[rows, :], wgb_ref[...]))
        y = x_ref[rows, :] + gate * _mm(p_ref[rows, :], wp)
        o32_ref[rows, :] = y
        o16_ref[rows, :] = y.astype(BF16)


def _ple(xb, x, p, wg, wp, layer, tm=512, tn=1024):
    m, k = xb.shape
    n = wg.shape[2]
    kp = p.shape[2]
    return pl.pallas_call(
        _ple_kernel,
        out_shape=(jax.ShapeDtypeStruct((m, n), F32), jax.ShapeDtypeStruct((m, n), BF16)),
        grid=(n // tn, m // tm),
        in_specs=[pl.BlockSpec((tm, k), lambda j, i: (i, 0)),
                  pl.BlockSpec((None, k, tn), lambda j, i: (layer, 0, j)),
                  pl.BlockSpec((None, tm, kp), lambda j, i: (layer, i, 0)),
                  pl.BlockSpec((None, kp, tn), lambda j, i: (layer, 0, j)),
                  pl.BlockSpec((tm, tn), lambda j, i: (i, j))],
        out_specs=(pl.BlockSpec((tm, tn), lambda j, i: (i, j)),
                   pl.BlockSpec((tm, tn), lambda j, i: (i, j))),
        scratch_shapes=[pltpu.VMEM((k, tn), BF16)],
        compiler_params=_cparams(("parallel", "arbitrary")),
        name="ple_gate",
    )(xb, wg, p, wp, x)


def _chunk_tri(tb):
    rb = _iota2((tb, tb), 0)
    cb = _iota2((tb, tb), 1)
    return ((cb <= rb) & ((cb // CHUNK) == (rb // CHUNK))).astype(F32)


def _gla_block(q, k, g, v, st_ref, heads, dk, dv):
    tb = q.shape[0]
    c = CHUNK
    causal = _iota2((c, c), 1) <= _iota2((c, c), 0)
    gc_all = _mm_sel(_chunk_tri(tb), g)
    nchunk = tb // c
    items = [(h, ci) for h in range(heads) for ci in range(nchunk)]

    def part(x, h, ci, d):
        return x[ci * c:(ci + 1) * c, h * d:(h + 1) * d]

    gcs = [part(gc_all, h, ci, dk) for h, ci in items]
    scores = [_mm(part(q, h, ci, dk) * jnp.exp(gc - gc[c // 2:c // 2 + 1]),
                  part(k, h, ci, dk) * jnp.exp(gc[c // 2:c // 2 + 1] - gc), NT)
              for (h, ci), gc in zip(items, gcs)]
    intra = [_mm(jnp.where(causal, a, 0.0), part(v, h, ci, dv)) for (h, ci), a in zip(items, scores)]
    incs = [_mm(part(v, h, ci, dv), part(k, h, ci, dk) * jnp.exp(gc[c - 1:c] - gc), TN)
            for (h, ci), gc in zip(items, gcs)]
    qgs = [(part(q, h, ci, dk) * jnp.exp(gc)).astype(BF16) for (h, ci), gc in zip(items, gcs)]
    sts = [st_ref[h] for h in range(heads)]
    o_h = [[] for _ in range(heads)]
    for ci in range(nchunk):
        for h in range(heads):
            i = h * nchunk + ci
            o_h[h].append(intra[i] + _mm(qgs[i], sts[h], NT))
            sts[h] = sts[h] * jnp.exp(gcs[i][c - 1:c]) + incs[i]
    for h in range(heads):
        st_ref[h] = sts[h]
    return [jnp.concatenate(o, axis=0) for o in o_h]


def _head_rms(o, gain, gate):
    o = o * lax.rsqrt(jnp.mean(o * o, axis=-1, keepdims=True) + RMS_EPS)
    return o * gain * _silu(gate)


def _hgrn2_kernel(q_ref, f_ref, i_ref, g_ref, lg_ref, nw_ref, o_ref, st_ref, *, layer, heads, nchunk):
    @pl.when(pl.program_id(2) == 0)
    def _():
        st_ref[...] = jnp.zeros_like(st_ref)

    lg = lg_ref[...]
    e = jnp.exp(lg - jnp.max(lg, axis=0, keepdims=True))
    lb = jnp.sum(e[0:layer + 1], axis=0, keepdims=True) / jnp.sum(e, axis=0, keepdims=True)

    f = lb + (1.0 - lb) * _sigmoid(f_ref[0])
    outs = _gla_block(_silu(q_ref[0]), 1.0 - f, jnp.log(f), i_ref[0], st_ref, heads, LANES, LANES)
    for h in range(heads):
        cols = slice(h * LANES, (h + 1) * LANES)
        o_ref[0, :, cols] = _head_rms(outs[h], nw_ref[:, cols], g_ref[0, :, cols]).astype(o_ref.dtype)


def _hgrn2(proj, logits, norm_w, layer, tb=256, heads=8):
    b, t, _ = proj.shape
    w = LANES * heads
    per = GROUP_W // w

    def col(section):
        return lambda bi, hi, ti: (bi, ti, section * per + hi)

    return pl.pallas_call(
        functools.partial(_hgrn2_kernel, layer=layer, heads=heads, nchunk=tb // CHUNK),
        out_shape=jax.ShapeDtypeStruct((b, t, GROUP_W), BF16),
        grid=(b, per, t // tb),
        in_specs=[pl.BlockSpec((1, tb, w), col(0)), pl.BlockSpec((1, tb, w), col(1)),
                  pl.BlockSpec((1, tb, w), col(2)), pl.BlockSpec((1, tb, w), col(3)),
                  pl.BlockSpec((logits.shape[0], w), lambda bi, hi, ti: (0, hi)),
                  pl.BlockSpec((1, w), lambda bi, hi, ti: (0, hi))],
        out_specs=pl.BlockSpec((1, tb, w), lambda bi, hi, ti: (bi, ti, hi)),
        scratch_shapes=[pltpu.VMEM((heads, LANES, LANES), F32)],
        compiler_params=_cparams(("parallel", "parallel", "arbitrary")),
        name="hgrn2_chunk",
    )(proj, proj, proj, proj, logits, norm_w)


def _gla_kernel(q_ref, k_ref, v_ref, r_ref, gd_ref, w2_ref, b_ref, nw_ref, o_ref, st_ref, *, heads):
    @pl.when(pl.program_id(2) == 0)
    def _():
        st_ref[...] = jnp.zeros_like(st_ref)

    z = _mm_hi(gd_ref[0], w2_ref[...]) + b_ref[...]
    g = -_softplus(-z) * (1.0 / GLA_TAU)
    outs = _gla_block(q_ref[0] * GLA_DK ** -0.5, k_ref[0], g, v_ref[0], st_ref, heads, GLA_DK, GLA_DV)
    for h in range(heads):
        cols = slice(h * GLA_DV, (h + 1) * GLA_DV)
        o_ref[0, :, cols] = _head_rms(outs[h], nw_ref[:, cols], r_ref[0, :, cols]).astype(o_ref.dtype)


def _gla(proj, tail, w2, bias, norm_w, tb=256, heads=4):
    b, t, _ = proj.shape
    wk = GLA_DK * heads
    wv = GLA_DV * heads
    groups = GLA_HEADS // heads
    qk0 = 4 * GROUP_W // wk
    v0 = (4 * GROUP_W + 2 * GLA_HEADS * GLA_DK) // wv
    gd_blk = GROUP_W // LANES
    return pl.pallas_call(
        functools.partial(_gla_kernel, heads=heads),
        out_shape=jax.ShapeDtypeStruct((b, t, GROUP_W), BF16),
        grid=(b, groups, t // tb),
        in_specs=[pl.BlockSpec((1, tb, wk), lambda bi, hi, ti: (bi, ti, qk0 + hi)),
                  pl.BlockSpec((1, tb, wk), lambda bi, hi, ti: (bi, ti, qk0 + groups + hi)),
                  pl.BlockSpec((1, tb, wv), lambda bi, hi, ti: (bi, ti, v0 + hi)),
                  pl.BlockSpec((1, tb, wv), lambda bi, hi, ti: (bi, ti, hi)),
                  pl.BlockSpec((1, tb, LANES), lambda bi, hi, ti: (bi, ti, gd_blk)),
                  pl.BlockSpec((LANES, wk), lambda bi, hi, ti: (0, hi)),
                  pl.BlockSpec((1, wk), lambda bi, hi, ti: (0, hi)),
                  pl.BlockSpec((1, wv), lambda bi, hi, ti: (0, hi))],
        out_specs=pl.BlockSpec((1, tb, wv), lambda bi, hi, ti: (bi, ti, hi)),
        scratch_shapes=[pltpu.VMEM((heads, GLA_DV, GLA_DK), F32)],
        compiler_params=_cparams(("parallel", "parallel", "arbitrary")),
        name="gla_chunk",
    )(proj, proj, proj, tail, tail, w2, bias, norm_w)


def _neumann_inverses(mats):
    n = mats[0].shape[0]
    pack = 2 if (2 * n <= LANES and len(mats) % 2 == 0) else 1
    width = pack * n
    eye = (_iota2((n, width), 0) == _iota2((n, width), 1) % n).astype(F32)
    left = _iota2((n, width), 1) < n

    def rhs(x):
        if pack == 1:
            return x
        return jnp.concatenate([jnp.where(left, x, 0.0), jnp.where(left, 0.0, x)], axis=0)

    if pack == 2:
        mats = [jnp.concatenate([mats[i], mats[i + 1]], axis=1) for i in range(0, len(mats), 2)]
    ts = [eye + a for a in mats]
    ps = [_mm(a, rhs(a)) for a in mats]
    steps = max(1, (CHUNK - 1).bit_length() - 1)
    for _ in range(steps - 1):
        both = [_mm(jnp.concatenate([t, p], axis=0), rhs(p)) for t, p in zip(ts, ps)]
        ts = [t + b[0:n] for t, b in zip(ts, both)]
        ps = [b[n:2 * n] for b in both]
    ts = [t + _mm(t, rhs(p)) for t, p in zip(ts, ps)]
    if pack == 1:
        return ts
    out = []
    for t in ts:
        out += [t[:, 0:n], t[:, n:2 * n]]
    return out


def _shifted(x, carry_ref, shift):
    tb = x.shape[0]
    ext = jnp.concatenate([carry_ref[...], x], axis=0)
    return pltpu.roll(ext, shift, 0)[SUBLANES:SUBLANES + tb]


def _gdn_kernel(q_ref, k_ref, v_ref, z_ref, ab_ref, cq_ref, ck_ref, cv_ref, al_ref, dt_ref, nw_ref,
                o_ref, st_ref, cq_s, ck_s, cv_s, *, nchunk, heads):
    head0 = pl.program_id(1) * heads

    @pl.when(pl.program_id(2) == 0)
    def _():
        st_ref[...] = jnp.zeros_like(st_ref)
        cq_s[...] = jnp.zeros_like(cq_s)
        ck_s[...] = jnp.zeros_like(ck_s)
        cv_s[...] = jnp.zeros_like(cv_s)

    tb = q_ref.shape[1]

    def conv_silu(x_ref, carry, w_ref):
        x = x_ref[0]
        w = w_ref[...]
        y = x * w[3:4]
        for j in (1, 2, 3):
            y = y + _shifted(x, carry, j) * w[3 - j:4 - j]
        carry[...] = x[tb - SUBLANES:tb]
        return _silu(y)

    def l2n(x):
        return x * lax.rsqrt(jnp.sum(x * x, axis=-1, keepdims=True) + L2_EPS)

    q_all = conv_silu(q_ref, cq_s, cq_ref)
    k_all = conv_silu(k_ref, ck_s, ck_ref)
    v_all = conv_silu(v_ref, cv_s, cv_ref)

    c = CHUNK
    incl = _iota2((c, c), 1) <= _iota2((c, c), 0)
    strict = _iota2((c, c), 1) < _iota2((c, c), 0)
    rb = _iota2((tb, tb), 0)
    cb = _iota2((tb, tb), 1)
    chunk_tri = ((cb <= rb) & ((cb // c) == (rb // c))).astype(F32)
    lane = _iota2((tb, LANES), 1)
    ab = ab_ref[0]
    gmat = -jnp.exp(al_ref[...]) * _softplus(ab + dt_ref[...])
    gcum = _mm_sel(chunk_tri, gmat)
    items = [(h, ci) for h in range(heads) for ci in range(nchunk)]
    per_head = []
    for h in range(heads):
        cols = slice(h * GDN_HEAD, (h + 1) * GDN_HEAD)
        q = l2n(q_all[:, cols]) * GDN_HEAD ** -0.5
        k = l2n(k_all[:, cols])
        g_col = jnp.sum(jnp.where(lane == head0 + h, gcum, 0.0), axis=1, keepdims=True)
        beta = _sigmoid(jnp.sum(jnp.where(lane == head0 + h + GDN_HEADS, ab, 0.0), axis=1, keepdims=True))
        eg = jnp.exp(g_col)
        kb = k * beta
        per_head.append(dict(q=q, k=k, g_col=g_col, kb=kb, qg=q * eg,
                             rhs=jnp.concatenate([v_all[:, cols] * beta, kb * eg], axis=1),
                             pick=(_iota2((c, LANES), 1) == head0 + h).astype(F32)))

    def rows(name, h, ci):
        return per_head[h][name][ci * c:(ci + 1) * c]

    g_rows = [_mm_sel(per_head[h]["pick"], gcum[ci * c:(ci + 1) * c], NT) for h, ci in items]
    kks = [_mm(rows("kb", h, ci), rows("k", h, ci), NT) for h, ci in items]
    qks = [_mm(rows("q", h, ci), rows("k", h, ci), NT) for h, ci in items]
    gams = [jnp.where(incl, jnp.exp(jnp.where(incl, rows("g_col", h, ci) - gr, 0.0)), 0.0)
            for (h, ci), gr in zip(items, g_rows)]
    lows = [jnp.where(strict, kk * gam, 0.0) for kk, gam in zip(kks, gams)]
    aqks = [jnp.where(incl, qk * gam, 0.0).astype(BF16) for qk, gam in zip(qks, gams)]
    g_lasts = [rows("g_col", h, ci)[c - 1:c] for h, ci in items]
    kgts = [(rows("k", h, ci) * jnp.exp(gl - rows("g_col", h, ci))).T.astype(BF16)
            for (h, ci), gl in zip(items, g_lasts)]
    decays = [jnp.exp(gl) for gl in g_lasts]
    tinvs = _neumann_inverses([-low for low in lows])
    uws = [_mm(tinvs[i], per_head[h]["rhs"][ci * c:(ci + 1) * c]) for i, (h, ci) in enumerate(items)]
    ows = [_mm(aqks[i], uws[i]) for i in range(len(items))]
    kws = [_mm(kgts[i], uws[i]) for i in range(len(items))]
    o0s = [ow[:, 0:GDN_HEAD] for ow in ows]
    qps = [(per_head[h]["qg"][ci * c:(ci + 1) * c] - ows[i][:, GDN_HEAD:2 * GDN_HEAD]).astype(BF16)
           for i, (h, ci) in enumerate(items)]
    nts = [kw[:, 0:GDN_HEAD] for kw in kws]
    mts = [(-kw[:, GDN_HEAD:2 * GDN_HEAD]).astype(BF16) for kw in kws]

    states = [st_ref[h] for h in range(heads)]
    outs = [[] for _ in range(heads)]
    for ci in range(nchunk):
        for h in range(heads):
            i = h * nchunk + ci
            sb = states[h].astype(BF16)
            outs[h].append(o0s[i] + _mm(qps[i], sb))
            states[h] = states[h] * decays[i] + _mm(mts[i], sb) + nts[i]
    for h in range(heads):
        cols = slice(h * GDN_HEAD, (h + 1) * GDN_HEAD)
        st_ref[h] = states[h]
        o = jnp.concatenate(outs[h], axis=0)
        o_ref[0, :, cols] = _head_rms(o, nw_ref[:, cols], z_ref[0, :, cols]).astype(o_ref.dtype)


def _gdn(proj, small, conv_w, alog_pad, dt_pad, norm_w, tb=256, heads=8):
    b, t, _ = proj.shape
    w = GDN_HEAD * heads
    groups = GDN_HEADS // heads
    base = 0
    ab_blk = (small.shape[2] - LANES) // LANES

    def col(section):
        return lambda bi, hi, ti: (bi, ti, base + section * groups + hi)

    def cw(section):
        return pl.BlockSpec((4, w), lambda bi, hi, ti: (0, section * groups + hi))

    return pl.pallas_call(
        functools.partial(_gdn_kernel, nchunk=tb // CHUNK, heads=heads),
        out_shape=jax.ShapeDtypeStruct((b, t, GROUP_W), BF16),
        grid=(b, groups, t // tb),
        in_specs=[pl.BlockSpec((1, tb, w), col(0)), pl.BlockSpec((1, tb, w), col(1)),
                  pl.BlockSpec((1, tb, w), col(2)), pl.BlockSpec((1, tb, w), col(3)),
                  pl.BlockSpec((1, tb, LANES), lambda bi, hi, ti: (bi, ti, ab_blk)),
                  cw(0), cw(1), cw(2),
                  pl.BlockSpec((1, LANES), lambda bi, hi, ti: (0, 0)),
                  pl.BlockSpec((1, LANES), lambda bi, hi, ti: (0, 0)),
                  pl.BlockSpec((1, w), lambda bi, hi, ti: (0, hi))],
        out_specs=pl.BlockSpec((1, tb, w), lambda bi, hi, ti: (bi, ti, hi)),
        scratch_shapes=[pltpu.VMEM((heads, GDN_HEAD, GDN_HEAD), F32)]
        + [pltpu.VMEM((SUBLANES, w), F32)] * 3,
        compiler_params=_cparams(("parallel", "parallel", "arbitrary")),
        name="gdn_chunk",
    )(proj, proj, proj, proj, small, conv_w, conv_w, conv_w, alog_pad, dt_pad, norm_w)


def _rwkv_kernel(r_ref, k_ref, v_ref, sm_ref, mur_ref, muk_ref, muv_ref, mus_ref, w0_ref, w2_ref, a0_ref,
                 a2_ref, g2_ref, kk_ref, ka_ref, rk_ref, gw_ref, gb_ref, o_ref,
                 st_ref, cr_s, ck_s, cv_s, cs_s, *, nchunk, lora):
    @pl.when(pl.program_id(2) == 0)
    def _():
        for ref in (st_ref, cr_s, ck_s, cv_s, cs_s):
            ref[...] = jnp.zeros_like(ref)

    tb = r_ref.shape[1]
    half = RWKV_HEAD
    block_diag = (_iota2((LANES, LANES), 0) // half) == (_iota2((LANES, LANES), 1) // half)
    ones_bd = block_diag.astype(BF16)

    def seg_sum(x):
        hi = x.astype(BF16)
        lo = (x - hi.astype(F32)).astype(BF16)
        parts = []
        for j in range(x.shape[1] // LANES):
            cols = slice(j * LANES, (j + 1) * LANES)
            parts.append(lax.dot_general(hi[:, cols], ones_bd, NN, preferred_element_type=F32)
                         + lax.dot_general(lo[:, cols], ones_bd, NN, preferred_element_type=F32))
        return parts[0] if len(parts) == 1 else jnp.concatenate(parts, axis=1)

    def token_shift(x, carry, mu):
        prev = _shifted(x, carry, 1)
        carry[...] = x[tb - SUBLANES:tb]
        return x + (prev - x) * mu

    r = token_shift(r_ref[0], cr_s, mur_ref[...])
    k = token_shift(k_ref[0], ck_s, muk_ref[...])
    v = token_shift(v_ref[0], cv_s, muv_ref[...])
    sm = token_shift(sm_ref[0], cs_s, mus_ref[...])
    wd = sm[:, 0:lora]
    ad = sm[:, lora:2 * lora]
    gd = sm[:, 2 * lora:2 * lora + g2_ref.shape[0]]
    z = w0_ref[...] + _mm_hi(jnp.tanh(wd), w2_ref[...])
    lw = -jnp.exp(-_softplus(-z) - 0.5)
    ag = _sigmoid(a0_ref[...] + _mm_hi(ad, a2_ref[...]))
    gate = _mm(_sigmoid(gd), g2_ref[...])
    kk = k * kk_ref[...]
    kk = kk * lax.rsqrt(seg_sum(kk * kk) + L2_EPS)
    k2 = k * (1.0 + (ag - 1.0) * ka_ref[...])
    aa_all = -kk
    bb_all = kk * ag
    npair = r.shape[1] // LANES

    c = CHUNK
    m0 = _iota2((c, LANES), 1) < half
    row = _iota2((c, LANES), 0)
    sub = _iota2((c, LANES), 1) % c
    strict = sub < row
    incl = sub <= row
    rb = _iota2((tb, tb), 0)
    cb = _iota2((tb, tb), 1)
    chunk_tri = ((cb <= rb) & ((cb // c) == (rb // c))).astype(F32)
    gc_all = _mm_sel(chunk_tri, lw)
    gx_all = gc_all - lw

    def stack(x):
        return jnp.concatenate([jnp.where(m0, x, 0.0), jnp.where(m0, 0.0, x)], axis=0)

    def unstack(x):
        return jnp.where(m0, x[0:c], x[c:2 * c])

    items = [(pi, ci) for pi in range(npair) for ci in range(nchunk)]
    a_bds, ak_xs, wmats, a_sts, v_sws, r_abss, nvs, bes, decays = [], [], [], [], [], [], [], [], []
    for pi, ci in items:
        sl = (slice(ci * c, (ci + 1) * c), slice(pi * LANES, (pi + 1) * LANES))
        rr, kc, vc, aa, bb = r[sl], k2[sl], v[sl], aa_all[sl], bb_all[sl]
        gc, gx = gc_all[sl], gx_all[sl]
        g_mid = gc[c // 2 - 1:c // 2]
        g_last = gc[c - 1:c]
        e_out = jnp.exp(g_mid - gc)
        at = aa * jnp.exp(gx - g_mid)
        rt = rr * jnp.exp(gc - g_mid)
        bt = bb * e_out
        kt = kc * e_out
        lhs0 = jnp.concatenate([jnp.where(m0, at, 0.0), jnp.where(m0, rt, 0.0)], axis=0)
        lhs1 = jnp.concatenate([jnp.where(m0, 0.0, at), jnp.where(m0, 0.0, rt)], axis=0)
        res0 = _mm(lhs0, jnp.concatenate([bt, kt], axis=0), NT)
        res1 = _mm(lhs1, jnp.concatenate([kt, bt], axis=0), NT)
        top0 = jnp.where(strict, res0[0:c], 0.0)
        top1 = jnp.where(strict, res1[0:c], 0.0)
        a_bds.append(jnp.concatenate([jnp.where(m0, top0, 0.0), jnp.where(m0, 0.0, top1)], axis=0))
        ak_xs.append(jnp.concatenate([jnp.where(m0, 0.0, top0), jnp.where(m0, top1, 0.0)], axis=0))
        wmats.append(jnp.concatenate([jnp.where(incl, res0[c:2 * c], 0.0),
                                      jnp.where(incl, res1[c:2 * c], 0.0)], axis=0))
        a_sts.append(stack(aa * jnp.exp(gx)))
        v_sws.append(jnp.concatenate([jnp.where(m0, 0.0, vc), jnp.where(m0, vc, 0.0)], axis=0))
        r_abss.append(rr * jnp.exp(gc))
        e_end = jnp.exp(g_last - gc)
        bes.append(bb * e_end)
        nvs.append((vc, kc * e_end))
        decays.append(jnp.exp(g_last))
    tinvs = _neumann_inverses(a_bds)

    n_items = len(items)
    rhs_vs = [_mm(ak_xs[i], v_sws[i]) for i in range(n_items)]
    uws = [_mm(tinvs[i], jnp.concatenate([rhs_vs[i], a_sts[i]], axis=1)) for i in range(n_items)]
    yws = [_mm(wmats[i], jnp.concatenate([uws[i][:, 0:LANES] + v_sws[i], uws[i][:, LANES:2 * LANES]], axis=1))
           for i in range(n_items)]
    y0s = [unstack(yws[i][:, 0:LANES]) for i in range(n_items)]
    qps = [(unstack(yws[i][:, LANES:2 * LANES]) + r_abss[i]).astype(BF16) for i in range(n_items)]
    gbds = [jnp.where(block_diag, _mm(unstack(uws[i][:, LANES:2 * LANES]), bes[i], TN), 0.0).astype(BF16)
            for i in range(n_items)]
    ncs = [jnp.where(block_diag, _mm(jnp.concatenate([unstack(uws[i][:, 0:LANES]), nvs[i][0]], axis=0),
                                     jnp.concatenate([bes[i], nvs[i][1]], axis=0), TN), 0.0)
           for i in range(n_items)]

    hts = [st_ref[pi] for pi in range(npair)]
    ys = [[] for _ in range(npair)]
    for ci in range(nchunk):
        for pi in range(npair):
            i = pi * nchunk + ci
            htb = hts[pi].astype(BF16)
            ys[pi].append(y0s[i] + _mm(qps[i], htb, NT))
            hts[pi] = hts[pi] * decays[i] + _mm(htb, gbds[i]) + ncs[i]
    for pi in range(npair):
        st_ref[pi] = hts[pi]

    ycols = [jnp.concatenate(ys[pi], axis=0) for pi in range(npair)]
    y = ycols[0] if npair == 1 else jnp.concatenate(ycols, axis=1)
    mean = seg_sum(y) * (1.0 / half)
    yc = y - mean
    var = seg_sum(yc * yc) * (1.0 / half)
    yn = yc * lax.rsqrt(var + RWKV_GN_EPS) * gw_ref[...] + gb_ref[...]
    bonus = seg_sum(r * k2 * rk_ref[...]) * v
    o_ref[0] = ((yn + bonus) * gate).astype(o_ref.dtype)


def _rwkv(proj, small, mu_main, mu_small, w0, w2p, a0, a2p, g2, kk, ka, rk, gw, gb, lora, tb=256, npair=8):
    b, t, _ = proj.shape
    w = LANES * npair
    groups = GROUP_W // w
    ws = small.shape[2]
    glora = g2.shape[0]

    def col(section):
        return lambda bi, hi, ti: (bi, ti, section * groups + hi)

    def vec(section=0):
        return pl.BlockSpec((1, w), lambda bi, hi, ti: (0, section * groups + hi))

    row_blk = lambda s: pl.BlockSpec((1, tb, w), col(s))
    return pl.pallas_call(
        functools.partial(_rwkv_kernel, nchunk=tb // CHUNK, lora=lora),
        out_shape=jax.ShapeDtypeStruct((b, t, GROUP_W), BF16),
        grid=(b, groups, t // tb),
        in_specs=[row_blk(0), row_blk(1), row_blk(2),
                  pl.BlockSpec((1, tb, ws), lambda bi, hi, ti: (bi, ti, 0)),
                  vec(0), vec(1), vec(2),
                  pl.BlockSpec((1, ws), lambda bi, hi, ti: (0, 0)),
                  vec(), pl.BlockSpec((lora, w), lambda bi, hi, ti: (0, hi)),
                  vec(), pl.BlockSpec((lora, w), lambda bi, hi, ti: (0, hi)),
                  pl.BlockSpec((glora, w), lambda bi, hi, ti: (0, hi)),
                  vec(), vec(), vec(), vec(), vec()],
        out_specs=pl.BlockSpec((1, tb, w), lambda bi, hi, ti: (bi, ti, hi)),
        scratch_shapes=[pltpu.VMEM((npair, LANES, LANES), F32)]
        + [pltpu.VMEM((SUBLANES, w), F32)] * 3
        + [pltpu.VMEM((SUBLANES, ws), F32)],
        compiler_params=_cparams(("parallel", "parallel", "arbitrary")),
        name="rwkv7_chunk",
    )(proj, proj, proj, small, mu_main, mu_main, mu_main, mu_small, w0, w2p, a0, a2p, g2,
      kk, ka, rk, gw, gb)


def _pad_cols(a, width):
    return jnp.pad(a, ((0, 0), (0, width - a.shape[1])))


def _pad_rows(a, height):
    return jnp.pad(a, ((0, height - a.shape[0]), (0, 0)))


def _row(v):
    return v.reshape(1, -1).astype(F32)


def kernel(x, p, hgrn_lb_logits, e_w_in, e_gla_w2, e_gla_b, e_hgrn_norm, e_gla_norm, e_w_out, o_w_in, o_rwkv_mu, o_rwkv_w0, o_rwkv_w2, o_rwkv_a0, o_rwkv_a2, o_rwkv_g2, o_rwkv_kk, o_rwkv_ka, o_rwkv_rk, o_rwkv_gn_w, o_rwkv_gn_b, o_gdn_conv, o_gdn_a_log, o_gdn_dt_bias, o_gdn_norm, o_w_out, ln_mix_w, ln_mix_b, ln_ffn_w, ln_ffn_b, ffn_w_up, ffn_conv, ffn_w_down, ple_w_proj, ple_w_gate):
    bsz, seq, d = x.shape
    m = bsz * seq
    depth = ln_mix_w.shape[0]
    alpha = (2 * depth) ** 0.25
    gw = GROUP_W
    x32 = x.reshape(m, d)
    xb = x32.astype(BF16)
    w_down = ffn_w_down.astype(BF16)
    for layer in range(depth):
        j = layer // 2
        if layer % 2 == 0:
            rank = e_gla_w2.shape[1]
            gd0 = 4 * gw + 2 * GLA_HEADS * GLA_DK + gw
            wt = jnp.swapaxes(e_w_in, 1, 2)
            w_tail = jnp.concatenate([wt[j, gd0 + rank:], _pad_rows(wt[j, gd0:gd0 + rank], LANES)],
                                     axis=0)
            proj = _proj(xb, wt, gd0, layer=j).reshape(bsz, seq, -1)
            tail = _proj(xb, w_tail, w_tail.shape[0], tn=w_tail.shape[0]).reshape(bsz, seq, -1)
            o_a = _hgrn2(proj, hgrn_lb_logits.astype(F32), _row(e_hgrn_norm[j]), layer)
            o_b = _gla(proj, tail, _pad_rows(e_gla_w2[j], LANES), _row(e_gla_b[j]), _row(e_gla_norm[j]))
            w_out = e_w_out[j]
        else:
            wt = jnp.swapaxes(o_w_in, 1, 2)
            lora_w = o_rwkv_w2.shape[1]
            lora_a = o_rwkv_a2.shape[1]
            lora_g = o_rwkv_g2.shape[1]
            lora = LANES * (-(-max(lora_w, lora_a) // LANES))
            c_wd = 3 * gw
            c_ad = c_wd + lora_w
            c_gd = c_ad + lora_a
            c_qkv = c_gd + lora_g
            c_z = c_qkv + 3 * gw
            c_ab = c_z + gw
            w_small = jnp.concatenate([_pad_rows(wt[j, c_wd:c_ad], lora), _pad_rows(wt[j, c_ad:c_gd], lora),
                                       wt[j, c_gd:c_qkv], _pad_rows(wt[j, c_ab:], LANES)], axis=0)
            mu = o_rwkv_mu[j].reshape(1, -1)
            mu_small = jnp.concatenate([_pad_cols(mu[:, c_wd:c_ad], lora), _pad_cols(mu[:, c_ad:c_gd], lora),
                                        mu[:, c_gd:c_qkv], jnp.zeros((1, LANES), F32)], axis=1)
            proj_rkv = _proj(xb, wt, c_wd, layer=j).reshape(bsz, seq, -1)
            proj_gdn = _proj(xb, wt, c_ab - c_qkv, layer=j, row0=c_qkv).reshape(bsz, seq, -1)
            small = _proj(xb, w_small, w_small.shape[0], tn=w_small.shape[0]).reshape(bsz, seq, -1)
            o_a = _rwkv(proj_rkv, small, mu[:, :c_wd], mu_small, _row(o_rwkv_w0[j]),
                        _pad_rows(o_rwkv_w2[j], lora), _row(o_rwkv_a0[j]), _pad_rows(o_rwkv_a2[j], lora),
                        o_rwkv_g2[j], _row(o_rwkv_kk[j]), _row(o_rwkv_ka[j]), _row(o_rwkv_rk[j]),
                        _row(o_rwkv_gn_w[j]), _row(o_rwkv_gn_b[j]), lora)
            o_b = _gdn(proj_gdn, small, o_gdn_conv[j].astype(F32),
                       _pad_cols(_row(o_gdn_a_log[j]), LANES), _pad_cols(_row(o_gdn_dt_bias[j]), LANES),
                       _row(o_gdn_norm[j]))
            w_out = o_w_out[j]
        x32, xb = _mix_out(o_a.reshape(m, gw), o_b.reshape(m, gw), w_out.astype(BF16), x32,
                           _row(ln_mix_w[layer]), _row(ln_mix_b[layer]), alpha)
        h = _ffn_up(xb, ffn_w_up, ffn_conv, layer, seq)
        x32, xb = _ffn_down(h, w_down, x32, _row(ln_ffn_w[layer]), _row(ln_ffn_b[layer]), alpha, layer)
        x32, xb = _ple(xb, x32, p.reshape(depth, m, -1), ple_w_gate, ple_w_proj, layer)
    return x32.reshape(bsz, seq, d)
```

```python
import functools

import jax
import jax.numpy as jnp
from jax import lax
from jax.experimental import pallas as pl
from jax.experimental.pallas import tpu as pltpu

F32 = jnp.float32
BF16 = jnp.bfloat16

NN = (((1,), (0,)), ((), ()))
NT = (((1,), (1,)), ((), ()))
TN = (((0,), (0,)), ((), ()))

D_MODEL = 2048
GROUP_W = 1024
CHUNK = 64
LANES = 128
SUBLANES = 8
HGRN_HEADS = 8
GLA_HEADS = 4
GLA_DK = 128
GLA_DV = 256
GLA_TAU = 16.0
RWKV_HEAD = 64
RWKV_GN_EPS = 64e-5
GDN_HEADS = 8
GDN_HEAD = 128
FFN_DIM = 5632
LN_EPS = 1e-5
LN_SUBROWS = 128
RMS_EPS = 1e-6
L2_EPS = 1e-6

VMEM_LIMIT = 56 * 1024 * 1024


def _mm(a, b, dims=NN):
    return lax.dot_general(a.astype(BF16), b.astype(BF16), dims, preferred_element_type=F32)


def _split_bf16(x, parts):
    out = []
    for _ in range(parts - 1):
        h = x.astype(BF16)
        out.append(h)
        x = x - h.astype(F32)
    out.append(x.astype(BF16))
    return out


def _bdot(a, b, dims):
    return lax.dot_general(a, b, dims, preferred_element_type=F32)


def _mm_sel(sel, b, dims=NN):
    s = sel.astype(BF16)
    b1, b2, b3 = _split_bf16(b, 3)
    return (_bdot(s, b3, dims) + _bdot(s, b2, dims)) + _bdot(s, b1, dims)


def _mm_hi(a, b, dims=NN):
    ah, al = _split_bf16(a, 2)
    bh, bl = _split_bf16(b, 2)
    return (_bdot(al, bh, dims) + _bdot(ah, bl, dims)) + _bdot(ah, bh, dims)


def _sigmoid(x):
    return 1.0 / (1.0 + jnp.exp(-x))


def _silu(x):
    return x * _sigmoid(x)


def _softplus(x):
    return jnp.maximum(x, 0.0) + jnp.log(1.0 + jnp.exp(-jnp.abs(x)))


def _iota2(shape, dim):
    return lax.broadcasted_iota(jnp.int32, shape, dim)


def _cparams(sem):
    return pltpu.CompilerParams(dimension_semantics=sem, vmem_limit_bytes=VMEM_LIMIT)


def _proj_kernel(x_ref, w_ref, o_ref, wb_ref):
    @pl.when(pl.program_id(1) == 0)
    def _():
        w = w_ref[...] if len(w_ref.shape) == 2 else w_ref[0]
        wb_ref[...] = w.T.astype(BF16)

    o_ref[...] = lax.dot_general(x_ref[...], wb_ref[...], NN, preferred_element_type=F32).astype(o_ref.dtype)


def _proj(x, wt, n_cols, out_dtype=F32, layer=None, row0=0, tm=1024, tn=1024):
    m, k = x.shape
    tn = min(tn, n_cols)
    if layer is None:
        w_spec = pl.BlockSpec((tn, k), lambda j, i: (j, 0))
    elif row0 % tn == 0:
        w_spec = pl.BlockSpec((None, tn, k), lambda j, i: (layer, row0 // tn + j, 0))
    else:
        w_spec = pl.BlockSpec((pl.Element(1), pl.Element(tn), pl.Element(k)),
                              lambda j, i: (layer, pl.multiple_of(row0 + j * tn, SUBLANES), 0))
    return pl.pallas_call(
        _proj_kernel,
        out_shape=jax.ShapeDtypeStruct((m, n_cols), out_dtype),
        grid=(n_cols // tn, m // tm),
        in_specs=[pl.BlockSpec((tm, k), lambda j, i: (i, 0)), w_spec],
        out_specs=pl.BlockSpec((tm, tn), lambda j, i: (i, j)),
        scratch_shapes=[pltpu.VMEM((k, tn), BF16)],
        compiler_params=_cparams(("parallel", "arbitrary")),
        name="proj_matmul",
    )(x, wt)


def _layer_norm_rows(y, w, b):
    mu = jnp.mean(y, axis=-1, keepdims=True)
    yc = y - mu
    var = jnp.mean(yc * yc, axis=-1, keepdims=True)
    return yc * lax.rsqrt(var + LN_EPS) * w + b


def _mix_out_kernel(oa_ref, ob_ref, w_ref, x_ref, lw_ref, lb_ref, o32_ref, o16_ref, *, alpha):
    half = oa_ref.shape[1]
    for s in range(oa_ref.shape[0] // LN_SUBROWS):
        rows = slice(s * LN_SUBROWS, (s + 1) * LN_SUBROWS)
        acc = _mm(oa_ref[rows, :], w_ref[0:half, :]) + _mm(ob_ref[rows, :], w_ref[half:2 * half, :])
        y = _layer_norm_rows(alpha * x_ref[rows, :] + acc, lw_ref[...], lb_ref[...])
        o32_ref[rows, :] = y
        o16_ref[rows, :] = y.astype(BF16)


def _mix_out(oa, ob, w, x, lw, lb, alpha, tm=512):
    m, half = oa.shape
    n = w.shape[1]
    row = lambda i: (i, 0)
    fixed = lambda i: (0, 0)
    return pl.pallas_call(
        functools.partial(_mix_out_kernel, alpha=alpha),
        out_shape=(jax.ShapeDtypeStruct((m, n), F32), jax.ShapeDtypeStruct((m, n), BF16)),
        grid=(m // tm,),
        in_specs=[pl.BlockSpec((tm, half), row), pl.BlockSpec((tm, half), row),
                  pl.BlockSpec((2 * half, n), fixed), pl.BlockSpec((tm, n), row),
                  pl.BlockSpec((1, n), fixed), pl.BlockSpec((1, n), fixed)],
        out_specs=(pl.BlockSpec((tm, n), row), pl.BlockSpec((tm, n), row)),
        compiler_params=_cparams(("parallel",)),
        name="mix_out_ln",
    )(oa, ob, w, x, lw, lb)


def _ffn_up_kernel(x_ref, wg_ref, wv_ref, cg_ref, cv_ref, o_ref, wgb_ref, wvb_ref, hg_ref, hv_ref,
                   *, tiles_per_seq):
    i = pl.program_id(1)
    tm = x_ref.shape[0]

    @pl.when(i == 0)
    def _():
        wgb_ref[...] = wg_ref[...].astype(BF16)
        wvb_ref[...] = wv_ref[...].astype(BF16)
        hg_ref[...] = jnp.zeros_like(hg_ref)
        hv_ref[...] = jnp.zeros_like(hv_ref)

    first = (i % tiles_per_seq) == 0

    def conv(w_ref, h_ref, c_ref):
        u = lax.dot_general(x_ref[...], w_ref[...], NN, preferred_element_type=F32)
        halo = jnp.where(first, 0.0, h_ref[...])
        h_ref[...] = u[tm - SUBLANES:tm]
        ext = jnp.concatenate([halo, u], axis=0)
        p1 = pltpu.roll(ext, 1, 0)[SUBLANES:SUBLANES + tm]
        p2 = pltpu.roll(ext, 2, 0)[SUBLANES:SUBLANES + tm]
        c = c_ref[...]
        return u * c[2:3] + p1 * c[1:2] + p2 * c[0:1]

    g = conv(wgb_ref, hg_ref, cg_ref)
    v = conv(wvb_ref, hv_ref, cv_ref)
    o_ref[...] = (_silu(g) * v).astype(o_ref.dtype)


def _ffn_up(xb, w_up, conv_w, layer, seq, tm=1024, tf=512):
    m, k = xb.shape
    f = w_up.shape[2] // 2
    nf = f // tf
    return pl.pallas_call(
        functools.partial(_ffn_up_kernel, tiles_per_seq=seq // tm),
        out_shape=jax.ShapeDtypeStruct((m, f), BF16),
        grid=(nf, m // tm),
        in_specs=[pl.BlockSpec((tm, k), lambda j, i: (i, 0)),
                  pl.BlockSpec((None, k, tf), lambda j, i: (layer, 0, j)),
                  pl.BlockSpec((None, k, tf), lambda j, i: (layer, 0, nf + j)),
                  pl.BlockSpec((None, 3, tf), lambda j, i: (layer, 0, j)),
                  pl.BlockSpec((None, 3, tf), lambda j, i: (layer, 0, nf + j))],
        out_specs=pl.BlockSpec((tm, tf), lambda j, i: (i, j)),
        scratch_shapes=[pltpu.VMEM((k, tf), BF16)] * 2 + [pltpu.VMEM((SUBLANES, tf), F32)] * 2,
        compiler_params=_cparams(("parallel", "arbitrary")),
        name="ffn_up_conv_gate",
    )(xb, w_up, w_up, conv_w, conv_w)


def _ffn_down_kernel(h_ref, w_ref, x_ref, lw_ref, lb_ref, o32_ref, o16_ref, acc_ref, *, alpha, nk):
    k = pl.program_id(1)

    @pl.when(k == 0)
    def _():
        acc_ref[...] = _mm(h_ref[...], w_ref[...])

    @pl.when((k > 0) & (k < nk - 1))
    def _():
        acc_ref[...] += _mm(h_ref[...], w_ref[...])

    @pl.when(k == nk - 1)
    def _():
        for s in range(h_ref.shape[0] // LN_SUBROWS):
            rows = slice(s * LN_SUBROWS, (s + 1) * LN_SUBROWS)
            acc = acc_ref[rows, :] + _mm(h_ref[rows, :], w_ref[...])
            y = _layer_norm_rows(alpha * x_ref[rows, :] + acc, lw_ref[...], lb_ref[...])
            o32_ref[rows, :] = y
            o16_ref[rows, :] = y.astype(BF16)


def _ffn_down(h, w, x, lw, lb, alpha, layer, tm=512, tk=2816):
    m, kdim = h.shape
    n = w.shape[2]
    nk = kdim // tk
    assert nk >= 2
    return pl.pallas_call(
        functools.partial(_ffn_down_kernel, alpha=alpha, nk=nk),
        out_shape=(jax.ShapeDtypeStruct((m, n), F32), jax.ShapeDtypeStruct((m, n), BF16)),
        grid=(m // tm, nk),
        in_specs=[pl.BlockSpec((tm, tk), lambda i, k: (i, k)),
                  pl.BlockSpec((None, tk, n), lambda i, k: (layer, k, 0)),
                  pl.BlockSpec((tm, n), lambda i, k: (i, 0)),
                  pl.BlockSpec((1, n), lambda i, k: (0, 0)),
                  pl.BlockSpec((1, n), lambda i, k: (0, 0))],
        out_specs=(pl.BlockSpec((tm, n), lambda i, k: (i, 0)),
                   pl.BlockSpec((tm, n), lambda i, k: (i, 0))),
        scratch_shapes=[pltpu.VMEM((tm, n), F32)],
        compiler_params=_cparams(("parallel", "arbitrary")),
        name="ffn_down_ln",
    )(h, w, x, lw, lb)


def _ple_kernel(xb_ref, wg_ref, p_ref, wp_ref, x_ref, o32_ref, o16_ref, wgb_ref, wpb_ref):
    @pl.when(pl.program_id(0) == 0)
    def _():
        wgb_ref[...] = wg_ref[...].astype(BF16)
        wpb_ref[...] = wp_ref[...].astype(BF16)

    for s in range(xb_ref.shape[0] // LN_SUBROWS):
        rows = slice(s * LN_SUBROWS, (s + 1) * LN_SUBROWS)
        gate = _sigmoid(_mm(xb_ref[rows, :], wgb_ref[...]))
        y = x_ref[rows, :] + gate * _mm(p_ref[rows, :], wpb_ref[...])
        o32_ref[rows, :] = y
        o16_ref[rows, :] = y.astype(BF16)


def _ple(xb, x, p, wg, wp, layer, tm=512):
    m, k = xb.shape
    n = wg.shape[2]
    kp = p.shape[2]
    once = pl.Buffered(1)
    return pl.pallas_call(
        _ple_kernel,
        out_shape=(jax.ShapeDtypeStruct((m, n), F32), jax.ShapeDtypeStruct((m, n), BF16)),
        grid=(m // tm,),
        in_specs=[pl.BlockSpec((tm, k), lambda i: (i, 0)),
                  pl.BlockSpec((None, k, n), lambda i: (layer, 0, 0), pipeline_mode=once),
                  pl.BlockSpec((None, tm, kp), lambda i: (layer, i, 0)),
                  pl.BlockSpec((None, kp, n), lambda i: (layer, 0, 0), pipeline_mode=once),
                  pl.BlockSpec((tm, n), lambda i: (i, 0))],
        out_specs=(pl.BlockSpec((tm, n), lambda i: (i, 0)),
                   pl.BlockSpec((tm, n), lambda i: (i, 0))),
        scratch_shapes=[pltpu.VMEM((k, n), BF16), pltpu.VMEM((kp, n), BF16)],
        compiler_params=_cparams(("arbitrary",)),
        name="ple_gate",
    )(xb, wg, p, wp, x)


def _chunk_tri(tb):
    rb = _iota2((tb, tb), 0)
    cb = _iota2((tb, tb), 1)
    return ((cb <= rb) & ((cb // CHUNK) == (rb // CHUNK))).astype(F32)


def _gla_block(q, k, g, v, st_ref, heads, dk, dv):
    tb = q.shape[0]
    c = CHUNK
    causal = _iota2((c, c), 1) <= _iota2((c, c), 0)
    gc_all = _mm_sel(_chunk_tri(tb), g)
    nchunk = tb // c
    items = [(h, ci) for h in range(heads) for ci in range(nchunk)]

    def part(x, h, ci, d):
        return x[ci * c:(ci + 1) * c, h * d:(h + 1) * d]

    gcs = [part(gc_all, h, ci, dk) for h, ci in items]
    scores = [_mm(part(q, h, ci, dk) * jnp.exp(gc - gc[c // 2:c // 2 + 1]),
                  part(k, h, ci, dk) * jnp.exp(gc[c // 2:c // 2 + 1] - gc), NT)
              for (h, ci), gc in zip(items, gcs)]
    intra = [_mm(jnp.where(causal, a, 0.0), part(v, h, ci, dv)) for (h, ci), a in zip(items, scores)]
    incs = [_mm(part(v, h, ci, dv), part(k, h, ci, dk) * jnp.exp(gc[c - 1:c] - gc), TN)
            for (h, ci), gc in zip(items, gcs)]
    qgs = [(part(q, h, ci, dk) * jnp.exp(gc)).astype(BF16) for (h, ci), gc in zip(items, gcs)]
    sts = [st_ref[h] for h in range(heads)]
    o_h = [[] for _ in range(heads)]
    for ci in range(nchunk):
        for h in range(heads):
            i = h * nchunk + ci
            o_h[h].append(intra[i] + _mm(qgs[i], sts[h], NT))
            sts[h] = sts[h] * jnp.exp(gcs[i][c - 1:c]) + incs[i]
    for h in range(heads):
        st_ref[h] = sts[h]
    return [jnp.concatenate(o, axis=0) for o in o_h]


def _head_rms(o, gain, gate):
    o = o * lax.rsqrt(jnp.mean(o * o, axis=-1, keepdims=True) + RMS_EPS)
    return o * gain * _silu(gate)


def _hgrn2_kernel(q_ref, f_ref, i_ref, g_ref, lg_ref, nw_ref, o_ref, st_ref, *, layer, heads, nchunk):
    @pl.when(pl.program_id(2) == 0)
    def _():
        st_ref[...] = jnp.zeros_like(st_ref)

    lg = lg_ref[...]
    e = jnp.exp(lg - jnp.max(lg, axis=0, keepdims=True))
    lb = jnp.sum(e[0:layer + 1], axis=0, keepdims=True) / jnp.sum(e, axis=0, keepdims=True)

    f = lb + (1.0 - lb) * _sigmoid(f_ref[0])
    outs = _gla_block(_silu(q_ref[0]), 1.0 - f, jnp.log(f), i_ref[0], st_ref, heads, LANES, LANES)
    for h in range(heads):
        cols = slice(h * LANES, (h + 1) * LANES)
        o_ref[0, :, cols] = _head_rms(outs[h], nw_ref[:, cols], g_ref[0, :, cols]).astype(o_ref.dtype)


def _hgrn2(proj, logits, norm_w, layer, tb=256, heads=8):
    b, t, _ = proj.shape
    w = LANES * heads
    per = GROUP_W // w

    def col(section):
        return lambda bi, hi, ti: (bi, ti, section * per + hi)

    return pl.pallas_call(
        functools.partial(_hgrn2_kernel, layer=layer, heads=heads, nchunk=tb // CHUNK),
        out_shape=jax.ShapeDtypeStruct((b, t, GROUP_W), BF16),
        grid=(b, per, t // tb),
        in_specs=[pl.BlockSpec((1, tb, w), col(0)), pl.BlockSpec((1, tb, w), col(1)),
                  pl.BlockSpec((1, tb, w), col(2)), pl.BlockSpec((1, tb, w), col(3)),
                  pl.BlockSpec((logits.shape[0], w), lambda bi, hi, ti: (0, hi)),
                  pl.BlockSpec((1, w), lambda bi, hi, ti: (0, hi))],
        out_specs=pl.BlockSpec((1, tb, w), lambda bi, hi, ti: (bi, ti, hi)),
        scratch_shapes=[pltpu.VMEM((heads, LANES, LANES), F32)],
        compiler_params=_cparams(("parallel", "parallel", "arbitrary")),
        name="hgrn2_chunk",
    )(proj, proj, proj, proj, logits, norm_w)


def _gla_kernel(q_ref, k_ref, v_ref, r_ref, gd_ref, w2_ref, b_ref, nw_ref, o_ref, st_ref, *, heads):
    @pl.when(pl.program_id(2) == 0)
    def _():
        st_ref[...] = jnp.zeros_like(st_ref)

    z = _mm_hi(gd_ref[0], w2_ref[...]) + b_ref[...]
    g = -_softplus(-z) * (1.0 / GLA_TAU)
    outs = _gla_block(q_ref[0] * GLA_DK ** -0.5, k_ref[0], g, v_ref[0], st_ref, heads, GLA_DK, GLA_DV)
    for h in range(heads):
        cols = slice(h * GLA_DV, (h + 1) * GLA_DV)
        o_ref[0, :, cols] = _head_rms(outs[h], nw_ref[:, cols], r_ref[0, :, cols]).astype(o_ref.dtype)


def _gla(proj, tail, w2, bias, norm_w, tb=256, heads=4):
    b, t, _ = proj.shape
    wk = GLA_DK * heads
    wv = GLA_DV * heads
    groups = GLA_HEADS // heads
    qk0 = 4 * GROUP_W // wk
    v0 = (4 * GROUP_W + 2 * GLA_HEADS * GLA_DK) // wv
    gd_blk = GROUP_W // LANES
    return pl.pallas_call(
        functools.partial(_gla_kernel, heads=heads),
        out_shape=jax.ShapeDtypeStruct((b, t, GROUP_W), BF16),
        grid=(b, groups, t // tb),
        in_specs=[pl.BlockSpec((1, tb, wk), lambda bi, hi, ti: (bi, ti, qk0 + hi)),
                  pl.BlockSpec((1, tb, wk), lambda bi, hi, ti: (bi, ti, qk0 + groups + hi)),
                  pl.BlockSpec((1, tb, wv), lambda bi, hi, ti: (bi, ti, v0 + hi)),
                  pl.BlockSpec((1, tb, wv), lambda bi, hi, ti: (bi, ti, hi)),
                  pl.BlockSpec((1, tb, LANES), lambda bi, hi, ti: (bi, ti, gd_blk)),
                  pl.BlockSpec((LANES, wk), lambda bi, hi, ti: (0, hi)),
                  pl.BlockSpec((1, wk), lambda bi, hi, ti: (0, hi)),
                  pl.BlockSpec((1, wv), lambda bi, hi, ti: (0, hi))],
        out_specs=pl.BlockSpec((1, tb, wv), lambda bi, hi, ti: (bi, ti, hi)),
        scratch_shapes=[pltpu.VMEM((heads, GLA_DV, GLA_DK), F32)],
        compiler_params=_cparams(("parallel", "parallel", "arbitrary")),
        name="gla_chunk",
    )(proj, proj, proj, tail, tail, w2, bias, norm_w)


def _neumann_inverses(mats):
    n = mats[0].shape[0]
    pack = 2 if (2 * n <= LANES and len(mats) % 2 == 0) else 1
    width = pack * n
    eye = (_iota2((n, width), 0) == _iota2((n, width), 1) % n).astype(F32)
    left = _iota2((n, width), 1) < n

    def rhs(x):
        if pack == 1:
            return x
        return jnp.concatenate([jnp.where(left, x, 0.0), jnp.where(left, 0.0, x)], axis=0)

    if pack == 2:
        mats = [jnp.concatenate([mats[i], mats[i + 1]], axis=1) for i in range(0, len(mats), 2)]
    ts = [eye + a for a in mats]
    ps = [_mm(a, rhs(a)) for a in mats]
    steps = max(1, (CHUNK - 1).bit_length() - 1)
    for _ in range(steps - 1):
        both = [_mm(jnp.concatenate([t, p], axis=0), rhs(p)) for t, p in zip(ts, ps)]
        ts = [t + b[0:n] for t, b in zip(ts, both)]
        ps = [b[n:2 * n] for b in both]
    ts = [t + _mm(t, rhs(p)) for t, p in zip(ts, ps)]
    if pack == 1:
        return ts
    out = []
    for t in ts:
        out += [t[:, 0:n], t[:, n:2 * n]]
    return out


def _shifted(x, carry_ref, shift):
    tb = x.shape[0]
    ext = jnp.concatenate([carry_ref[...], x], axis=0)
    return pltpu.roll(ext, shift, 0)[SUBLANES:SUBLANES + tb]


def _gdn_kernel(q_ref, k_ref, v_ref, z_ref, ab_ref, cq_ref, ck_ref, cv_ref, al_ref, dt_ref, nw_ref,
                o_ref, st_ref, cq_s, ck_s, cv_s, *, nchunk, heads):
    head0 = pl.program_id(1) * heads

    @pl.when(pl.program_id(2) == 0)
    def _():
        st_ref[...] = jnp.zeros_like(st_ref)
        cq_s[...] = jnp.zeros_like(cq_s)
        ck_s[...] = jnp.zeros_like(ck_s)
        cv_s[...] = jnp.zeros_like(cv_s)

    tb = q_ref.shape[1]

    def conv_silu(x_ref, carry, w_ref):
        x = x_ref[0]
        w = w_ref[...]
        y = x * w[3:4]
        for j in (1, 2, 3):
            y = y + _shifted(x, carry, j) * w[3 - j:4 - j]
        carry[...] = x[tb - SUBLANES:tb]
        return _silu(y)

    def l2n(x):
        return x * lax.rsqrt(jnp.sum(x * x, axis=-1, keepdims=True) + L2_EPS)

    q_all = conv_silu(q_ref, cq_s, cq_ref)
    k_all = conv_silu(k_ref, ck_s, ck_ref)
    v_all = conv_silu(v_ref, cv_s, cv_ref)

    c = CHUNK
    incl = _iota2((c, c), 1) <= _iota2((c, c), 0)
    strict = _iota2((c, c), 1) < _iota2((c, c), 0)
    rb = _iota2((tb, tb), 0)
    cb = _iota2((tb, tb), 1)
    chunk_tri = ((cb <= rb) & ((cb // c) == (rb // c))).astype(F32)
    lane = _iota2((tb, LANES), 1)
    ab = ab_ref[0]
    gmat = -jnp.exp(al_ref[...]) * _softplus(ab + dt_ref[...])
    gcum = _mm_sel(chunk_tri, gmat)
    items = [(h, ci) for h in range(heads) for ci in range(nchunk)]
    per_head = []
    for h in range(heads):
        cols = slice(h * GDN_HEAD, (h + 1) * GDN_HEAD)
        q = l2n(q_all[:, cols]) * GDN_HEAD ** -0.5
        k = l2n(k_all[:, cols])
        g_col = jnp.sum(jnp.where(lane == head0 + h, gcum, 0.0), axis=1, keepdims=True)
        beta = _sigmoid(jnp.sum(jnp.where(lane == head0 + h + GDN_HEADS, ab, 0.0), axis=1, keepdims=True))
        eg = jnp.exp(g_col)
        kb = k * beta
        per_head.append(dict(q=q, k=k, g_col=g_col, kb=kb, qg=q * eg,
                             rhs=jnp.concatenate([v_all[:, cols] * beta, kb * eg], axis=1),
                             pick=(_iota2((c, LANES), 1) == head0 + h).astype(F32)))

    def rows(name, h, ci):
        return per_head[h][name][ci * c:(ci + 1) * c]

    g_rows = [_mm_sel(per_head[h]["pick"], gcum[ci * c:(ci + 1) * c], NT) for h, ci in items]
    kks = [_mm(rows("kb", h, ci), rows("k", h, ci), NT) for h, ci in items]
    qks = [_mm(rows("q", h, ci), rows("k", h, ci), NT) for h, ci in items]
    gams = [jnp.where(incl, jnp.exp(jnp.where(incl, rows("g_col", h, ci) - gr, 0.0)), 0.0)
            for (h, ci), gr in zip(items, g_rows)]
    lows = [jnp.where(strict, kk * gam, 0.0) for kk, gam in zip(kks, gams)]
    aqks = [jnp.where(incl, qk * gam, 0.0).astype(BF16) for qk, gam in zip(qks, gams)]
    g_lasts = [rows("g_col", h, ci)[c - 1:c] for h, ci in items]
    kgts = [(rows("k", h, ci) * jnp.exp(gl - rows("g_col", h, ci))).T.astype(BF16)
            for (h, ci), gl in zip(items, g_lasts)]
    decays = [jnp.exp(gl) for gl in g_lasts]
    tinvs = _neumann_inverses([-low for low in lows])
    uws = [_mm(tinvs[i], per_head[h]["rhs"][ci * c:(ci + 1) * c]) for i, (h, ci) in enumerate(items)]
    ows = [_mm(aqks[i], uws[i]) for i in range(len(items))]
    kws = [_mm(kgts[i], uws[i]) for i in range(len(items))]
    o0s = [ow[:, 0:GDN_HEAD] for ow in ows]
    qps = [(per_head[h]["qg"][ci * c:(ci + 1) * c] - ows[i][:, GDN_HEAD:2 * GDN_HEAD]).astype(BF16)
           for i, (h, ci) in enumerate(items)]
    nts = [kw[:, 0:GDN_HEAD] for kw in kws]
    mts = [(-kw[:, GDN_HEAD:2 * GDN_HEAD]).astype(BF16) for kw in kws]

    states = [st_ref[h] for h in range(heads)]
    outs = [[] for _ in range(heads)]
    for ci in range(nchunk):
        for h in range(heads):
            i = h * nchunk + ci
            sb = states[h].astype(BF16)
            outs[h].append(o0s[i] + _mm(qps[i], sb))
            states[h] = states[h] * decays[i] + _mm(mts[i], sb) + nts[i]
    for h in range(heads):
        cols = slice(h * GDN_HEAD, (h + 1) * GDN_HEAD)
        st_ref[h] = states[h]
        o = jnp.concatenate(outs[h], axis=0)
        o_ref[0, :, cols] = _head_rms(o, nw_ref[:, cols], z_ref[0, :, cols]).astype(o_ref.dtype)


def _gdn(proj, small, conv_w, alog_pad, dt_pad, norm_w, tb=256, heads=8):
    b, t, _ = proj.shape
    w = GDN_HEAD * heads
    groups = GDN_HEADS // heads
    base = 0
    ab_blk = (small.shape[2] - LANES) // LANES

    def col(section):
        return lambda bi, hi, ti: (bi, ti, base + section * groups + hi)

    def cw(section):
        return pl.BlockSpec((4, w), lambda bi, hi, ti: (0, section * groups + hi))

    return pl.pallas_call(
        functools.partial(_gdn_kernel, nchunk=tb // CHUNK, heads=heads),
        out_shape=jax.ShapeDtypeStruct((b, t, GROUP_W), BF16),
        grid=(b, groups, t // tb),
        in_specs=[pl.BlockSpec((1, tb, w), col(0)), pl.BlockSpec((1, tb, w), col(1)),
                  pl.BlockSpec((1, tb, w), col(2)), pl.BlockSpec((1, tb, w), col(3)),
                  pl.BlockSpec((1, tb, LANES), lambda bi, hi, ti: (bi, ti, ab_blk)),
                  cw(0), cw(1), cw(2),
                  pl.BlockSpec((1, LANES), lambda bi, hi, ti: (0, 0)),
                  pl.BlockSpec((1, LANES), lambda bi, hi, ti: (0, 0)),
                  pl.BlockSpec((1, w), lambda bi, hi, ti: (0, hi))],
        out_specs=pl.BlockSpec((1, tb, w), lambda bi, hi, ti: (bi, ti, hi)),
        scratch_shapes=[pltpu.VMEM((heads, GDN_HEAD, GDN_HEAD), F32)]
        + [pltpu.VMEM((SUBLANES, w), F32)] * 3,
        compiler_params=_cparams(("parallel", "parallel", "arbitrary")),
        name="gdn_chunk",
    )(proj, proj, proj, proj, small, conv_w, conv_w, conv_w, alog_pad, dt_pad, norm_w)


def _rwkv_kernel(r_ref, k_ref, v_ref, sm_ref, mur_ref, muk_ref, muv_ref, mus_ref, w0_ref, w2_ref, a0_ref,
                 a2_ref, g2_ref, kk_ref, ka_ref, rk_ref, gw_ref, gb_ref, o_ref,
                 st_ref, cr_s, ck_s, cv_s, cs_s, *, nchunk, lora):
    @pl.when(pl.program_id(2) == 0)
    def _():
        for ref in (st_ref, cr_s, ck_s, cv_s, cs_s):
            ref[...] = jnp.zeros_like(ref)

    tb = r_ref.shape[1]
    half = RWKV_HEAD
    block_diag = (_iota2((LANES, LANES), 0) // half) == (_iota2((LANES, LANES), 1) // half)
    ones_bd = block_diag.astype(BF16)

    def seg_sum(x):
        hi = x.astype(BF16)
        lo = (x - hi.astype(F32)).astype(BF16)
        parts = []
        for j in range(x.shape[1] // LANES):
            cols = slice(j * LANES, (j + 1) * LANES)
            parts.append(lax.dot_general(hi[:, cols], ones_bd, NN, preferred_element_type=F32)
                         + lax.dot_general(lo[:, cols], ones_bd, NN, preferred_element_type=F32))
        return parts[0] if len(parts) == 1 else jnp.concatenate(parts, axis=1)

    def token_shift(x, carry, mu):
        prev = _shifted(x, carry, 1)
        carry[...] = x[tb - SUBLANES:tb]
        return x + (prev - x) * mu

    r = token_shift(r_ref[0], cr_s, mur_ref[...])
    k = token_shift(k_ref[0], ck_s, muk_ref[...])
    v = token_shift(v_ref[0], cv_s, muv_ref[...])
    sm = token_shift(sm_ref[0], cs_s, mus_ref[...])
    wd = sm[:, 0:lora]
    ad = sm[:, lora:2 * lora]
    gd = sm[:, 2 * lora:2 * lora + g2_ref.shape[0]]
    z = w0_ref[...] + _mm_hi(jnp.tanh(wd), w2_ref[...])
    lw = -jnp.exp(-_softplus(-z) - 0.5)
    ag = _sigmoid(a0_ref[...] + _mm_hi(ad, a2_ref[...]))
    gate = _mm(_sigmoid(gd), g2_ref[...])
    kk = k * kk_ref[...]
    kk = kk * lax.rsqrt(seg_sum(kk * kk) + L2_EPS)
    k2 = k * (1.0 + (ag - 1.0) * ka_ref[...])
    aa_all = -kk
    bb_all = kk * ag
    npair = r.shape[1] // LANES

    c = CHUNK
    m0 = _iota2((c, LANES), 1) < half
    row = _iota2((c, LANES), 0)
    sub = _iota2((c, LANES), 1) % c
    strict = sub < row
    incl = sub <= row
    rb = _iota2((tb, tb), 0)
    cb = _iota2((tb, tb), 1)
    chunk_tri = ((cb <= rb) & ((cb // c) == (rb // c))).astype(F32)
    gc_all = _mm_sel(chunk_tri, lw)
    gx_all = gc_all - lw

    def stack(x):
        return jnp.concatenate([jnp.where(m0, x, 0.0), jnp.where(m0, 0.0, x)], axis=0)

    def unstack(x):
        return jnp.where(m0, x[0:c], x[c:2 * c])

    items = [(pi, ci) for pi in range(npair) for ci in range(nchunk)]
    a_bds, ak_xs, wmats, a_sts, v_sws, r_abss, nvs, bes, decays = [], [], [], [], [], [], [], [], []
    for pi, ci in items:
        sl = (slice(ci * c, (ci + 1) * c), slice(pi * LANES, (pi + 1) * LANES))
        rr, kc, vc, aa, bb = r[sl], k2[sl], v[sl], aa_all[sl], bb_all[sl]
        gc, gx = gc_all[sl], gx_all[sl]
        g_mid = gc[c // 2 - 1:c // 2]
        g_last = gc[c - 1:c]
        e_out = jnp.exp(g_mid - gc)
        at = aa * jnp.exp(gx - g_mid)
        rt = rr * jnp.exp(gc - g_mid)
        bt = bb * e_out
        kt = kc * e_out
        lhs0 = jnp.concatenate([jnp.where(m0, at, 0.0), jnp.where(m0, rt, 0.0)], axis=0)
        lhs1 = jnp.concatenate([jnp.where(m0, 0.0, at), jnp.where(m0, 0.0, rt)], axis=0)
        res0 = _mm(lhs0, jnp.concatenate([bt, kt], axis=0), NT)
        res1 = _mm(lhs1, jnp.concatenate([kt, bt], axis=0), NT)
        top0 = jnp.where(strict, res0[0:c], 0.0)
        top1 = jnp.where(strict, res1[0:c], 0.0)
        a_bds.append(jnp.concatenate([jnp.where(m0, top0, 0.0), jnp.where(m0, 0.0, top1)], axis=0))
        ak_xs.append(jnp.concatenate([jnp.where(m0, 0.0, top0), jnp.where(m0, top1, 0.0)], axis=0))
        wmats.append(jnp.concatenate([jnp.where(incl, res0[c:2 * c], 0.0),
                                      jnp.where(incl, res1[c:2 * c], 0.0)], axis=0))
        a_sts.append(stack(aa * jnp.exp(gx)))
        v_sws.append(jnp.concatenate([jnp.where(m0, 0.0, vc), jnp.where(m0, vc, 0.0)], axis=0))
        r_abss.append(rr * jnp.exp(gc))
        e_end = jnp.exp(g_last - gc)
        bes.append(bb * e_end)
        nvs.append((vc, kc * e_end))
        decays.append(jnp.exp(g_last))
    tinvs = _neumann_inverses(a_bds)

    n_items = len(items)
    rhs_vs = [_mm(ak_xs[i], v_sws[i]) for i in range(n_items)]
    uws = [_mm(tinvs[i], jnp.concatenate([rhs_vs[i], a_sts[i]], axis=1)) for i in range(n_items)]
    yws = [_mm(wmats[i], jnp.concatenate([uws[i][:, 0:LANES] + v_sws[i], uws[i][:, LANES:2 * LANES]], axis=1))
           for i in range(n_items)]
    y0s = [unstack(yws[i][:, 0:LANES]) for i in range(n_items)]
    qps = [(unstack(yws[i][:, LANES:2 * LANES]) + r_abss[i]).astype(BF16) for i in range(n_items)]
    gbds = [jnp.where(block_diag, _mm(unstack(uws[i][:, LANES:2 * LANES]), bes[i], TN), 0.0).astype(BF16)
            for i in range(n_items)]
    ncs = [jnp.where(block_diag, _mm(jnp.concatenate([unstack(uws[i][:, 0:LANES]), nvs[i][0]], axis=0),
                                     jnp.concatenate([bes[i], nvs[i][1]], axis=0), TN), 0.0)
           for i in range(n_items)]

    hts = [st_ref[pi] for pi in range(npair)]
    ys = [[] for _ in range(npair)]
    for ci in range(nchunk):
        for pi in range(npair):
            i = pi * nchunk + ci
            htb = hts[pi].astype(BF16)
            ys[pi].append(y0s[i] + _mm(qps[i], htb, NT))
            hts[pi] = hts[pi] * decays[i] + _mm(htb, gbds[i]) + ncs[i]
    for pi in range(npair):
        st_ref[pi] = hts[pi]

    ycols = [jnp.concatenate(ys[pi], axis=0) for pi in range(npair)]
    y = ycols[0] if npair == 1 else jnp.concatenate(ycols, axis=1)
    mean = seg_sum(y) * (1.0 / half)
    yc = y - mean
    var = seg_sum(yc * yc) * (1.0 / half)
    yn = yc * lax.rsqrt(var + RWKV_GN_EPS) * gw_ref[...] + gb_ref[...]
    bonus = seg_sum(r * k2 * rk_ref[...]) * v
    o_ref[0] = ((yn + bonus) * gate).astype(o_ref.dtype)


def _rwkv(proj, small, mu_main, mu_small, w0, w2p, a0, a2p, g2, kk, ka, rk, gw, gb, lora, tb=256, npair=8):
    b, t, _ = proj.shape
    w = LANES * npair
    groups = GROUP_W // w
    ws = small.shape[2]
    glora = g2.shape[0]

    def col(section):
        return lambda bi, hi, ti: (bi, ti, section * groups + hi)

    def vec(section=0):
        return pl.BlockSpec((1, w), lambda bi, hi, ti: (0, section * groups + hi))

    row_blk = lambda s: pl.BlockSpec((1, tb, w), col(s))
    return pl.pallas_call(
        functools.partial(_rwkv_kernel, nchunk=tb // CHUNK, lora=lora),
        out_shape=jax.ShapeDtypeStruct((b, t, GROUP_W), BF16),
        grid=(b, groups, t // tb),
        in_specs=[row_blk(0), row_blk(1), row_blk(2),
                  pl.BlockSpec((1, tb, ws), lambda bi, hi, ti: (bi, ti, 0)),
                  vec(0), vec(1), vec(2),
                  pl.BlockSpec((1, ws), lambda bi, hi, ti: (0, 0)),
                  vec(), pl.BlockSpec((lora, w), lambda bi, hi, ti: (0, hi)),
                  vec(), pl.BlockSpec((lora, w), lambda bi, hi, ti: (0, hi)),
                  pl.BlockSpec((glora, w), lambda bi, hi, ti: (0, hi)),
                  vec(), vec(), vec(), vec(), vec()],
        out_specs=pl.BlockSpec((1, tb, w), lambda bi, hi, ti: (bi, ti, hi)),
        scratch_shapes=[pltpu.VMEM((npair, LANES, LANES), F32)]
        + [pltpu.VMEM((SUBLANES, w), F32)] * 3
        + [pltpu.VMEM((SUBLANES, ws), F32)],
        compiler_params=_cparams(("parallel", "parallel", "arbitrary")),
        name="rwkv7_chunk",
    )(proj, proj, proj, small, mu_main, mu_main, mu_main, mu_small, w0, w2p, a0, a2p, g2,
      kk, ka, rk, gw, gb)


def _pad_cols(a, width):
    return jnp.pad(a, ((0, 0), (0, width - a.shape[1])))


def _pad_rows(a, height):
    return jnp.pad(a, ((0, height - a.shape[0]), (0, 0)))


def _row(v):
    return v.reshape(1, -1).astype(F32)


def kernel(x, p, hgrn_lb_logits, e_w_in, e_gla_w2, e_gla_b, e_hgrn_norm, e_gla_norm, e_w_out, o_w_in, o_rwkv_mu, o_rwkv_w0, o_rwkv_w2, o_rwkv_a0, o_rwkv_a2, o_rwkv_g2, o_rwkv_kk, o_rwkv_ka, o_rwkv_rk, o_rwkv_gn_w, o_rwkv_gn_b, o_gdn_conv, o_gdn_a_log, o_gdn_dt_bias, o_gdn_norm, o_w_out, ln_mix_w, ln_mix_b, ln_ffn_w, ln_ffn_b, ffn_w_up, ffn_conv, ffn_w_down, ple_w_proj, ple_w_gate):
    bsz, seq, d = x.shape
    m = bsz * seq
    depth = ln_mix_w.shape[0]
    alpha = (2 * depth) ** 0.25
    gw = GROUP_W
    x32 = x.reshape(m, d)
    xb = x32.astype(BF16)
    w_down = ffn_w_down.astype(BF16)
    for layer in range(depth):
        j = layer // 2
        if layer % 2 == 0:
            rank = e_gla_w2.shape[1]
            gd0 = 4 * gw + 2 * GLA_HEADS * GLA_DK + gw
            wt = jnp.swapaxes(e_w_in, 1, 2)
            w_tail = jnp.concatenate([wt[j, gd0 + rank:], _pad_rows(wt[j, gd0:gd0 + rank], LANES)],
                                     axis=0)
            proj = _proj(xb, wt, gd0, layer=j).reshape(bsz, seq, -1)
            tail = _proj(xb, w_tail, w_tail.shape[0], tn=w_tail.shape[0]).reshape(bsz, seq, -1)
            o_a = _hgrn2(proj, hgrn_lb_logits.astype(F32), _row(e_hgrn_norm[j]), layer)
            o_b = _gla(proj, tail, _pad_rows(e_gla_w2[j], LANES), _row(e_gla_b[j]), _row(e_gla_norm[j]))
            w_out = e_w_out[j]
        else:
            wt = jnp.swapaxes(o_w_in, 1, 2)
            lora_w = o_rwkv_w2.shape[1]
            lora_a = o_rwkv_a2.shape[1]
            lora_g = o_rwkv_g2.shape[1]
            lora = LANES * (-(-max(lora_w, lora_a) // LANES))
            c_wd = 3 * gw
            c_ad = c_wd + lora_w
            c_gd = c_ad + lora_a
            c_qkv = c_gd + lora_g
            c_z = c_qkv + 3 * gw
            c_ab = c_z + gw
            w_small = jnp.concatenate([_pad_rows(wt[j, c_wd:c_ad], lora), _pad_rows(wt[j, c_ad:c_gd], lora),
                                       wt[j, c_gd:c_qkv], _pad_rows(wt[j, c_ab:], LANES)], axis=0)
            mu = o_rwkv_mu[j].reshape(1, -1)
            mu_small = jnp.concatenate([_pad_cols(mu[:, c_wd:c_ad], lora), _pad_cols(mu[:, c_ad:c_gd], lora),
                                        mu[:, c_gd:c_qkv], jnp.zeros((1, LANES), F32)], axis=1)
            proj_rkv = _proj(xb, wt, c_wd, layer=j).reshape(bsz, seq, -1)
            proj_gdn = _proj(xb, wt, c_ab - c_qkv, layer=j, row0=c_qkv).reshape(bsz, seq, -1)
            small = _proj(xb, w_small, w_small.shape[0], tn=w_small.shape[0]).reshape(bsz, seq, -1)
            o_a = _rwkv(proj_rkv, small, mu[:, :c_wd], mu_small, _row(o_rwkv_w0[j]),
                        _pad_rows(o_rwkv_w2[j], lora), _row(o_rwkv_a0[j]), _pad_rows(o_rwkv_a2[j], lora),
                        o_rwkv_g2[j], _row(o_rwkv_kk[j]), _row(o_rwkv_ka[j]), _row(o_rwkv_rk[j]),
                        _row(o_rwkv_gn_w[j]), _row(o_rwkv_gn_b[j]), lora)
            o_b = _gdn(proj_gdn, small, o_gdn_conv[j].astype(F32),
                       _pad_cols(_row(o_gdn_a_log[j]), LANES), _pad_cols(_row(o_gdn_dt_bias[j]), LANES),
                       _row(o_gdn_norm[j]))
            w_out = o_w_out[j]
        x32, xb = _mix_out(o_a.reshape(m, gw), o_b.reshape(m, gw), w_out.astype(BF16), x32,
                           _row(ln_mix_w[layer]), _row(ln_mix_b[layer]), alpha)
        h = _ffn_up(xb, ffn_w_up, ffn_conv, layer, seq)
        x32, xb = _ffn_down(h, w_down, x32, _row(ln_ffn_w[layer]), _row(ln_ffn_b[layer]), alpha, layer)
        x32, xb = _ple(xb, x32, p.reshape(depth, m, -1), ple_w_gate, ple_w_proj, layer)
    return x32.reshape(bsz, seq, d)
```

```python
import functools

import jax
import jax.numpy as jnp
from jax import lax
from jax.experimental import pallas as pl
from jax.experimental.pallas import tpu as pltpu

F32 = jnp.float32
BF16 = jnp.bfloat16

NN = (((1,), (0,)), ((), ()))
NT = (((1,), (1,)), ((), ()))
TN = (((0,), (0,)), ((), ()))

D_MODEL = 2048
GROUP_W = 1024
CHUNK = 64
LANES = 128
SUBLANES = 8
HGRN_HEADS = 8
GLA_HEADS = 4
GLA_DK = 128
GLA_DV = 256
GLA_TAU = 16.0
RWKV_HEAD = 64
RWKV_GN_EPS = 64e-5
GDN_HEADS = 8
GDN_HEAD = 128
FFN_DIM = 5632
LN_EPS = 1e-5
LN_SUBROWS = 128
RMS_EPS = 1e-6
L2_EPS = 1e-6

VMEM_LIMIT = 56 * 1024 * 1024


def _mm(a, b, dims=NN):
    return lax.dot_general(a.astype(BF16), b.astype(BF16), dims, preferred_element_type=F32)


def _split_bf16(x, parts):
    out = []
    for _ in range(parts - 1):
        h = x.astype(BF16)
        out.append(h)
        x = x - h.astype(F32)
    out.append(x.astype(BF16))
    return out


def _bdot(a, b, dims):
    return lax.dot_general(a, b, dims, preferred_element_type=F32)


def _mm_sel(sel, b, dims=NN):
    s = sel.astype(BF16)
    b1, b2, b3 = _split_bf16(b, 3)
    return (_bdot(s, b3, dims) + _bdot(s, b2, dims)) + _bdot(s, b1, dims)


def _mm_hi(a, b, dims=NN):
    ah, al = _split_bf16(a, 2)
    bh, bl = _split_bf16(b, 2)
    return (_bdot(al, bh, dims) + _bdot(ah, bl, dims)) + _bdot(ah, bh, dims)


def _sigmoid(x):
    return 1.0 / (1.0 + jnp.exp(-x))


def _silu(x):
    return x * _sigmoid(x)


def _softplus(x):
    return jnp.maximum(x, 0.0) + jnp.log(1.0 + jnp.exp(-jnp.abs(x)))


def _iota2(shape, dim):
    return lax.broadcasted_iota(jnp.int32, shape, dim)


def _cparams(sem):
    return pltpu.CompilerParams(dimension_semantics=sem, vmem_limit_bytes=VMEM_LIMIT)


def _proj_kernel(x_ref, w_ref, o_ref, wb_ref):
    @pl.when(pl.program_id(1) == 0)
    def _():
        w = w_ref[...] if len(w_ref.shape) == 2 else w_ref[0]
        wb_ref[...] = w.T.astype(BF16)

    o_ref[...] = lax.dot_general(x_ref[...], wb_ref[...], NN, preferred_element_type=F32).astype(o_ref.dtype)


def _proj(x, wt, n_cols, out_dtype=F32, layer=None, row0=0, tm=1024, tn=1024):
    m, k = x.shape
    tn = min(tn, n_cols)
    if layer is None:
        w_spec = pl.BlockSpec((tn, k), lambda j, i: (j, 0))
    elif row0 % tn == 0:
        w_spec = pl.BlockSpec((None, tn, k), lambda j, i: (layer, row0 // tn + j, 0))
    else:
        w_spec = pl.BlockSpec((pl.Element(1), pl.Element(tn), pl.Element(k)),
                              lambda j, i: (layer, pl.multiple_of(row0 + j * tn, SUBLANES), 0))
    return pl.pallas_call(
        _proj_kernel,
        out_shape=jax.ShapeDtypeStruct((m, n_cols), out_dtype),
        grid=(n_cols // tn, m // tm),
        in_specs=[pl.BlockSpec((tm, k), lambda j, i: (i, 0)), w_spec],
        out_specs=pl.BlockSpec((tm, tn), lambda j, i: (i, j)),
        scratch_shapes=[pltpu.VMEM((k, tn), BF16)],
        compiler_params=_cparams(("parallel", "arbitrary")),
        name="proj_matmul",
    )(x, wt)


def _layer_norm_rows(y, w, b):
    mu = jnp.mean(y, axis=-1, keepdims=True)
    yc = y - mu
    var = jnp.mean(yc * yc, axis=-1, keepdims=True)
    return yc * lax.rsqrt(var + LN_EPS) * w + b


def _mix_out_kernel(oa_ref, ob_ref, w_ref, x_ref, lw_ref, lb_ref, o32_ref, o16_ref, wb_ref, *, alpha):
    @pl.when(pl.program_id(0) == 0)
    def _():
        wb_ref[...] = w_ref[...].astype(BF16)

    half = oa_ref.shape[1]
    for s in range(oa_ref.shape[0] // LN_SUBROWS):
        rows = slice(s * LN_SUBROWS, (s + 1) * LN_SUBROWS)
        acc = _mm(oa_ref[rows, :], wb_ref[0:half, :]) + _mm(ob_ref[rows, :], wb_ref[half:2 * half, :])
        y = _layer_norm_rows(alpha * x_ref[rows, :] + acc, lw_ref[...], lb_ref[...])
        o32_ref[rows, :] = y
        o16_ref[rows, :] = y.astype(BF16)


def _mix_out(oa, ob, w, x, lw, lb, alpha, layer, tm=512):
    m, half = oa.shape
    n = w.shape[2]
    row = lambda i: (i, 0)
    fixed = lambda i: (0, 0)
    return pl.pallas_call(
        functools.partial(_mix_out_kernel, alpha=alpha),
        out_shape=(jax.ShapeDtypeStruct((m, n), F32), jax.ShapeDtypeStruct((m, n), BF16)),
        grid=(m // tm,),
        in_specs=[pl.BlockSpec((tm, half), row), pl.BlockSpec((tm, half), row),
                  pl.BlockSpec((None, 2 * half, n), lambda i: (layer, 0, 0), pipeline_mode=pl.Buffered(1)),
                  pl.BlockSpec((tm, n), row),
                  pl.BlockSpec((1, n), fixed), pl.BlockSpec((1, n), fixed)],
        out_specs=(pl.BlockSpec((tm, n), row), pl.BlockSpec((tm, n), row)),
        scratch_shapes=[pltpu.VMEM((2 * half, n), BF16)],
        compiler_params=_cparams(("arbitrary",)),
        name="mix_out_ln",
    )(oa, ob, w, x, lw, lb)


def _ffn_up_kernel(x_ref, wg_ref, wv_ref, cg_ref, cv_ref, o_ref, wgb_ref, wvb_ref, hg_ref, hv_ref,
                   *, tiles_per_seq):
    i = pl.program_id(1)
    tm = x_ref.shape[0]

    @pl.when(i == 0)
    def _():
        wgb_ref[...] = wg_ref[...].astype(BF16)
        wvb_ref[...] = wv_ref[...].astype(BF16)
        hg_ref[...] = jnp.zeros_like(hg_ref)
        hv_ref[...] = jnp.zeros_like(hv_ref)

    first = (i % tiles_per_seq) == 0

    def conv(w_ref, h_ref, c_ref):
        u = lax.dot_general(x_ref[...], w_ref[...], NN, preferred_element_type=F32)
        halo = jnp.where(first, 0.0, h_ref[...])
        h_ref[...] = u[tm - SUBLANES:tm]
        ext = jnp.concatenate([halo, u], axis=0)
        p1 = pltpu.roll(ext, 1, 0)[SUBLANES:SUBLANES + tm]
        p2 = pltpu.roll(ext, 2, 0)[SUBLANES:SUBLANES + tm]
        c = c_ref[...]
        return u * c[2:3] + p1 * c[1:2] + p2 * c[0:1]

    g = conv(wgb_ref, hg_ref, cg_ref)
    v = conv(wvb_ref, hv_ref, cv_ref)
    o_ref[...] = (_silu(g) * v).astype(o_ref.dtype)


def _ffn_up(xb, w_up, conv_w, layer, seq, tm=1024, tf=512):
    m, k = xb.shape
    f = w_up.shape[2] // 2
    nf = f // tf
    return pl.pallas_call(
        functools.partial(_ffn_up_kernel, tiles_per_seq=seq // tm),
        out_shape=jax.ShapeDtypeStruct((m, f), BF16),
        grid=(nf, m // tm),
        in_specs=[pl.BlockSpec((tm, k), lambda j, i: (i, 0)),
                  pl.BlockSpec((None, k, tf), lambda j, i: (layer, 0, j)),
                  pl.BlockSpec((None, k, tf), lambda j, i: (layer, 0, nf + j)),
                  pl.BlockSpec((None, 3, tf), lambda j, i: (layer, 0, j)),
                  pl.BlockSpec((None, 3, tf), lambda j, i: (layer, 0, nf + j))],
        out_specs=pl.BlockSpec((tm, tf), lambda j, i: (i, j)),
        scratch_shapes=[pltpu.VMEM((k, tf), BF16)] * 2 + [pltpu.VMEM((SUBLANES, tf), F32)] * 2,
        compiler_params=_cparams(("parallel", "arbitrary")),
        name="ffn_up_conv_gate",
    )(xb, w_up, w_up, conv_w, conv_w)


def _ffn_down_kernel(h_ref, w_ref, x_ref, lw_ref, lb_ref, o32_ref, o16_ref, acc_ref, *, alpha, nk):
    k = pl.program_id(1)

    @pl.when(k == 0)
    def _():
        acc_ref[...] = _mm(h_ref[...], w_ref[...])

    @pl.when((k > 0) & (k < nk - 1))
    def _():
        acc_ref[...] += _mm(h_ref[...], w_ref[...])

    @pl.when(k == nk - 1)
    def _():
        for s in range(h_ref.shape[0] // LN_SUBROWS):
            rows = slice(s * LN_SUBROWS, (s + 1) * LN_SUBROWS)
            acc = acc_ref[rows, :] + _mm(h_ref[rows, :], w_ref[...])
            y = _layer_norm_rows(alpha * x_ref[rows, :] + acc, lw_ref[...], lb_ref[...])
            o32_ref[rows, :] = y
            o16_ref[rows, :] = y.astype(BF16)


def _ffn_down(h, w, x, lw, lb, alpha, layer, tm=512, tk=2816):
    m, kdim = h.shape
    n = w.shape[2]
    nk = kdim // tk
    assert nk >= 2
    return pl.pallas_call(
        functools.partial(_ffn_down_kernel, alpha=alpha, nk=nk),
        out_shape=(jax.ShapeDtypeStruct((m, n), F32), jax.ShapeDtypeStruct((m, n), BF16)),
        grid=(m // tm, nk),
        in_specs=[pl.BlockSpec((tm, tk), lambda i, k: (i, k)),
                  pl.BlockSpec((None, tk, n), lambda i, k: (layer, k, 0)),
                  pl.BlockSpec((tm, n), lambda i, k: (i, 0)),
                  pl.BlockSpec((1, n), lambda i, k: (0, 0)),
                  pl.BlockSpec((1, n), lambda i, k: (0, 0))],
        out_specs=(pl.BlockSpec((tm, n), lambda i, k: (i, 0)),
                   pl.BlockSpec((tm, n), lambda i, k: (i, 0))),
        scratch_shapes=[pltpu.VMEM((tm, n), F32)],
        compiler_params=_cparams(("parallel", "arbitrary")),
        name="ffn_down_ln",
    )(h, w, x, lw, lb)


def _ple_kernel(xb_ref, wg_ref, p_ref, wp_ref, x_ref, o32_ref, o16_ref, wgb_ref, wpb_ref):
    @pl.when(pl.program_id(0) == 0)
    def _():
        wgb_ref[...] = wg_ref[...].astype(BF16)
        wpb_ref[...] = wp_ref[...].astype(BF16)

    for s in range(xb_ref.shape[0] // LN_SUBROWS):
        rows = slice(s * LN_SUBROWS, (s + 1) * LN_SUBROWS)
        gate = _sigmoid(_mm(xb_ref[rows, :], wgb_ref[...]))
        y = x_ref[rows, :] + gate * _mm(p_ref[rows, :], wpb_ref[...])
        o32_ref[rows, :] = y
        o16_ref[rows, :] = y.astype(BF16)


def _ple(xb, x, p, wg, wp, layer, tm=512):
    m, k = xb.shape
    n = wg.shape[2]
    kp = p.shape[2]
    once = pl.Buffered(1)
    return pl.pallas_call(
        _ple_kernel,
        out_shape=(jax.ShapeDtypeStruct((m, n), F32), jax.ShapeDtypeStruct((m, n), BF16)),
        grid=(m // tm,),
        in_specs=[pl.BlockSpec((tm, k), lambda i: (i, 0)),
                  pl.BlockSpec((None, k, n), lambda i: (layer, 0, 0), pipeline_mode=once),
                  pl.BlockSpec((None, tm, kp), lambda i: (layer, i, 0)),
                  pl.BlockSpec((None, kp, n), lambda i: (layer, 0, 0), pipeline_mode=once),
                  pl.BlockSpec((tm, n), lambda i: (i, 0))],
        out_specs=(pl.BlockSpec((tm, n), lambda i: (i, 0)),
                   pl.BlockSpec((tm, n), lambda i: (i, 0))),
        scratch_shapes=[pltpu.VMEM((k, n), BF16), pltpu.VMEM((kp, n), BF16)],
        compiler_params=_cparams(("arbitrary",)),
        name="ple_gate",
    )(xb, wg, p, wp, x)


def _chunk_tri(tb):
    rb = _iota2((tb, tb), 0)
    cb = _iota2((tb, tb), 1)
    return ((cb <= rb) & ((cb // CHUNK) == (rb // CHUNK))).astype(F32)


def _gla_block(q, k, g, v, st_ref, heads, dk, dv):
    tb = q.shape[0]
    c = CHUNK
    causal = _iota2((c, c), 1) <= _iota2((c, c), 0)
    gc_all = _mm_sel(_chunk_tri(tb), g)
    nchunk = tb // c
    items = [(h, ci) for h in range(heads) for ci in range(nchunk)]

    def part(x, h, ci, d):
        return x[ci * c:(ci + 1) * c, h * d:(h + 1) * d]

    gcs = [part(gc_all, h, ci, dk) for h, ci in items]
    scores = [_mm(part(q, h, ci, dk) * jnp.exp(gc - gc[c // 2:c // 2 + 1]),
                  part(k, h, ci, dk) * jnp.exp(gc[c // 2:c // 2 + 1] - gc), NT)
              for (h, ci), gc in zip(items, gcs)]
    intra = [_mm(jnp.where(causal, a, 0.0), part(v, h, ci, dv)) for (h, ci), a in zip(items, scores)]
    incs = [_mm(part(v, h, ci, dv), part(k, h, ci, dk) * jnp.exp(gc[c - 1:c] - gc), TN)
            for (h, ci), gc in zip(items, gcs)]
    qgs = [(part(q, h, ci, dk) * jnp.exp(gc)).astype(BF16) for (h, ci), gc in zip(items, gcs)]
    sts = [st_ref[h] for h in range(heads)]
    o_h = [[] for _ in range(heads)]
    for ci in range(nchunk):
        for h in range(heads):
            i = h * nchunk + ci
            o_h[h].append(intra[i] + _mm(qgs[i], sts[h], NT))
            sts[h] = sts[h] * jnp.exp(gcs[i][c - 1:c]) + incs[i]
    for h in range(heads):
        st_ref[h] = sts[h]
    return [jnp.concatenate(o, axis=0) for o in o_h]


def _head_rms(o, gain, gate):
    o = o * lax.rsqrt(jnp.mean(o * o, axis=-1, keepdims=True) + RMS_EPS)
    return o * gain * _silu(gate)


def _hgrn2_kernel(q_ref, f_ref, i_ref, g_ref, lg_ref, nw_ref, o_ref, st_ref, *, layer, heads, nchunk):
    @pl.when(pl.program_id(2) == 0)
    def _():
        st_ref[...] = jnp.zeros_like(st_ref)

    lg = lg_ref[...]
    e = jnp.exp(lg - jnp.max(lg, axis=0, keepdims=True))
    lb = jnp.sum(e[0:layer + 1], axis=0, keepdims=True) / jnp.sum(e, axis=0, keepdims=True)

    f = lb + (1.0 - lb) * _sigmoid(f_ref[0])
    outs = _gla_block(_silu(q_ref[0]), 1.0 - f, jnp.log(f), i_ref[0], st_ref, heads, LANES, LANES)
    for h in range(heads):
        cols = slice(h * LANES, (h + 1) * LANES)
        o_ref[0, :, cols] = _head_rms(outs[h], nw_ref[:, cols], g_ref[0, :, cols]).astype(o_ref.dtype)


def _hgrn2(proj, logits, norm_w, layer, tb=256, heads=8):
    b, t, _ = proj.shape
    w = LANES * heads
    per = GROUP_W // w

    def col(section):
        return lambda bi, hi, ti: (bi, ti, section * per + hi)

    return pl.pallas_call(
        functools.partial(_hgrn2_kernel, layer=layer, heads=heads, nchunk=tb // CHUNK),
        out_shape=jax.ShapeDtypeStruct((b, t, GROUP_W), BF16),
        grid=(b, per, t // tb),
        in_specs=[pl.BlockSpec((1, tb, w), col(0)), pl.BlockSpec((1, tb, w), col(1)),
                  pl.BlockSpec((1, tb, w), col(2)), pl.BlockSpec((1, tb, w), col(3)),
                  pl.BlockSpec((logits.shape[0], w), lambda bi, hi, ti: (0, hi)),
                  pl.BlockSpec((1, w), lambda bi, hi, ti: (0, hi))],
        out_specs=pl.BlockSpec((1, tb, w), lambda bi, hi, ti: (bi, ti, hi)),
        scratch_shapes=[pltpu.VMEM((heads, LANES, LANES), F32)],
        compiler_params=_cparams(("parallel", "parallel", "arbitrary")),
        name="hgrn2_chunk",
    )(proj, proj, proj, proj, logits, norm_w)


def _gla_kernel(q_ref, k_ref, v_ref, r_ref, gd_ref, w2_ref, b_ref, nw_ref, o_ref, st_ref, *, heads):
    @pl.when(pl.program_id(2) == 0)
    def _():
        st_ref[...] = jnp.zeros_like(st_ref)

    z = _mm_hi(gd_ref[0], w2_ref[...]) + b_ref[...]
    g = -_softplus(-z) * (1.0 / GLA_TAU)
    outs = _gla_block(q_ref[0] * GLA_DK ** -0.5, k_ref[0], g, v_ref[0], st_ref, heads, GLA_DK, GLA_DV)
    for h in range(heads):
        cols = slice(h * GLA_DV, (h + 1) * GLA_DV)
        o_ref[0, :, cols] = _head_rms(outs[h], nw_ref[:, cols], r_ref[0, :, cols]).astype(o_ref.dtype)


def _gla(proj, tail, w2, bias, norm_w, tb=256, heads=4):
    b, t, _ = proj.shape
    wk = GLA_DK * heads
    wv = GLA_DV * heads
    groups = GLA_HEADS // heads
    qk0 = 4 * GROUP_W // wk
    v0 = (4 * GROUP_W + 2 * GLA_HEADS * GLA_DK) // wv
    gd_blk = GROUP_W // LANES
    return pl.pallas_call(
        functools.partial(_gla_kernel, heads=heads),
        out_shape=jax.ShapeDtypeStruct((b, t, GROUP_W), BF16),
        grid=(b, groups, t // tb),
        in_specs=[pl.BlockSpec((1, tb, wk), lambda bi, hi, ti: (bi, ti, qk0 + hi)),
                  pl.BlockSpec((1, tb, wk), lambda bi, hi, ti: (bi, ti, qk0 + groups + hi)),
                  pl.BlockSpec((1, tb, wv), lambda bi, hi, ti: (bi, ti, v0 + hi)),
                  pl.BlockSpec((1, tb, wv), lambda bi, hi, ti: (bi, ti, hi)),
                  pl.BlockSpec((1, tb, LANES), lambda bi, hi, ti: (bi, ti, gd_blk)),
                  pl.BlockSpec((LANES, wk), lambda bi, hi, ti: (0, hi)),
                  pl.BlockSpec((1, wk), lambda bi, hi, ti: (0, hi)),
                  pl.BlockSpec((1, wv), lambda bi, hi, ti: (0, hi))],
        out_specs=pl.BlockSpec((1, tb, wv), lambda bi, hi, ti: (bi, ti, hi)),
        scratch_shapes=[pltpu.VMEM((heads, GLA_DV, GLA_DK), F32)],
        compiler_params=_cparams(("parallel", "parallel", "arbitrary")),
        name="gla_chunk",
    )(proj, proj, proj, tail, tail, w2, bias, norm_w)


def _neumann_inverses(mats):
    n = mats[0].shape[0]
    pack = 2 if (2 * n <= LANES and len(mats) % 2 == 0) else 1
    width = pack * n
    eye = (_iota2((n, width), 0) == _iota2((n, width), 1) % n).astype(F32)
    left = _iota2((n, width), 1) < n

    def rhs(x):
        if pack == 1:
            return x
        return jnp.concatenate([jnp.where(left, x, 0.0), jnp.where(left, 0.0, x)], axis=0)

    if pack == 2:
        mats = [jnp.concatenate([mats[i], mats[i + 1]], axis=1) for i in range(0, len(mats), 2)]
    ts = [eye + a for a in mats]
    ps = [_mm(a, rhs(a)) for a in mats]
    steps = max(1, (CHUNK - 1).bit_length() - 1)
    for _ in range(steps - 1):
        both = [_mm(jnp.concatenate([t, p], axis=0), rhs(p)) for t, p in zip(ts, ps)]
        ts = [t + b[0:n] for t, b in zip(ts, both)]
        ps = [b[n:2 * n] for b in both]
    ts = [t + _mm(t, rhs(p)) for t, p in zip(ts, ps)]
    if pack == 1:
        return ts
    out = []
    for t in ts:
        out += [t[:, 0:n], t[:, n:2 * n]]
    return out


def _shifted(x, carry_ref, shift):
    tb = x.shape[0]
    ext = jnp.concatenate([carry_ref[...], x], axis=0)
    return pltpu.roll(ext, shift, 0)[SUBLANES:SUBLANES + tb]


def _gdn_kernel(q_ref, k_ref, v_ref, z_ref, ab_ref, cq_ref, ck_ref, cv_ref, al_ref, dt_ref, nw_ref,
                o_ref, st_ref, cq_s, ck_s, cv_s, *, nchunk, heads):
    head0 = pl.program_id(1) * heads

    @pl.when(pl.program_id(2) == 0)
    def _():
        st_ref[...] = jnp.zeros_like(st_ref)
        cq_s[...] = jnp.zeros_like(cq_s)
        ck_s[...] = jnp.zeros_like(ck_s)
        cv_s[...] = jnp.zeros_like(cv_s)

    tb = q_ref.shape[1]

    def conv_silu(x_ref, carry, w_ref):
        x = x_ref[0]
        w = w_ref[...]
        y = x * w[3:4]
        for j in (1, 2, 3):
            y = y + _shifted(x, carry, j) * w[3 - j:4 - j]
        carry[...] = x[tb - SUBLANES:tb]
        return _silu(y)

    def l2n(x):
        return x * lax.rsqrt(jnp.sum(x * x, axis=-1, keepdims=True) + L2_EPS)

    q_all = conv_silu(q_ref, cq_s, cq_ref)
    k_all = conv_silu(k_ref, ck_s, ck_ref)
    v_all = conv_silu(v_ref, cv_s, cv_ref)

    c = CHUNK
    incl = _iota2((c, c), 1) <= _iota2((c, c), 0)
    strict = _iota2((c, c), 1) < _iota2((c, c), 0)
    rb = _iota2((tb, tb), 0)
    cb = _iota2((tb, tb), 1)
    chunk_tri = ((cb <= rb) & ((cb // c) == (rb // c))).astype(F32)
    lane = _iota2((tb, LANES), 1)
    ab = ab_ref[0]
    gmat = -jnp.exp(al_ref[...]) * _softplus(ab + dt_ref[...])
    gcum = _mm_sel(chunk_tri, gmat)
    items = [(h, ci) for h in range(heads) for ci in range(nchunk)]
    per_head = []
    for h in range(heads):
        cols = slice(h * GDN_HEAD, (h + 1) * GDN_HEAD)
        q = l2n(q_all[:, cols]) * GDN_HEAD ** -0.5
        k = l2n(k_all[:, cols])
        g_col = jnp.sum(jnp.where(lane == head0 + h, gcum, 0.0), axis=1, keepdims=True)
        beta = _sigmoid(jnp.sum(jnp.where(lane == head0 + h + GDN_HEADS, ab, 0.0), axis=1, keepdims=True))
        eg = jnp.exp(g_col)
        kb = k * beta
        per_head.append(dict(q=q, k=k, g_col=g_col, kb=kb, qg=q * eg,
                             rhs=jnp.concatenate([v_all[:, cols] * beta, kb * eg], axis=1),
                             pick=(_iota2((c, LANES), 1) == head0 + h).astype(F32)))

    def rows(name, h, ci):
        return per_head[h][name][ci * c:(ci + 1) * c]

    g_rows = [_mm_sel(per_head[h]["pick"], gcum[ci * c:(ci + 1) * c], NT) for h, ci in items]
    kks = [_mm(rows("kb", h, ci), rows("k", h, ci), NT) for h, ci in items]
    qks = [_mm(rows("q", h, ci), rows("k", h, ci), NT) for h, ci in items]
    gams = [jnp.where(incl, jnp.exp(jnp.where(incl, rows("g_col", h, ci) - gr, 0.0)), 0.0)
            for (h, ci), gr in zip(items, g_rows)]
    lows = [jnp.where(strict, kk * gam, 0.0) for kk, gam in zip(kks, gams)]
    aqks = [jnp.where(incl, qk * gam, 0.0).astype(BF16) for qk, gam in zip(qks, gams)]
    g_lasts = [rows("g_col", h, ci)[c - 1:c] for h, ci in items]
    kgts = [(rows("k", h, ci) * jnp.exp(gl - rows("g_col", h, ci))).T.astype(BF16)
            for (h, ci), gl in zip(items, g_lasts)]
    decays = [jnp.exp(gl) for gl in g_lasts]
    tinvs = _neumann_inverses([-low for low in lows])
    uws = [_mm(tinvs[i], per_head[h]["rhs"][ci * c:(ci + 1) * c]) for i, (h, ci) in enumerate(items)]
    ows = [_mm(aqks[i], uws[i]) for i in range(len(items))]
    kws = [_mm(kgts[i], uws[i]) for i in range(len(items))]
    o0s = [ow[:, 0:GDN_HEAD] for ow in ows]
    qps = [(per_head[h]["qg"][ci * c:(ci + 1) * c] - ows[i][:, GDN_HEAD:2 * GDN_HEAD]).astype(BF16)
           for i, (h, ci) in enumerate(items)]
    nts = [kw[:, 0:GDN_HEAD] for kw in kws]
    mts = [(-kw[:, GDN_HEAD:2 * GDN_HEAD]).astype(BF16) for kw in kws]

    states = [st_ref[h] for h in range(heads)]
    outs = [[] for _ in range(heads)]
    for ci in range(nchunk):
        for h in range(heads):
            i = h * nchunk + ci
            sb = states[h].astype(BF16)
            outs[h].append(o0s[i] + _mm(qps[i], sb))
            states[h] = states[h] * decays[i] + _mm(mts[i], sb) + nts[i]
    for h in range(heads):
        cols = slice(h * GDN_HEAD, (h + 1) * GDN_HEAD)
        st_ref[h] = states[h]
        o = jnp.concatenate(outs[h], axis=0)
        o_ref[0, :, cols] = _head_rms(o, nw_ref[:, cols], z_ref[0, :, cols]).astype(o_ref.dtype)


def _gdn(proj, small, conv_w, alog_pad, dt_pad, norm_w, tb=256, heads=8):
    b, t, _ = proj.shape
    w = GDN_HEAD * heads
    groups = GDN_HEADS // heads
    base = 0
    ab_blk = (small.shape[2] - LANES) // LANES

    def col(section):
        return lambda bi, hi, ti: (bi, ti, base + section * groups + hi)

    def cw(section):
        return pl.BlockSpec((4, w), lambda bi, hi, ti: (0, section * groups + hi))

    return pl.pallas_call(
        functools.partial(_gdn_kernel, nchunk=tb // CHUNK, heads=heads),
        out_shape=jax.ShapeDtypeStruct((b, t, GROUP_W), BF16),
        grid=(b, groups, t // tb),
        in_specs=[pl.BlockSpec((1, tb, w), col(0)), pl.BlockSpec((1, tb, w), col(1)),
                  pl.BlockSpec((1, tb, w), col(2)), pl.BlockSpec((1, tb, w), col(3)),
                  pl.BlockSpec((1, tb, LANES), lambda bi, hi, ti: (bi, ti, ab_blk)),
                  cw(0), cw(1), cw(2),
                  pl.BlockSpec((1, LANES), lambda bi, hi, ti: (0, 0)),
                  pl.BlockSpec((1, LANES), lambda bi, hi, ti: (0, 0)),
                  pl.BlockSpec((1, w), lambda bi, hi, ti: (0, hi))],
        out_specs=pl.BlockSpec((1, tb, w), lambda bi, hi, ti: (bi, ti, hi)),
        scratch_shapes=[pltpu.VMEM((heads, GDN_HEAD, GDN_HEAD), F32)]
        + [pltpu.VMEM((SUBLANES, w), F32)] * 3,
        compiler_params=_cparams(("parallel", "parallel", "arbitrary")),
        name="gdn_chunk",
    )(proj, proj, proj, proj, small, conv_w, conv_w, conv_w, alog_pad, dt_pad, norm_w)


def _rwkv_kernel(r_ref, k_ref, v_ref, sm_ref, mur_ref, muk_ref, muv_ref, mus_ref, w0_ref, w2_ref, a0_ref,
                 a2_ref, g2_ref, kk_ref, ka_ref, rk_ref, gw_ref, gb_ref, o_ref,
                 st_ref, cr_s, ck_s, cv_s, cs_s, *, nchunk, lora):
    @pl.when(pl.program_id(2) == 0)
    def _():
        for ref in (st_ref, cr_s, ck_s, cv_s, cs_s):
            ref[...] = jnp.zeros_like(ref)

    tb = r_ref.shape[1]
    half = RWKV_HEAD
    block_diag = (_iota2((LANES, LANES), 0) // half) == (_iota2((LANES, LANES), 1) // half)
    ones_bd = block_diag.astype(BF16)

    def seg_sum(x):
        hi = x.astype(BF16)
        lo = (x - hi.astype(F32)).astype(BF16)
        parts = []
        for j in range(x.shape[1] // LANES):
            cols = slice(j * LANES, (j + 1) * LANES)
            parts.append(lax.dot_general(hi[:, cols], ones_bd, NN, preferred_element_type=F32)
                         + lax.dot_general(lo[:, cols], ones_bd, NN, preferred_element_type=F32))
        return parts[0] if len(parts) == 1 else jnp.concatenate(parts, axis=1)

    def token_shift(x, carry, mu):
        prev = _shifted(x, carry, 1)
        carry[...] = x[tb - SUBLANES:tb]
        return x + (prev - x) * mu

    r = token_shift(r_ref[0], cr_s, mur_ref[...])
    k = token_shift(k_ref[0], ck_s, muk_ref[...])
    v = token_shift(v_ref[0], cv_s, muv_ref[...])
    sm = token_shift(sm_ref[0], cs_s, mus_ref[...])
    wd = sm[:, 0:lora]
    ad = sm[:, lora:2 * lora]
    gd = sm[:, 2 * lora:2 * lora + g2_ref.shape[0]]
    z = w0_ref[...] + _mm_hi(jnp.tanh(wd), w2_ref[...])
    lw = -jnp.exp(-_softplus(-z) - 0.5)
    ag = _sigmoid(a0_ref[...] + _mm_hi(ad, a2_ref[...]))
    gate = _mm(_sigmoid(gd), g2_ref[...])
    kk = k * kk_ref[...]
    kk = kk * lax.rsqrt(seg_sum(kk * kk) + L2_EPS)
    k2 = k * (1.0 + (ag - 1.0) * ka_ref[...])
    aa_all = -kk
    bb_all = kk * ag
    npair = r.shape[1] // LANES

    c = CHUNK
    m0 = _iota2((c, LANES), 1) < half
    row = _iota2((c, LANES), 0)
    sub = _iota2((c, LANES), 1) % c
    strict = sub < row
    incl = sub <= row
    rb = _iota2((tb, tb), 0)
    cb = _iota2((tb, tb), 1)
    chunk_tri = ((cb <= rb) & ((cb // c) == (rb // c))).astype(F32)
    gc_all = _mm_sel(chunk_tri, lw)
    gx_all = gc_all - lw

    def stack(x):
        return jnp.concatenate([jnp.where(m0, x, 0.0), jnp.where(m0, 0.0, x)], axis=0)

    def unstack(x):
        return jnp.where(m0, x[0:c], x[c:2 * c])

    items = [(pi, ci) for pi in range(npair) for ci in range(nchunk)]
    a_bds, ak_xs, wmats, a_sts, v_sws, r_abss, nvs, bes, decays = [], [], [], [], [], [], [], [], []
    for pi, ci in items:
        sl = (slice(ci * c, (ci + 1) * c), slice(pi * LANES, (pi + 1) * LANES))
        rr, kc, vc, aa, bb = r[sl], k2[sl], v[sl], aa_all[sl], bb_all[sl]
        gc, gx = gc_all[sl], gx_all[sl]
        g_mid = gc[c // 2 - 1:c // 2]
        g_last = gc[c - 1:c]
        e_out = jnp.exp(g_mid - gc)
        at = aa * jnp.exp(gx - g_mid)
        rt = rr * jnp.exp(gc - g_mid)
        bt = bb * e_out
        kt = kc * e_out
        lhs0 = jnp.concatenate([jnp.where(m0, at, 0.0), jnp.where(m0, rt, 0.0)], axis=0)
        lhs1 = jnp.concatenate([jnp.where(m0, 0.0, at), jnp.where(m0, 0.0, rt)], axis=0)
        res0 = _mm(lhs0, jnp.concatenate([bt, kt], axis=0), NT)
        res1 = _mm(lhs1, jnp.concatenate([kt, bt], axis=0), NT)
        top0 = jnp.where(strict, res0[0:c], 0.0)
        top1 = jnp.where(strict, res1[0:c], 0.0)
        a_bds.append(jnp.concatenate([jnp.where(m0, top0, 0.0), jnp.where(m0, 0.0, top1)], axis=0))
        ak_xs.append(jnp.concatenate([jnp.where(m0, 0.0, top0), jnp.where(m0, top1, 0.0)], axis=0))
        wmats.append(jnp.concatenate([jnp.where(incl, res0[c:2 * c], 0.0),
                                      jnp.where(incl, res1[c:2 * c], 0.0)], axis=0))
        a_sts.append(stack(aa * jnp.exp(gx)))
        v_sws.append(jnp.concatenate([jnp.where(m0, 0.0, vc), jnp.where(m0, vc, 0.0)], axis=0))
        r_abss.append(rr * jnp.exp(gc))
        e_end = jnp.exp(g_last - gc)
        bes.append(bb * e_end)
        nvs.append((vc, kc * e_end))
        decays.append(jnp.exp(g_last))
    tinvs = _neumann_inverses(a_bds)

    n_items = len(items)
    rhs_vs = [_mm(ak_xs[i], v_sws[i]) for i in range(n_items)]
    uws = [_mm(tinvs[i], jnp.concatenate([rhs_vs[i], a_sts[i]], axis=1)) for i in range(n_items)]
    yws = [_mm(wmats[i], jnp.concatenate([uws[i][:, 0:LANES] + v_sws[i], uws[i][:, LANES:2 * LANES]], axis=1))
           for i in range(n_items)]
    y0s = [unstack(yws[i][:, 0:LANES]) for i in range(n_items)]
    qps = [(unstack(yws[i][:, LANES:2 * LANES]) + r_abss[i]).astype(BF16) for i in range(n_items)]
    gbds = [jnp.where(block_diag, _mm(unstack(uws[i][:, LANES:2 * LANES]), bes[i], TN), 0.0).astype(BF16)
            for i in range(n_items)]
    ncs = [jnp.where(block_diag, _mm(jnp.concatenate([unstack(uws[i][:, 0:LANES]), nvs[i][0]], axis=0),
                                     jnp.concatenate([bes[i], nvs[i][1]], axis=0), TN), 0.0)
           for i in range(n_items)]

    hts = [st_ref[pi] for pi in range(npair)]
    ys = [[] for _ in range(npair)]
    for ci in range(nchunk):
        for pi in range(npair):
            i = pi * nchunk + ci
            htb = hts[pi].astype(BF16)
            ys[pi].append(y0s[i] + _mm(qps[i], htb, NT))
            hts[pi] = hts[pi] * decays[i] + _mm(htb, gbds[i]) + ncs[i]
    for pi in range(npair):
        st_ref[pi] = hts[pi]

    ycols = [jnp.concatenate(ys[pi], axis=0) for pi in range(npair)]
    y = ycols[0] if npair == 1 else jnp.concatenate(ycols, axis=1)
    mean = seg_sum(y) * (1.0 / half)
    yc = y - mean
    var = seg_sum(yc * yc) * (1.0 / half)
    yn = yc * lax.rsqrt(var + RWKV_GN_EPS) * gw_ref[...] + gb_ref[...]
    bonus = seg_sum(r * k2 * rk_ref[...]) * v
    o_ref[0] = ((yn + bonus) * gate).astype(o_ref.dtype)


def _rwkv(proj, small, mu_main, mu_small, w0, w2p, a0, a2p, g2, kk, ka, rk, gw, gb, lora, tb=256, npair=8):
    b, t, _ = proj.shape
    w = LANES * npair
    groups = GROUP_W // w
    ws = small.shape[2]
    glora = g2.shape[0]

    def col(section):
        return lambda bi, hi, ti: (bi, ti, section * groups + hi)

    def vec(section=0):
        return pl.BlockSpec((1, w), lambda bi, hi, ti: (0, section * groups + hi))

    row_blk = lambda s: pl.BlockSpec((1, tb, w), col(s))
    return pl.pallas_call(
        functools.partial(_rwkv_kernel, nchunk=tb // CHUNK, lora=lora),
        out_shape=jax.ShapeDtypeStruct((b, t, GROUP_W), BF16),
        grid=(b, groups, t // tb),
        in_specs=[row_blk(0), row_blk(1), row_blk(2),
                  pl.BlockSpec((1, tb, ws), lambda bi, hi, ti: (bi, ti, 0)),
                  vec(0), vec(1), vec(2),
                  pl.BlockSpec((1, ws), lambda bi, hi, ti: (0, 0)),
                  vec(), pl.BlockSpec((lora, w), lambda bi, hi, ti: (0, hi)),
                  vec(), pl.BlockSpec((lora, w), lambda bi, hi, ti: (0, hi)),
                  pl.BlockSpec((glora, w), lambda bi, hi, ti: (0, hi)),
                  vec(), vec(), vec(), vec(), vec()],
        out_specs=pl.BlockSpec((1, tb, w), lambda bi, hi, ti: (bi, ti, hi)),
        scratch_shapes=[pltpu.VMEM((npair, LANES, LANES), F32)]
        + [pltpu.VMEM((SUBLANES, w), F32)] * 3
        + [pltpu.VMEM((SUBLANES, ws), F32)],
        compiler_params=_cparams(("parallel", "parallel", "arbitrary")),
        name="rwkv7_chunk",
    )(proj, proj, proj, small, mu_main, mu_main, mu_main, mu_small, w0, w2p, a0, a2p, g2,
      kk, ka, rk, gw, gb)


def _pad_cols(a, width):
    return jnp.pad(a, ((0, 0), (0, width - a.shape[1])))


def _pad_rows(a, height):
    return jnp.pad(a, ((0, height - a.shape[0]), (0, 0)))


def _row(v):
    return v.reshape(1, -1).astype(F32)


def kernel(x, p, hgrn_lb_logits, e_w_in, e_gla_w2, e_gla_b, e_hgrn_norm, e_gla_norm, e_w_out, o_w_in, o_rwkv_mu, o_rwkv_w0, o_rwkv_w2, o_rwkv_a0, o_rwkv_a2, o_rwkv_g2, o_rwkv_kk, o_rwkv_ka, o_rwkv_rk, o_rwkv_gn_w, o_rwkv_gn_b, o_gdn_conv, o_gdn_a_log, o_gdn_dt_bias, o_gdn_norm, o_w_out, ln_mix_w, ln_mix_b, ln_ffn_w, ln_ffn_b, ffn_w_up, ffn_conv, ffn_w_down, ple_w_proj, ple_w_gate):
    bsz, seq, d = x.shape
    m = bsz * seq
    depth = ln_mix_w.shape[0]
    alpha = (2 * depth) ** 0.25
    gw = GROUP_W
    x32 = x.reshape(m, d)
    xb = x32.astype(BF16)
    w_down = ffn_w_down.astype(BF16)
    for layer in range(depth):
        j = layer // 2
        if layer % 2 == 0:
            rank = e_gla_w2.shape[1]
            gd0 = 4 * gw + 2 * GLA_HEADS * GLA_DK + gw
            wt = jnp.swapaxes(e_w_in, 1, 2)
            w_tail = jnp.concatenate([wt[j, gd0 + rank:], _pad_rows(wt[j, gd0:gd0 + rank], LANES)],
                                     axis=0)
            proj = _proj(xb, wt, gd0, layer=j).reshape(bsz, seq, -1)
            tail = _proj(xb, w_tail, w_tail.shape[0], tn=w_tail.shape[0]).reshape(bsz, seq, -1)
            o_a = _hgrn2(proj, hgrn_lb_logits.astype(F32), _row(e_hgrn_norm[j]), layer)
            o_b = _gla(proj, tail, _pad_rows(e_gla_w2[j], LANES), _row(e_gla_b[j]), _row(e_gla_norm[j]))
            w_out = e_w_out
        else:
            wt = jnp.swapaxes(o_w_in, 1, 2)
            lora_w = o_rwkv_w2.shape[1]
            lora_a = o_rwkv_a2.shape[1]
            lora_g = o_rwkv_g2.shape[1]
            lora = LANES * (-(-max(lora_w, lora_a) // LANES))
            c_wd = 3 * gw
            c_ad = c_wd + lora_w
            c_gd = c_ad + lora_a
            c_qkv = c_gd + lora_g
            c_z = c_qkv + 3 * gw
            c_ab = c_z + gw
            w_small = jnp.concatenate([_pad_rows(wt[j, c_wd:c_ad], lora), _pad_rows(wt[j, c_ad:c_gd], lora),
                                       wt[j, c_gd:c_qkv], _pad_rows(wt[j, c_ab:], LANES)], axis=0)
            mu = o_rwkv_mu[j].reshape(1, -1)
            mu_small = jnp.concatenate([_pad_cols(mu[:, c_wd:c_ad], lora), _pad_cols(mu[:, c_ad:c_gd], lora),
                                        mu[:, c_gd:c_qkv], jnp.zeros((1, LANES), F32)], axis=1)
            proj_rkv = _proj(xb, wt, c_wd, layer=j).reshape(bsz, seq, -1)
            proj_gdn = _proj(xb, wt, c_ab - c_qkv, layer=j, row0=c_qkv).reshape(bsz, seq, -1)
            small = _proj(xb, w_small, w_small.shape[0], tn=w_small.shape[0]).reshape(bsz, seq, -1)
            o_a = _rwkv(proj_rkv, small, mu[:, :c_wd], mu_small, _row(o_rwkv_w0[j]),
                        _pad_rows(o_rwkv_w2[j], lora), _row(o_rwkv_a0[j]), _pad_rows(o_rwkv_a2[j], lora),
                        o_rwkv_g2[j], _row(o_rwkv_kk[j]), _row(o_rwkv_ka[j]), _row(o_rwkv_rk[j]),
                        _row(o_rwkv_gn_w[j]), _row(o_rwkv_gn_b[j]), lora)
            o_b = _gdn(proj_gdn, small, o_gdn_conv[j].astype(F32),
                       _pad_cols(_row(o_gdn_a_log[j]), LANES), _pad_cols(_row(o_gdn_dt_bias[j]), LANES),
                       _row(o_gdn_norm[j]))
            w_out = o_w_out
        x32, xb = _mix_out(o_a.reshape(m, gw), o_b.reshape(m, gw), w_out, x32,
                           _row(ln_mix_w[layer]), _row(ln_mix_b[layer]), alpha, j)
        h = _ffn_up(xb, ffn_w_up, ffn_conv, layer, seq)
        x32, xb = _ffn_down(h, w_down, x32, _row(ln_ffn_w[layer]), _row(ln_ffn_b[layer]), alpha, layer)
        x32, xb = _ple(xb, x32, p.reshape(depth, m, -1), ple_w_gate, ple_w_proj, layer)
    return x32.reshape(bsz, seq, d)
```
